```python
import math
import jax, jax.numpy as jnp
from jax import lax
import numpy as np

D_MODEL = 1024
BATCH = 32
SEQ = 2048
DEPTH = 4

GRID_W = 64
CTX_LEN = 256
HEAD_DIM = 64
ROPE_THETA = 10000.0
EPS = 1e-6
Q_BLOCK = 128
A_Q_HEADS = 8
A_KV_HEADS = 2
A_WIDTH = A_Q_HEADS * HEAD_DIM
A_KV_WIDTH = A_KV_HEADS * HEAD_DIM
B_WIDTH = D_MODEL // 2
B_WINDOWS = (2, 4, 8, 16)
B_GROUPS = len(B_WINDOWS)
B_GROUP_W = B_WIDTH // B_GROUPS
EVEN_IN = A_WIDTH + 2 * A_KV_WIDTH + B_WIDTH
EVEN_MIX = A_WIDTH + B_WIDTH
C_Q_HEADS = 16
C_KV_HEADS = 4
C_WIDTH = C_Q_HEADS * HEAD_DIM
C_KV_WIDTH = C_KV_HEADS * HEAD_DIM
ODD_IN = C_WIDTH + 2 * C_KV_WIDTH
WINDOW = 128
N_SIDE = WINDOW // Q_BLOCK
BAND_BLOCKS = 2 * N_SIDE + 1
BAND_LEN = BAND_BLOCKS * Q_BLOCK
FFN_HIDDEN = int(math.ceil(8 * D_MODEL / 3 / 256)) * 256
N_EVEN = (DEPTH + 1) // 2
N_ODD = DEPTH // 2

kernel_name = "hybrid_dit_gqa_pool_swa_prefix"


def rmsnorm(x, g):
    xf = x.astype(jnp.float32)
    y = xf * lax.rsqrt(jnp.mean(xf * xf, axis=-1, keepdims=True) + EPS)
    return (y * g.astype(jnp.float32)).astype(x.dtype)


def modulate(x, g, shift, scale):
    return rmsnorm(x, g) * (1 + scale) + shift


def split_heads(t, h):
    return t.reshape(t.shape[0], t.shape[1], h, HEAD_DIM)


def axial_rope(x, rows, cols):
    hd = x.shape[-1]
    quarter = hd // 4
    freqs = ROPE_THETA ** (-jnp.arange(quarter, dtype=jnp.float32) / quarter)

    def rot(a, pos):
        ang = pos.astype(jnp.float32)[:, None] * freqs[None, :]
        cos = jnp.cos(ang)[None, :, None, :]
        sin = jnp.sin(ang)[None, :, None, :]
        a1, a2 = a[..., :quarter], a[..., quarter:]
        return jnp.concatenate([a1 * cos - a2 * sin, a2 * cos + a1 * sin], axis=-1)

    out = jnp.concatenate([rot(x[..., :hd // 2], rows), rot(x[..., hd // 2:], cols)], axis=-1)
    return out.astype(x.dtype)


def dense_gqa_blocks(q, k, v):
    B, S, H, hd = q.shape
    KV = k.shape[2]
    G = H // KV
    nb = S // Q_BLOCK
    scale = 1.0 / math.sqrt(hd)
    qb = q.reshape(B, nb, Q_BLOCK, KV, G, hd).transpose(1, 0, 2, 3, 4, 5)

    def one(qblk):
        s = jnp.einsum('bqkgd,btkd->bkgqt', qblk, k, preferred_element_type=jnp.float32) * scale
        p = jax.nn.softmax(s, axis=-1).astype(v.dtype)
        return jnp.einsum('bkgqt,btkd->bqkgd', p, v)

    o = lax.map(one, qb)
    return o.transpose(1, 0, 2, 3, 4, 5).reshape(B, S, H * hd)


def ctx_attention(q, k, v, sink=None):
    B, T, H, hd = q.shape
    KV = k.shape[2]
    G = H // KV
    scale = 1.0 / math.sqrt(hd)
    qg = q.reshape(B, T, KV, G, hd)
    s = jnp.einsum('bqkgd,btkd->bkgqt', qg, k, preferred_element_type=jnp.float32) * scale
    if sink is not None:
        s_sink = jnp.broadcast_to(sink.reshape(KV, G)[None, :, :, None, None].astype(jnp.float32),
                                  (B, KV, G, T, 1))
        s = jnp.concatenate([s, s_sink], axis=-1)
    p = jax.nn.softmax(s, axis=-1)
    if sink is not None:
        p = p[..., :-1]
    o = jnp.einsum('bkgqt,btkd->bqkgd', p.astype(v.dtype), v)
    return o.reshape(B, T, H * hd)


def window_gqa_blocks(q, k, v, k_ctx, v_ctx, sink):
    B, S, H, hd = q.shape
    KV = k.shape[2]
    G = H // KV
    C = k_ctx.shape[1]
    nb = S // Q_BLOCK
    scale = 1.0 / math.sqrt(hd)
    qb = q.reshape(B, nb, Q_BLOCK, KV, G, hd).transpose(1, 0, 2, 3, 4, 5)

    def band(t):
        tp = jnp.pad(t, ((0, 0), (WINDOW, WINDOW), (0, 0), (0, 0)))
        tp = tp.reshape(B, nb + 2 * N_SIDE, Q_BLOCK, KV, hd)
        tb = jnp.concatenate([tp[:, j:j + nb] for j in range(BAND_BLOCKS)], axis=2)
        return tb.transpose(1, 0, 2, 3, 4)

    kb, vb = band(k), band(v)
    n_i = jnp.arange(nb)[:, None, None]
    q_i = jnp.arange(Q_BLOCK)[None, :, None]
    k_j = jnp.arange(BAND_LEN)[None, None, :]
    qpos = n_i * Q_BLOCK + q_i
    kpos = n_i * Q_BLOCK - WINDOW + k_j
    mask = (kpos >= 0) & (kpos < S) & (jnp.abs(qpos - kpos) <= WINDOW)
    sink_l = sink.reshape(KV, G)[None, :, :, None, None].astype(jnp.float32)

    def one(args):
        qblk, kblk, vblk, m = args
        s_w = jnp.einsum('bqkgd,btkd->bkgqt', qblk, kblk, preferred_element_type=jnp.float32) * scale
        s_w = jnp.where(m[None, None, None], s_w, -jnp.inf)
        s_c = jnp.einsum('bqkgd,btkd->bkgqt', qblk, k_ctx, preferred_element_type=jnp.float32) * scale
        s_s = jnp.broadcast_to(sink_l, (B, KV, G, Q_BLOCK, 1))
        p = jax.nn.softmax(jnp.concatenate([s_w, s_c, s_s], axis=-1), axis=-1).astype(v.dtype)
        o = jnp.einsum('bkgqt,btkd->bqkgd', p[..., :BAND_LEN], vblk)
        o = o + jnp.einsum('bkgqt,btkd->bqkgd', p[..., BAND_LEN:BAND_LEN + C], v_ctx)
        return o

    o = lax.map(one, (qb, kb, vb, mask))
    return o.transpose(1, 0, 2, 3, 4, 5).reshape(B, S, H * hd)


def centred_mean(u, w):
    S = u.shape[1]
    cs = jnp.pad(jnp.cumsum(u.astype(jnp.float32), axis=1), ((0, 0), (1, 0), (0, 0)))
    t = jnp.arange(S)
    lo = jnp.clip(t - w // 2, 0, S)
    hi = jnp.clip(t + w - w // 2, 0, S)
    cnt = (hi - lo).astype(jnp.float32)
    return ((cs[:, hi] - cs[:, lo]) / cnt[None, :, None]).astype(u.dtype)


def pool_mixer(u, w_pool, pool_scale):
    B, S, _ = u.shape
    diffs = []
    for g, w in enumerate(B_WINDOWS):
        ug = u[..., g * B_GROUP_W:(g + 1) * B_GROUP_W]
        diffs.append(centred_mean(ug, w) - ug)
    d = jnp.stack(diffs, axis=2)
    y = jnp.einsum('bsgc,gcd->bsgd', d, w_pool).reshape(B, S, B_WIDTH)
    return y * pool_scale


def even_mixer(h, hc, w_in, w_out, q_gain, k_gain, w_pool, pool_scale, rows, cols, with_ctx):
    def project(t):
        p = t @ w_in
        q, k, v, u = jnp.split(p, [A_WIDTH, A_WIDTH + A_KV_WIDTH, A_WIDTH + 2 * A_KV_WIDTH], axis=-1)
        q = rmsnorm(split_heads(q, A_Q_HEADS), q_gain)
        k = rmsnorm(split_heads(k, A_KV_HEADS), k_gain)
        return q, k, split_heads(v, A_KV_HEADS), u

    q, k, v, u = project(h)
    qc, kc, vc, uc = project(hc)
    q = axial_rope(q, rows, cols)
    k = axial_rope(k, rows, cols)
    a = dense_gqa_blocks(q, jnp.concatenate([k, kc], axis=1), jnp.concatenate([v, vc], axis=1))
    b = pool_mixer(u, w_pool, pool_scale)
    y = jnp.concatenate([a, b], axis=-1) @ w_out
    yc = None
    if with_ctx:
        ac = ctx_attention(qc, kc, vc)
        bc = pool_mixer(uc, w_pool, pool_scale)
        yc = jnp.concatenate([ac, bc], axis=-1) @ w_out
    return y, yc


def odd_mixer(h, hc, w_in, w_out, sink, rows, cols, with_ctx):
    def project(t):
        p = t @ w_in
        q, k, v = jnp.split(p, [C_WIDTH, C_WIDTH + C_KV_WIDTH], axis=-1)
        return split_heads(q, C_Q_HEADS), split_heads(k, C_KV_HEADS), split_heads(v, C_KV_HEADS)

    q, k, v = project(h)
    qc, kc, vc = project(hc)
    q = axial_rope(q, rows, cols)
    k = axial_rope(k, rows, cols)
    y = window_gqa_blocks(q, k, v, kc, vc, sink) @ w_out
    yc = None
    if with_ctx:
        yc = ctx_attention(qc, kc, vc, sink) @ w_out
    return y, yc


def swiglu(h, w_in, w_out):
    g, u = jnp.split(h @ w_in, 2, axis=-1)
    return (jax.nn.silu(g) * u) @ w_out


def setup_inputs(seed: int = 0) -> dict:
    key = jax.random.key(seed)
    ks = jax.random.split(key, 24)
    D = D_MODEL

    def nrm(k, shape, std):
        return jax.random.normal(k, shape, dtype=jnp.float32) * std

    return {
        "x": nrm(ks[0], (BATCH, SEQ, D), 1.0),
        "c": nrm(ks[1], (BATCH, D), 1.0),
        "ctx": nrm(ks[2], (BATCH, CTX_LEN, D), 1.0),
        "c_ctx": nrm(ks[3], (D,), 1.0),
        "w_mod": nrm(ks[4], (DEPTH, D, 6 * D), 0.5 * D ** -0.5),
        "b_mod": nrm(ks[5], (DEPTH, 6 * D), 0.02),
        "g_pre_mix": 1.0 + nrm(ks[6], (DEPTH, D), 0.05),
        "g_post_mix": 1.0 + nrm(ks[7], (DEPTH, D), 0.05),
        "g_pre_ffn": 1.0 + nrm(ks[8], (DEPTH, D), 0.05),
        "g_post_ffn": 1.0 + nrm(ks[9], (DEPTH, D), 0.05),
        "we_in": nrm(ks[10], (N_EVEN, D, EVEN_IN), D ** -0.5),
        "we_out": nrm(ks[11], (N_EVEN, EVEN_MIX, D), EVEN_MIX ** -0.5),
        "we_q_gain": 1.0 + nrm(ks[12], (N_EVEN, HEAD_DIM), 0.05),
        "we_k_gain": 1.0 + nrm(ks[13], (N_EVEN, HEAD_DIM), 0.05),
        "we_pool": nrm(ks[14], (N_EVEN, B_GROUPS, B_GROUP_W, B_GROUP_W), B_GROUP_W ** -0.5),
        "we_pool_scale": 1.0 + nrm(ks[15], (N_EVEN, B_WIDTH), 0.1),
        "wo_in": nrm(ks[16], (N_ODD, D, ODD_IN), D ** -0.5),
        "wo_out": nrm(ks[17], (N_ODD, C_WIDTH, D), C_WIDTH ** -0.5),
        "wo_sink": nrm(ks[18], (N_ODD, C_Q_HEADS), 0.5),
        "w_ffn_in": nrm(ks[19], (DEPTH, D, 2 * FFN_HIDDEN), D ** -0.5),
        "w_ffn_out": nrm(ks[20], (DEPTH, FFN_HIDDEN, D), FFN_HIDDEN ** -0.5),
    }


def reference(x, c, ctx, c_ctx, w_mod, b_mod, g_pre_mix, g_post_mix, g_pre_ffn, g_post_ffn,
              we_in, we_out, we_q_gain, we_k_gain, we_pool, we_pool_scale,
              wo_in, wo_out, wo_sink, w_ffn_in, w_ffn_out):
    S = x.shape[1]
    ROWS = S // GRID_W
    rows = jnp.repeat(jnp.arange(ROWS, dtype=jnp.int32), GRID_W)
    cols = jnp.tile(jnp.arange(GRID_W, dtype=jnp.int32), ROWS)
    silu_c = jax.nn.silu(c)
    silu_cc = jax.nn.silu(c_ctx)

    for l in range(DEPTH):
        with_ctx = l < DEPTH - 1
        mod = (silu_c @ w_mod[l] + b_mod[l])[:, None, :]
        mod_c = (silu_cc @ w_mod[l] + b_mod[l])[None, None, :]
        sh_m, sc_m, gt_m, sh_f, sc_f, gt_f = jnp.split(mod, 6, axis=-1)
        csh_m, csc_m, cgt_m, csh_f, csc_f, cgt_f = jnp.split(mod_c, 6, axis=-1)

        h = modulate(x, g_pre_mix[l], sh_m, sc_m)
        hc = modulate(ctx, g_pre_mix[l], csh_m, csc_m)
        i = l // 2
        if l % 2 == 0:
            y, yc = even_mixer(h, hc, we_in[i], we_out[i], we_q_gain[i], we_k_gain[i],
                               we_pool[i], we_pool_scale[i], rows, cols, with_ctx)
        else:
            y, yc = odd_mixer(h, hc, wo_in[i], wo_out[i], wo_sink[i], rows, cols, with_ctx)

        x = x + gt_m * rmsnorm(y, g_post_mix[l])
        h = modulate(x, g_pre_ffn[l], sh_f, sc_f)
        x = x + gt_f * rmsnorm(swiglu(h, w_ffn_in[l], w_ffn_out[l]), g_post_ffn[l])

        if with_ctx:
            ctx = ctx + cgt_m * rmsnorm(yc, g_post_mix[l])
            hc = modulate(ctx, g_pre_ffn[l], csh_f, csc_f)
            ctx = ctx + cgt_f * rmsnorm(swiglu(hc, w_ffn_in[l], w_ffn_out[l]), g_post_ffn[l])
    return x
```

```python
import functools
import math

import numpy as np
import jax
import jax.numpy as jnp
from jax import lax
from jax.experimental import pallas as pl
from jax.experimental.pallas import tpu as pltpu

D_MODEL = 1024
HEAD_DIM = 64
GRID_W = 64
ROPE_THETA = 10000.0
EPS = 1e-6
WINDOW = 128
POOL_WINDOWS = (2, 4, 8, 16)
POOL_HALO = 8
FFN_HIDDEN = 2816
FFN_CHUNK = 256
LANES = 128
QK_SCALE = 1.0 / math.sqrt(HEAD_DIM)
VMEM_LIMIT = 56 * 1024 * 1024

F32 = jnp.float32
BF16 = jnp.bfloat16


def _lane_slot_dim():
    lane = np.arange(LANES)
    part = lane // 32
    i = lane % 32
    dim = np.where(i < 16, i, 32 + (i - 16)) + np.where(part >= 2, 16, 0)
    return part % 2, dim


def _q_cols(n_heads, base=0):
    slot, dim = _lane_slot_dim()
    return np.concatenate([base + (2 * j + slot) * HEAD_DIM + dim for j in range(n_heads // 2)])


def _k_cols(n_kv, base):
    _, dim = _lane_slot_dim()
    return np.concatenate([base + kv * HEAD_DIM + dim for kv in range(n_kv)])


def _out_rows(n_heads):
    rows = []
    for c in range(n_heads // 2):
        lo = (c // 4) * 8 + c % 4
        for head in (lo, lo + 4):
            rows.append(head * HEAD_DIM + np.arange(HEAD_DIM))
    return np.concatenate(rows)


def _ffn_in_cols():
    cols = []
    for i in range(FFN_HIDDEN // FFN_CHUNK):
        cols.append(np.arange(i * FFN_CHUNK, (i + 1) * FFN_CHUNK))
        cols.append(FFN_HIDDEN + np.arange(i * FFN_CHUNK, (i + 1) * FFN_CHUNK))
    return np.concatenate(cols)


def _head_mean_matrix():
    slot, _ = _lane_slot_dim()
    return (slot[:, None] == slot[None, :]).astype(np.float32) / HEAD_DIM


def _rope_tables(seq):
    quarter = HEAD_DIM // 4
    freqs = ROPE_THETA ** (-jnp.arange(quarter, dtype=F32) / quarter)
    t = jnp.arange(seq, dtype=jnp.int32)
    rows = (t // GRID_W).astype(F32)[:, None] * freqs[None, :]
    cols = (t % GRID_W).astype(F32)[:, None] * freqs[None, :]
    ang = jnp.concatenate([rows, cols], axis=-1)
    cos = jnp.tile(jnp.cos(ang), (1, 4))
    sin = jnp.sin(ang)
    sin = jnp.concatenate([-sin, -sin, sin, sin], axis=-1)
    return cos, sin


def _rms_normalise(x):
    return x * lax.rsqrt(jnp.mean(x * x, axis=-1, keepdims=True) + EPS)


def _silu(x):
    return x * (1.0 / (1.0 + jnp.exp(-x)))


def _dot(a, b):
    return jnp.dot(a, b, preferred_element_type=F32)


def _const_spec(shape):
    return pl.BlockSpec(shape, lambda *_: (0,) * len(shape), pipeline_mode=pl.Buffered(1))


def _mod_kernel(c_ref, w_ref, b_ref, o_ref):
    s = _silu(c_ref[...]).astype(BF16)
    o_ref[0] = _dot(s, w_ref[0].astype(BF16)) + b_ref[0]


def _modulation(cc, w_mod, b_mod):
    depth, d, n = w_mod.shape
    rows = cc.shape[0]
    tn = 1536
    return pl.pallas_call(
        _mod_kernel,
        grid=(depth, n // tn),
        in_specs=[
            pl.BlockSpec((rows, d), lambda l, j: (0, 0)),
            pl.BlockSpec((1, d, tn), lambda l, j: (l, 0, j)),
            pl.BlockSpec((1, 1, tn), lambda l, j: (l, 0, j)),
        ],
        out_specs=pl.BlockSpec((1, rows, tn), lambda l, j: (l, 0, j)),
        out_shape=jax.ShapeDtypeStruct((depth, rows, n), F32),
        compiler_params=pltpu.CompilerParams(
            dimension_semantics=("parallel", "parallel"), vmem_limit_bytes=VMEM_LIMIT),
        name="modulation",
    )(cc, w_mod, b_mod.reshape(depth, 1, n))


def _pre_kernel(*refs, n_q, n_k, n_v, n_u, qk_norm, rope):
    x_ref, mod_ref, gains_ref, w_ref = refs[:4]
    pos = 4
    if qk_norm:
        pm_ref, qg_ref, kg_ref = refs[pos:pos + 3]
        pos += 3
    if rope:
        cos_ref, sin_ref = refs[pos:pos + 2]
        pos += 2
    q_ref, k_ref, v_ref = refs[pos:pos + 3]
    u_ref = refs[pos + 3] if n_u else None

    x = x_ref[0]
    h = _rms_normalise(x) * gains_ref[0:1, :]
    h = h * (1.0 + mod_ref[0, 1:2, :]) + mod_ref[0, 0:1, :]
    p = _dot(h.astype(BF16), w_ref[...])

    def head_chunk(c, gain_ref, scale):
        if qk_norm:
            c2 = c * c
            hi = c2.astype(BF16)
            lo = (c2 - hi.astype(F32)).astype(BF16)
            ms = _dot(hi, pm_ref[...]) + _dot(lo, pm_ref[...])
            c = c * lax.rsqrt(ms + EPS) * gain_ref[...]
        if rope:
            c = c * cos_ref[...] + pltpu.roll(c, LANES // 2, 1) * sin_ref[...]
        if scale != 1.0:
            c = c * scale
        return c.astype(BF16)

    for j in range(n_q // LANES):
        sl = slice(j * LANES, (j + 1) * LANES)
        q_ref[0, :, sl] = head_chunk(p[:, sl], qg_ref if qk_norm else None, QK_SCALE)
    for j in range(n_k // LANES):
        sl = slice(j * LANES, (j + 1) * LANES)
        k_ref[0, :, sl] = head_chunk(p[:, n_q + j * LANES:n_q + (j + 1) * LANES],
                                     kg_ref if qk_norm else None, 1.0)
    v_ref[0] = p[:, n_q + n_k:n_q + n_k + n_v].astype(BF16)
    if n_u:
        u_ref[0] = p[:, n_q + n_k + n_v:]


def _pre(x, mod, gains, w, norm_args, rope_args, *, widths, tm):
    n_q, n_k, n_v, n_u = widths
    b, s, d = x.shape
    per_batch_mod = mod.shape[0] > 1
    qk_norm = norm_args is not None
    rope = rope_args is not None
    in_specs = [
        pl.BlockSpec((1, tm, d), lambda i, t: (i, t, 0)),
        pl.BlockSpec((1, 8, d), (lambda i, t: (i, 0, 0)) if per_batch_mod else (lambda i, t: (0, 0, 0))),
        _const_spec(gains.shape),
        _const_spec(w.shape),
    ]
    args = [x, mod, gains, w]
    if qk_norm:
        in_specs += [_const_spec(a.shape) for a in norm_args]
        args += list(norm_args)
    if rope:
        in_specs += [pl.BlockSpec((tm, LANES), lambda i, t: (t, 0))] * 2
        args += list(rope_args)
    out_shape = [jax.ShapeDtypeStruct((b, s, n_q), BF16), jax.ShapeDtypeStruct((b, s, n_k), BF16),
                 jax.ShapeDtypeStruct((b, s, n_v), BF16)]
    out_specs = [pl.BlockSpec((1, tm, n), lambda i, t: (i, t, 0)) for n in (n_q, n_k, n_v)]
    if n_u:
        out_shape.append(jax.ShapeDtypeStruct((b, s, n_u), F32))
        out_specs.append(pl.BlockSpec((1, tm, n_u), lambda i, t: (i, t, 0)))
    return pl.pallas_call(
        functools.partial(_pre_kernel, n_q=n_q, n_k=n_k, n_v=n_v, n_u=n_u, qk_norm=qk_norm, rope=rope),
        grid=(b, s // tm),
        in_specs=in_specs,
        out_specs=out_specs,
        out_shape=out_shape,
        compiler_params=pltpu.CompilerParams(
            dimension_semantics=("parallel", "parallel"), vmem_limit_bytes=VMEM_LIMIT),
        name="pre_even" if qk_norm else "pre_odd",
    )(*args)


def _attn_kernel(*refs, mode, n_kv, tq, seq, use_sink):
    q_ref = refs[0]
    pos = 1
    if mode != "ctx":
        k_ref, v_ref = refs[pos:pos + 2]
        pos += 2
    kc_ref, vc_ref = refs[pos:pos + 2]
    pos += 2
    if use_sink:
        sink_ref = refs[pos]
        pos += 1
    o_ref = refs[pos]

    lane = lax.broadcasted_iota(jnp.int32, (1, LANES), 1)
    slot_a = (lane // 32) % 2 == 0
    low_half = lane < LANES // 2

    if mode == "window":
        band = tq + 2 * WINDOW
        q0 = pl.program_id(1) * tq
        start = pl.multiple_of(jnp.clip(q0 - WINDOW, 0, seq - band), LANES)
        qpos = q0 + lax.broadcasted_iota(jnp.int32, (tq, band), 0)
        kpos = start + lax.broadcasted_iota(jnp.int32, (tq, band), 1)
        in_band = jnp.abs(qpos - kpos) <= WINDOW
        in_band = jnp.concatenate([in_band] * 4, axis=0)

    outs = []
    for g in range(n_kv):
        ksl = slice(g * LANES, (g + 1) * LANES)
        vsl = slice((g // 2) * LANES, (g // 2 + 1) * LANES)
        qa = q_ref[0, :, 2 * g * LANES:(2 * g + 1) * LANES]
        qb = q_ref[0, :, (2 * g + 1) * LANES:(2 * g + 2) * LANES]
        zero = jnp.zeros_like(qa)
        qs = jnp.concatenate([jnp.where(slot_a, qa, zero), jnp.where(slot_a, zero, qa),
                              jnp.where(slot_a, qb, zero), jnp.where(slot_a, zero, qb)], axis=0)

        segs = []
        if mode == "dense":
            segs.append((k_ref[0, :, ksl], v_ref[0, :, vsl], None))
        elif mode == "window":
            segs.append((k_ref[0, pl.ds(start, band), ksl], v_ref[0, pl.ds(start, band), vsl], in_band))
        segs.append((kc_ref[0, :, ksl], vc_ref[0, :, vsl], None))

        scores = []
        for keys, _, mask in segs:
            s = lax.dot_general(qs, keys, (((1,), (1,)), ((), ())), preferred_element_type=F32)
            if mask is not None:
                s = jnp.where(mask, s, -jnp.inf)
            scores.append(s)
        m = functools.reduce(jnp.maximum, [s.max(axis=-1, keepdims=True) for s in scores])
        if use_sink:
            sink = jnp.concatenate(
                [jnp.broadcast_to(sink_ref[4 * g + i:4 * g + i + 1, 0:1], (tq, 1)) for i in range(4)], axis=0)
            m = jnp.maximum(m, sink)
        denom = jnp.exp(sink - m) if use_sink else None
        o = None
        for s, (_, values, _) in zip(scores, segs):
            p = jnp.exp(s - m)
            row = p.sum(axis=-1, keepdims=True)
            denom = row if denom is None else denom + row
            pv = _dot(p.astype(BF16), values)
            o = pv if o is None else o + pv
        outs.append(o / denom)

    for c in range(2 * n_kv):
        blk, r = c // 4, c % 4
        lo = outs[2 * blk][r * tq:(r + 1) * tq]
        hi = outs[2 * blk + 1][r * tq:(r + 1) * tq]
        o_ref[0, :, c * LANES:(c + 1) * LANES] = jnp.where(low_half, lo, hi).astype(BF16)


def _attention(q, kv, kv_ctx, sink, *, mode, tq):
    b, sq, wq = q.shape
    n_kv = wq // (2 * LANES)
    kc, vc = kv_ctx
    c_len = kc.shape[1]
    use_sink = sink is not None
    in_specs = [pl.BlockSpec((1, tq, wq), lambda i, t: (i, t, 0))]
    args = [q]
    seq = sq
    if mode != "ctx":
        k, v = kv
        seq = k.shape[1]
        in_specs += [pl.BlockSpec((1, seq, k.shape[2]), lambda i, t: (i, 0, 0)),
                     pl.BlockSpec((1, seq, v.shape[2]), lambda i, t: (i, 0, 0))]
        args += [k, v]
    in_specs += [pl.BlockSpec((1, c_len, kc.shape[2]), lambda i, t: (i, 0, 0)),
                 pl.BlockSpec((1, c_len, vc.shape[2]), lambda i, t: (i, 0, 0))]
    args += [kc, vc]
    if use_sink:
        in_specs.append(_const_spec(sink.shape))
        args.append(sink)
    return pl.pallas_call(
        functools.partial(_attn_kernel, mode=mode, n_kv=n_kv, tq=tq, seq=seq, use_sink=use_sink),
        grid=(b, sq // tq),
        in_specs=in_specs,
        out_specs=pl.BlockSpec((1, tq, wq), lambda i, t: (i, t, 0)),
        out_shape=jax.ShapeDtypeStruct((b, sq, wq), BF16),
        compiler_params=pltpu.CompilerParams(
            dimension_semantics=("parallel", "parallel"), vmem_limit_bytes=VMEM_LIMIT),
        name="attn_" + mode,
    )(*args)


def _post_kernel(*refs, pool, tm, seq):
    x_ref, a_ref, mod_ref, gains_ref, wo_ref, wfi_ref, wfo_ref = refs[:7]
    pos = 7
    if pool:
        u_ref, uprev_ref, unext_ref, wpool_ref, pscale_ref = refs[pos:pos + 5]
        pos += 5
    o_ref = refs[pos]
    hid_ref = refs[pos + 1]
    if pool:
        ext_ref = refs[pos + 2]

    a = a_ref[0]
    if pool:
        t = pl.program_id(1)
        last = pl.num_programs(1) - 1
        halo = jnp.zeros((POOL_HALO, ext_ref.shape[1]), F32)
        ext_ref[0:POOL_HALO, :] = jnp.where(t > 0, uprev_ref[0], halo)
        ext_ref[POOL_HALO:POOL_HALO + tm, :] = u_ref[0]
        ext_ref[POOL_HALO + tm:, :] = jnp.where(t < last, unext_ref[0], halo)
        tok = t * tm + lax.broadcasted_iota(jnp.int32, (tm, 1), 0)
        mixed = []
        for g, w in enumerate(POOL_WINDOWS):
            sl = slice(g * LANES, (g + 1) * LANES)
            acc = None
            for j in range(-(w // 2), w - w // 2):
                term = ext_ref[POOL_HALO + j:POOL_HALO + j + tm, sl]
                acc = term if acc is None else acc + term
            cnt = jnp.minimum(tok + (w - w // 2), seq) - jnp.maximum(tok - w // 2, 0)
            diff = acc / cnt.astype(F32) - ext_ref[POOL_HALO:POOL_HALO + tm, sl]
            mixed.append((_dot(diff.astype(BF16), wpool_ref[g]) * pscale_ref[:, sl]).astype(BF16))
        a = jnp.concatenate([a] + mixed, axis=-1)

    y = _dot(a, wo_ref[...])
    x1 = x_ref[0] + mod_ref[0, 2:3, :] * (_rms_normalise(y) * gains_ref[1:2, :])
    h = _rms_normalise(x1) * gains_ref[2:3, :]
    h = (h * (1.0 + mod_ref[0, 4:5, :]) + mod_ref[0, 3:4, :]).astype(BF16)
    for i in range(FFN_HIDDEN // FFN_CHUNK):
        gu = _dot(h, wfi_ref[:, 2 * i * FFN_CHUNK:2 * (i + 1) * FFN_CHUNK])
        gate, up = gu[:, :FFN_CHUNK], gu[:, FFN_CHUNK:]
        hid_ref[:, i * FFN_CHUNK:(i + 1) * FFN_CHUNK] = (_silu(gate) * up).astype(BF16)
    z = _dot(hid_ref[...], wfo_ref[...])
    o_ref[0] = x1 + mod_ref[0, 5:6, :] * (_rms_normalise(z) * gains_ref[3:4, :])


def _post(x, a, mod, gains, wo, wfi, wfo, pool_args, *, tm):
    b, s, d = x.shape
    per_batch_mod = mod.shape[0] > 1
    pool = pool_args is not None
    in_specs = [
        pl.BlockSpec((1, tm, d), lambda i, t: (i, t, 0)),
        pl.BlockSpec((1, tm, a.shape[2]), lambda i, t: (i, t, 0)),
        pl.BlockSpec((1, 8, d), (lambda i, t: (i, 0, 0)) if per_batch_mod else (lambda i, t: (0, 0, 0))),
        _const_spec(gains.shape), _const_spec(wo.shape), _const_spec(wfi.shape), _const_spec(wfo.shape),
    ]
    args = [x, a, mod, gains, wo, wfi, wfo]
    scratch = [pltpu.VMEM((tm, FFN_HIDDEN), BF16)]
    if pool:
        u, wpool, pscale = pool_args
        nu = u.shape[2]
        per_tile = tm // POOL_HALO
        n_halo_blocks = s // POOL_HALO
        in_specs += [
            pl.BlockSpec((1, tm, nu), lambda i, t: (i, t, 0)),
            pl.BlockSpec((1, POOL_HALO, nu), lambda i, t: (i, jnp.maximum(t * per_tile - 1, 0), 0)),
            pl.BlockSpec((1, POOL_HALO, nu),
                         lambda i, t: (i, jnp.minimum((t + 1) * per_tile, n_halo_blocks - 1), 0)),
            _const_spec(wpool.shape), _const_spec(pscale.shape),
        ]
        args += [u, u, u, wpool, pscale]
        scratch.append(pltpu.VMEM((tm + 2 * POOL_HALO, nu), F32))
    return pl.pallas_call(
        functools.partial(_post_kernel, pool=pool, tm=tm, seq=s),
        grid=(b, s // tm),
        in_specs=in_specs,
        out_specs=pl.BlockSpec((1, tm, d), lambda i, t: (i, t, 0)),
        out_shape=jax.ShapeDtypeStruct((b, s, d), F32),
        scratch_shapes=scratch,
        compiler_params=pltpu.CompilerParams(
            dimension_semantics=("parallel", "parallel"), vmem_limit_bytes=VMEM_LIMIT),
        name="post_even" if pool else "post_odd",
    )(*args)


def _pad_rows(a, rows):
    return jnp.pad(a, [(0, 0)] * (a.ndim - 2) + [(0, rows - a.shape[-2]), (0, 0)])


def kernel(x, c, ctx, c_ctx, w_mod, b_mod, g_pre_mix, g_post_mix, g_pre_ffn, g_post_ffn, we_in, we_out,
           we_q_gain, we_k_gain, we_pool, we_pool_scale, wo_in, wo_out, wo_sink, w_ffn_in, w_ffn_out):
    batch, seq, d = x.shape
    c_len = ctx.shape[1]
    depth = w_mod.shape[0]
    tm = 512
    tm_ctx = c_len

    cc = jnp.concatenate([c, c_ctx[None, :]], axis=0)
    cc = _pad_rows(cc, -(-(batch + 1) // 8) * 8)
    mods = _modulation(cc, w_mod, b_mod)

    cos, sin = _rope_tables(seq)
    _, lane_dim = _lane_slot_dim()
    head_mean = jnp.asarray(_head_mean_matrix(), BF16)
    ffn_cols = _ffn_in_cols()

    for l in range(depth):
        even = l % 2 == 0
        i = l // 2
        with_ctx = l < depth - 1
        mod_x = _pad_rows(mods[l, :batch].reshape(batch, 6, d), 8)
        mod_c = _pad_rows(mods[l, batch:batch + 1].reshape(1, 6, d), 8)
        gains = _pad_rows(jnp.stack([g_pre_mix[l], g_post_mix[l], g_pre_ffn[l], g_post_ffn[l]]), 8)
        wfi = w_ffn_in[l][:, ffn_cols].astype(BF16)
        wfo = w_ffn_out[l].astype(BF16)

        if even:
            n_heads, n_kv = 8, 2
            q_w, kv_w = n_heads * HEAD_DIM, n_kv * HEAD_DIM
            cols = np.concatenate([_q_cols(n_heads), _k_cols(n_kv, q_w),
                                   np.arange(q_w + kv_w, we_in.shape[2])])
            w_in = we_in[i][:, cols].astype(BF16)
            widths = (q_w, n_kv * LANES, kv_w, we_in.shape[2] - q_w - 2 * kv_w)
            norm_args = (head_mean, we_q_gain[i][lane_dim][None, :], we_k_gain[i][lane_dim][None, :])
            w_out = jnp.concatenate([we_out[i][_out_rows(n_heads)], we_out[i][q_w:]], axis=0).astype(BF16)
            pool_w = (we_pool[i].astype(BF16), we_pool_scale[i][None, :])
            sink = None
            mode = "dense"
        else:
            n_heads, n_kv = 16, 4
            q_w, kv_w = n_heads * HEAD_DIM, n_kv * HEAD_DIM
            cols = np.concatenate([_q_cols(n_heads), _k_cols(n_kv, q_w),
                                   np.arange(q_w + kv_w, wo_in.shape[2])])
            w_in = wo_in[i][:, cols].astype(BF16)
            widths = (q_w, n_kv * LANES, kv_w, 0)
            norm_args = None
            w_out = wo_out[i][_out_rows(n_heads)].astype(BF16)
            pool_w = None
            sink = jnp.broadcast_to(wo_sink[i][:, None], (n_heads, LANES))
            mode = "window"

        lat = _pre(x, mod_x, gains, w_in, norm_args, (cos, sin), widths=widths, tm=tm)
        con = _pre(ctx, mod_c, gains, w_in, norm_args, None, widths=widths, tm=tm_ctx)
        a = _attention(lat[0], (lat[1], lat[2]), (con[1], con[2]), sink, mode=mode, tq=128 if even else 256)
        x = _post(x, a, mod_x, gains, w_out, wfi, wfo,
                  (lat[3],) + pool_w if even else None, tm=tm)
        if with_ctx:
            ac = _attention(con[0], None, (con[1], con[2]), sink, mode="ctx", tq=c_len)
            ctx = _post(ctx, ac, mod_c, gains, w_out, wfi, wfo,
                        (con[3],) + pool_w if even else None, tm=tm_ctx)
    return x
```

```python
import functools
import math

import numpy as np
import jax
import jax.numpy as jnp
from jax import lax
from jax.experimental import pallas as pl
from jax.experimental.pallas import tpu as pltpu

D_MODEL = 1024
HEAD_DIM = 64
GQA_GROUP = 4
GRID_W = 64
ROPE_THETA = 10000.0
EPS = 1e-6
WINDOW = 128
POOL_WINDOWS = (2, 4, 8, 16)
POOL_HALO = 8
FFN_HIDDEN = 2816
FFN_CHUNK = 256
LANES = 128
LOG2E = math.log2(math.e)
Q_SCALE = LOG2E / math.sqrt(HEAD_DIM)
VMEM_LIMIT = 56 * 1024 * 1024
TOKEN_TILE = 512
QUERY_TILE = 512
QUERY_SUB = LANES

F32 = jnp.float32
BF16 = jnp.bfloat16


def _lane_slot_dim():
    lane = np.arange(LANES)
    part = lane // 32
    i = lane % 32
    dim = np.where(i < 16, i, 32 + (i - 16)) + np.where(part >= 2, 16, 0)
    return part % 2, dim


def _chunk_heads(c):
    head_a = (c // GQA_GROUP) * 2 * GQA_GROUP + c % GQA_GROUP
    return head_a, head_a + GQA_GROUP


def _q_cols(n_heads):
    slot, dim = _lane_slot_dim()
    cols = []
    for c in range(n_heads // 2):
        head_a, head_b = _chunk_heads(c)
        cols.append(np.where(slot == 0, head_a, head_b) * HEAD_DIM + dim)
    return np.concatenate(cols)


def _k_cols(n_kv, base):
    slot, dim = _lane_slot_dim()
    return np.concatenate([base + (2 * m + slot) * HEAD_DIM + dim for m in range(n_kv // 2)])


def _ffn_in_cols():
    cols = []
    for i in range(FFN_HIDDEN // FFN_CHUNK):
        cols.append(np.arange(i * FFN_CHUNK, (i + 1) * FFN_CHUNK))
        cols.append(FFN_HIDDEN + np.arange(i * FFN_CHUNK, (i + 1) * FFN_CHUNK))
    return np.concatenate(cols)


def _head_mean_matrix():
    slot, _ = _lane_slot_dim()
    return (slot[:, None] == slot[None, :]).astype(np.float32) / HEAD_DIM


def _rope_tables(seq):
    quarter = HEAD_DIM // 4
    freqs = ROPE_THETA ** (-jnp.arange(quarter, dtype=F32) / quarter)
    t = jnp.arange(seq, dtype=jnp.int32)
    rows = (t // GRID_W).astype(F32)[:, None] * freqs[None, :]
    cols = (t % GRID_W).astype(F32)[:, None] * freqs[None, :]
    ang = jnp.concatenate([rows, cols], axis=-1)
    cos = jnp.tile(jnp.cos(ang), (1, 4))
    sin = jnp.sin(ang)
    sin = jnp.concatenate([-sin, -sin, sin, sin], axis=-1)
    return cos, sin


def _rms_normalise(x):
    return x * lax.rsqrt(jnp.mean(x * x, axis=-1, keepdims=True) + EPS)


def _silu(x):
    return x * (1.0 / (1.0 + jnp.exp(-x)))


def _dot(a, b):
    return jnp.dot(a, b, preferred_element_type=F32)


def _dot_tn(a_t, b):
    return lax.dot_general(a_t, b, (((0,), (0,)), ((), ())), preferred_element_type=F32)


def _const_spec(shape):
    return pl.BlockSpec(shape, lambda *_: (0,) * len(shape), pipeline_mode=pl.Buffered(1))


def _params():
    return pltpu.CompilerParams(dimension_semantics=("parallel", "parallel"), vmem_limit_bytes=VMEM_LIMIT)


def _mod_kernel(c_ref, w_ref, b_ref, o_ref):
    s = _silu(c_ref[...]).astype(BF16)
    o_ref[0] = _dot(s, w_ref[0].astype(BF16)) + b_ref[0]


def _modulation(cc, w_mod, b_mod):
    depth, d, n = w_mod.shape
    rows = cc.shape[0]
    tn = 1536
    return pl.pallas_call(
        _mod_kernel,
        grid=(depth, n // tn),
        in_specs=[
            pl.BlockSpec((rows, d), lambda l, j: (0, 0)),
            pl.BlockSpec((1, d, tn), lambda l, j: (l, 0, j)),
            pl.BlockSpec((1, 1, tn), lambda l, j: (l, 0, j)),
        ],
        out_specs=pl.BlockSpec((1, rows, tn), lambda l, j: (l, 0, j)),
        out_shape=jax.ShapeDtypeStruct((depth, rows, n), F32),
        compiler_params=_params(),
        name="modulation",
    )(cc, w_mod, b_mod.reshape(depth, 1, n))


def _pre_kernel(*refs, n_q, n_k, n_v, n_u, qk_norm, rope):
    x_ref, mod_ref, gains_ref, w_ref = refs[:4]
    pos = 4
    if qk_norm:
        pm_ref, qg_ref, kg_ref = refs[pos:pos + 3]
        pos += 3
    if rope:
        cos_ref, sin_ref = refs[pos:pos + 2]
        pos += 2
    qt_ref, k_ref, vt_ref = refs[pos:pos + 3]
    u_ref = refs[pos + 3] if n_u else None

    x = x_ref[0]
    h = _rms_normalise(x) * gains_ref[0:1, :]
    h = h * (1.0 + mod_ref[0, 1:2, :]) + mod_ref[0, 0:1, :]
    p = _dot(h.astype(BF16), w_ref[...])

    def head_chunk(c, gain_ref):
        if qk_norm:
            c2 = c * c
            hi = c2.astype(BF16)
            lo = (c2 - hi.astype(F32)).astype(BF16)
            ms = _dot(hi, pm_ref[...]) + _dot(lo, pm_ref[...])
            c = c * lax.rsqrt(ms + EPS) * gain_ref[...]
        if rope:
            c = c * cos_ref[...] + pltpu.roll(c, LANES // 2, 1) * sin_ref[...]
        return c

    for j in range(n_q // LANES):
        sl = slice(j * LANES, (j + 1) * LANES)
        c = head_chunk(p[:, sl], qg_ref if qk_norm else None) * Q_SCALE
        qt_ref[0, sl, :] = c.T.astype(BF16)
    for j in range(n_k // LANES):
        sl = slice(j * LANES, (j + 1) * LANES)
        c = head_chunk(p[:, n_q + j * LANES:n_q + (j + 1) * LANES], kg_ref if qk_norm else None)
        k_ref[0, :, sl] = c.astype(BF16)
    for j in range(n_v // LANES):
        sl = slice(j * LANES, (j + 1) * LANES)
        vt_ref[0, sl, :] = p[:, n_q + n_k + j * LANES:n_q + n_k + (j + 1) * LANES].T.astype(BF16)
    if n_u:
        u_ref[0] = p[:, n_q + n_k + n_v:]


def _pre(x, mod, gains, w, norm_args, rope_args, *, widths, tm):
    n_q, n_k, n_v, n_u = widths
    b, s, d = x.shape
    per_batch_mod = mod.shape[0] > 1
    qk_norm = norm_args is not None
    rope = rope_args is not None
    in_specs = [
        pl.BlockSpec((1, tm, d), lambda i, t: (i, t, 0)),
        pl.BlockSpec((1, 8, d), (lambda i, t: (i, 0, 0)) if per_batch_mod else (lambda i, t: (0, 0, 0))),
        _const_spec(gains.shape),
        _const_spec(w.shape),
    ]
    args = [x, mod, gains, w]
    if qk_norm:
        in_specs += [_const_spec(a.shape) for a in norm_args]
        args += list(norm_args)
    if rope:
        in_specs += [pl.BlockSpec((tm, LANES), lambda i, t: (t, 0))] * 2
        args += list(rope_args)
    out_shape = [jax.ShapeDtypeStruct((b, n_q, s), BF16), jax.ShapeDtypeStruct((b, s, n_k), BF16),
                 jax.ShapeDtypeStruct((b, n_v, s), BF16)]
    out_specs = [pl.BlockSpec((1, n_q, tm), lambda i, t: (i, 0, t)),
                 pl.BlockSpec((1, tm, n_k), lambda i, t: (i, t, 0)),
                 pl.BlockSpec((1, n_v, tm), lambda i, t: (i, 0, t))]
    if n_u:
        out_shape.append(jax.ShapeDtypeStruct((b, s, n_u), F32))
        out_specs.append(pl.BlockSpec((1, tm, n_u), lambda i, t: (i, t, 0)))
    return pl.pallas_call(
        functools.partial(_pre_kernel, n_q=n_q, n_k=n_k, n_v=n_v, n_u=n_u, qk_norm=qk_norm, rope=rope),
        grid=(b, s // tm),
        in_specs=in_specs,
        out_specs=out_specs,
        out_shape=out_shape,
        compiler_params=_params(),
        name="pre_even" if qk_norm else "pre_odd",
    )(*args)


def _attn_kernel(*refs, mode, n_pairs, n_sub, tq, seq, use_sink):
    qt_ref = refs[0]
    pos = 1
    if mode != "ctx":
        k_ref, vt_ref = refs[pos:pos + 2]
        pos += 2
    kc_ref, vct_ref = refs[pos:pos + 2]
    pos += 2
    if use_sink:
        sink_ref = refs[pos]
        pos += 1
    o_ref = refs[pos]

    sub = QUERY_SUB
    row = lax.broadcasted_iota(jnp.int32, (LANES, 1), 0)
    slot_a = (row // 32) % 2 == 0
    first_head = lax.broadcasted_iota(jnp.int32, (1, 2 * sub), 1) < sub
    band = sub + 2 * WINDOW

    def sub_block(sb, carry):
        off = pl.multiple_of(sb * sub, sub)
        if mode == "window":
            q0 = pl.program_id(1) * tq + off
            start = pl.multiple_of(jnp.clip(q0 - WINDOW, 0, seq - band), LANES)
            kpos = start + lax.broadcasted_iota(jnp.int32, (band, sub), 0)
            qpos = q0 + lax.broadcasted_iota(jnp.int32, (band, sub), 1)
            bias = jnp.where(jnp.abs(qpos - kpos) <= WINDOW, 0.0, -jnp.inf).astype(F32)
            bias = jnp.concatenate([bias, bias], axis=1)
        for c in range(GQA_GROUP * n_pairs):
            pair = c // GQA_GROUP
            head_a, head_b = _chunk_heads(c)
            psl = slice(pair * LANES, (pair + 1) * LANES)
            qt = qt_ref[0, c * LANES:(c + 1) * LANES, pl.ds(off, sub)]
            zero = jnp.zeros_like(qt)
            rhs = jnp.concatenate([jnp.where(slot_a, qt, zero), jnp.where(slot_a, zero, qt)], axis=1)

            scores, values = [], []
            if mode == "dense":
                scores.append(_dot(k_ref[0, :, psl], rhs))
                values.append(vt_ref[0, psl, :])
            elif mode == "window":
                scores.append(_dot(k_ref[0, pl.ds(start, band), psl], rhs) + bias)
                values.append(vt_ref[0, psl, pl.ds(start, band)])
            scores.append(_dot(kc_ref[0, :, psl], rhs))
            values.append(vct_ref[0, psl, :])

            m = functools.reduce(jnp.maximum, [s.max(axis=0, keepdims=True) for s in scores])
            denom = None
            if use_sink:
                sink = jnp.where(first_head, sink_ref[head_a], sink_ref[head_b]) * LOG2E
                m = jnp.maximum(m, sink)
                denom = jnp.exp2(sink - m)
            o2 = None
            for s, v_t in zip(scores, values):
                p = jnp.exp2(s - m)
                col = p.sum(axis=0, keepdims=True)
                denom = col if denom is None else denom + col
                pv = _dot(v_t, p.astype(BF16))
                o2 = pv if o2 is None else o2 + pv
            o2 = o2 * (1.0 / denom)
            o_ref[0, head_a * HEAD_DIM:(head_a + 1) * HEAD_DIM, pl.ds(off, sub)] = (
                o2[:HEAD_DIM, :sub].astype(BF16))
            o_ref[0, head_b * HEAD_DIM:(head_b + 1) * HEAD_DIM, pl.ds(off, sub)] = (
                o2[HEAD_DIM:, sub:].astype(BF16))
        return carry

    lax.fori_loop(0, n_sub, sub_block, 0)


def _attention(qt, kv, kv_ctx, sink, *, mode, tq):
    b, wq, sq = qt.shape
    n_pairs = wq // (2 * GQA_GROUP * HEAD_DIM)
    kc, vct = kv_ctx
    c_len = kc.shape[1]
    use_sink = sink is not None
    in_specs = [pl.BlockSpec((1, wq, tq), lambda i, t: (i, 0, t))]
    args = [qt]
    seq = sq
    if mode != "ctx":
        k, vt = kv
        seq = k.shape[1]
        in_specs += [pl.BlockSpec((1, seq, k.shape[2]), lambda i, t: (i, 0, 0)),
                     pl.BlockSpec((1, vt.shape[1], seq), lambda i, t: (i, 0, 0))]
        args += [k, vt]
    in_specs += [pl.BlockSpec((1, c_len, kc.shape[2]), lambda i, t: (i, 0, 0)),
                 pl.BlockSpec((1, vct.shape[1], c_len), lambda i, t: (i, 0, 0))]
    args += [kc, vct]
    if use_sink:
        in_specs.append(pl.BlockSpec(memory_space=pltpu.SMEM))
        args.append(sink)
    return pl.pallas_call(
        functools.partial(_attn_kernel, mode=mode, n_pairs=n_pairs, n_sub=tq // QUERY_SUB, tq=tq, seq=seq,
                          use_sink=use_sink),
        grid=(b, sq // tq),
        in_specs=in_specs,
        out_specs=pl.BlockSpec((1, wq, tq), lambda i, t: (i, 0, t)),
        out_shape=jax.ShapeDtypeStruct((b, wq, sq), BF16),
        compiler_params=_params(),
        name="attn_" + mode,
    )(*args)


def _post_kernel(*refs, pool, tm, seq):
    x_ref, at_ref, mod_ref, gains_ref, wo_ref, wfi_ref, wfo_ref = refs[:7]
    pos = 7
    if pool:
        u_ref, uprev_ref, unext_ref, wpool_ref, pscale_ref = refs[pos:pos + 5]
        pos += 5
    o_ref = refs[pos]
    hid_ref = refs[pos + 1]
    if pool:
        ext_ref = refs[pos + 2]

    n_a = at_ref.shape[1]
    y = _dot_tn(at_ref[0], wo_ref[0:n_a, :])
    if pool:
        t = pl.program_id(1)
        last = pl.num_programs(1) - 1
        halo = jnp.zeros((POOL_HALO, ext_ref.shape[1]), F32)
        ext_ref[0:POOL_HALO, :] = jnp.where(t > 0, uprev_ref[0], halo)
        ext_ref[POOL_HALO:POOL_HALO + tm, :] = u_ref[0]
        ext_ref[POOL_HALO + tm:, :] = jnp.where(t < last, unext_ref[0], halo)
        tok = t * tm + lax.broadcasted_iota(jnp.int32, (tm, 1), 0)
        mixed = []
        for g, w in enumerate(POOL_WINDOWS):
            sl = slice(g * LANES, (g + 1) * LANES)
            acc = None
            for j in range(-(w // 2), w - w // 2):
                term = ext_ref[POOL_HALO + j:POOL_HALO + j + tm, sl]
                acc = term if acc is None else acc + term
            cnt = jnp.minimum(tok + (w - w // 2), seq) - jnp.maximum(tok - w // 2, 0)
            diff = acc / cnt.astype(F32) - ext_ref[POOL_HALO:POOL_HALO + tm, sl]
            mixed.append((_dot(diff.astype(BF16), wpool_ref[g]) * pscale_ref[:, sl]).astype(BF16))
        y = y + _dot(jnp.concatenate(mixed, axis=-1), wo_ref[n_a:, :])

    x1 = x_ref[0] + mod_ref[0, 2:3, :] * (_rms_normalise(y) * gains_ref[1:2, :])
    h = _rms_normalise(x1) * gains_ref[2:3, :]
    h = (h * (1.0 + mod_ref[0, 4:5, :]) + mod_ref[0, 3:4, :]).astype(BF16)
    for i in range(FFN_HIDDEN // FFN_CHUNK):
        gu = _dot(h, wfi_ref[:, 2 * i * FFN_CHUNK:2 * (i + 1) * FFN_CHUNK])
        gate, up = gu[:, :FFN_CHUNK], gu[:, FFN_CHUNK:]
        hid_ref[:, i * FFN_CHUNK:(i + 1) * FFN_CHUNK] = (_silu(gate) * up).astype(BF16)
    z = _dot(hid_ref[...], wfo_ref[...])
    o_ref[0] = x1 + mod_ref[0, 5:6, :] * (_rms_normalise(z) * gains_ref[3:4, :])


def _post(x, a_t, mod, gains, wo, wfi, wfo, pool_args, *, tm):
    b, s, d = x.shape
    per_batch_mod = mod.shape[0] > 1
    pool = pool_args is not None
    in_specs = [
        pl.BlockSpec((1, tm, d), lambda i, t: (i, t, 0)),
        pl.BlockSpec((1, a_t.shape[1], tm), lambda i, t: (i, 0, t)),
        pl.BlockSpec((1, 8, d), (lambda i, t: (i, 0, 0)) if per_batch_mod else (lambda i, t: (0, 0, 0))),
        _const_spec(gains.shape), _const_spec(wo.shape), _const_spec(wfi.shape), _const_spec(wfo.shape),
    ]
    args = [x, a_t, mod, gains, wo, wfi, wfo]
    scratch = [pltpu.VMEM((tm, FFN_HIDDEN), BF16)]
    if pool:
        u, wpool, pscale = pool_args
        nu = u.shape[2]
        per_tile = tm // POOL_HALO
        n_halo_blocks = s // POOL_HALO
        in_specs += [
            pl.BlockSpec((1, tm, nu), lambda i, t: (i, t, 0)),
            pl.BlockSpec((1, POOL_HALO, nu), lambda i, t: (i, jnp.maximum(t * per_tile - 1, 0), 0)),
            pl.BlockSpec((1, POOL_HALO, nu),
                         lambda i, t: (i, jnp.minimum((t + 1) * per_tile, n_halo_blocks - 1), 0)),
            _const_spec(wpool.shape), _const_spec(pscale.shape),
        ]
        args += [u, u, u, wpool, pscale]
        scratch.append(pltpu.VMEM((tm + 2 * POOL_HALO, nu), F32))
    return pl.pallas_call(
        functools.partial(_post_kernel, pool=pool, tm=tm, seq=s),
        grid=(b, s // tm),
        in_specs=in_specs,
        out_specs=pl.BlockSpec((1, tm, d), lambda i, t: (i, t, 0)),
        out_shape=jax.ShapeDtypeStruct((b, s, d), F32),
        scratch_shapes=scratch,
        compiler_params=_params(),
        name="post_even" if pool else "post_odd",
    )(*args)


def _pad_rows(a, rows):
    return jnp.pad(a, [(0, 0)] * (a.ndim - 2) + [(0, rows - a.shape[-2]), (0, 0)])


def kernel(x, c, ctx, c_ctx, w_mod, b_mod, g_pre_mix, g_post_mix, g_pre_ffn, g_post_ffn, we_in, we_out,
           we_q_gain, we_k_gain, we_pool, we_pool_scale, wo_in, wo_out, wo_sink, w_ffn_in, w_ffn_out):
    batch, seq, d = x.shape
    c_len = ctx.shape[1]
    depth = w_mod.shape[0]
    tm = TOKEN_TILE
    tm_ctx = c_len

    cc = jnp.concatenate([c, c_ctx[None, :]], axis=0)
    cc = _pad_rows(cc, -(-(batch + 1) // 8) * 8)
    mods = _modulation(cc, w_mod, b_mod)

    cos, sin = _rope_tables(seq)
    _, lane_dim = _lane_slot_dim()
    head_mean = jnp.asarray(_head_mean_matrix(), BF16)
    ffn_cols = _ffn_in_cols()

    for l in range(depth):
        even = l % 2 == 0
        i = l // 2
        with_ctx = l < depth - 1
        mod_x = _pad_rows(mods[l, :batch].reshape(batch, 6, d), 8)
        mod_c = _pad_rows(mods[l, batch:batch + 1].reshape(1, 6, d), 8)
        gains = _pad_rows(jnp.stack([g_pre_mix[l], g_post_mix[l], g_pre_ffn[l], g_post_ffn[l]]), 8)
        wfi = w_ffn_in[l][:, ffn_cols].astype(BF16)
        wfo = w_ffn_out[l].astype(BF16)

        if even:
            n_heads, n_kv = 8, 2
            w_full, w_out = we_in[i], we_out[i].astype(BF16)
            norm_args = (head_mean, we_q_gain[i][lane_dim][None, :], we_k_gain[i][lane_dim][None, :])
            sink = None
            mode = "dense"
        else:
            n_heads, n_kv = 16, 4
            w_full, w_out = wo_in[i], wo_out[i].astype(BF16)
            norm_args = None
            sink = wo_sink[i]
            mode = "window"
        q_w, kv_w = n_heads * HEAD_DIM, n_kv * HEAD_DIM
        cols = np.concatenate([_q_cols(n_heads), _k_cols(n_kv, q_w), np.arange(q_w + kv_w, w_full.shape[1])])
        w_in = w_full[:, cols].astype(BF16)
        widths = (q_w, kv_w, kv_w, w_full.shape[1] - q_w - 2 * kv_w)

        lat = _pre(x, mod_x, gains, w_in, norm_args, (cos, sin), widths=widths, tm=tm)
        con = _pre(ctx, mod_c, gains, w_in, norm_args, None, widths=widths, tm=tm_ctx)
        a_t = _attention(lat[0], (lat[1], lat[2]), (con[1], con[2]), sink, mode=mode, tq=QUERY_TILE)
        pool_w = (we_pool[i].astype(BF16), we_pool_scale[i][None, :]) if even else None
        x = _post(x, a_t, mod_x, gains, w_out, wfi, wfo, (lat[3],) + pool_w if even else None, tm=tm)
        if with_ctx:
            ac_t = _attention(con[0], None, (con[1], con[2]), sink, mode="ctx", tq=c_len)
            ctx = _post(ctx, ac_t, mod_c, gains, w_out, wfi, wfo,
                        (con[3],) + pool_w if even else None, tm=tm_ctx)
    return x
```

```python
import functools
import math

import numpy as np
import jax
import jax.numpy as jnp
from jax import lax
from jax.experimental import pallas as pl
from jax.experimental.pallas import tpu as pltpu

D_MODEL = 1024
HEAD_DIM = 64
GQA_GROUP = 4
GRID_W = 64
ROPE_THETA = 10000.0
EPS = 1e-6
WINDOW = 128
POOL_WINDOWS = (2, 4, 8, 16)
POOL_HALO = 8
FFN_HIDDEN = 2816
FFN_CHUNK = 256
LANES = 128
LOG2E = math.log2(math.e)
Q_SCALE = LOG2E / math.sqrt(HEAD_DIM)
VMEM_LIMIT = 56 * 1024 * 1024
TOKEN_TILE = 512
QUERY_TILE = 256
QUERY_SUB = LANES

F32 = jnp.float32
BF16 = jnp.bfloat16


def _lane_slot_dim():
    lane = np.arange(LANES)
    part = lane // 32
    i = lane % 32
    dim = np.where(i < 16, i, 32 + (i - 16)) + np.where(part >= 2, 16, 0)
    return part % 2, dim


def _chunk_heads(c):
    head_a = (c // GQA_GROUP) * 2 * GQA_GROUP + c % GQA_GROUP
    return head_a, head_a + GQA_GROUP


def _q_cols(n_heads):
    slot, dim = _lane_slot_dim()
    cols = []
    for c in range(n_heads // 2):
        head_a, head_b = _chunk_heads(c)
        cols.append(np.where(slot == 0, head_a, head_b) * HEAD_DIM + dim)
    return np.concatenate(cols)


def _k_cols(n_kv, base):
    slot, dim = _lane_slot_dim()
    return np.concatenate([base + (2 * m + slot) * HEAD_DIM + dim for m in range(n_kv // 2)])


def _ffn_in_cols():
    cols = []
    for i in range(FFN_HIDDEN // FFN_CHUNK):
        cols.append(np.arange(i * FFN_CHUNK, (i + 1) * FFN_CHUNK))
        cols.append(FFN_HIDDEN + np.arange(i * FFN_CHUNK, (i + 1) * FFN_CHUNK))
    return np.concatenate(cols)


def _head_mean_matrix():
    slot, _ = _lane_slot_dim()
    return (slot[:, None] == slot[None, :]).astype(np.float32) / HEAD_DIM


def _rope_tables(seq):
    quarter = HEAD_DIM // 4
    freqs = ROPE_THETA ** (-jnp.arange(quarter, dtype=F32) / quarter)
    t = jnp.arange(seq, dtype=jnp.int32)
    rows = (t // GRID_W).astype(F32)[:, None] * freqs[None, :]
    cols = (t % GRID_W).astype(F32)[:, None] * freqs[None, :]
    ang = jnp.concatenate([rows, cols], axis=-1)
    cos = jnp.tile(jnp.cos(ang), (1, 4))
    sin = jnp.sin(ang)
    sin = jnp.concatenate([-sin, -sin, sin, sin], axis=-1)
    return cos, sin


def _rms_normalise(x):
    return x * lax.rsqrt(jnp.mean(x * x, axis=-1, keepdims=True) + EPS)


def _silu(x):
    return x * (1.0 / (1.0 + jnp.exp(-x)))


def _dot(a, b):
    return jnp.dot(a, b, preferred_element_type=F32)


def _dot_tn(a_t, b):
    return lax.dot_general(a_t, b, (((0,), (0,)), ((), ())), preferred_element_type=F32)


def _const_spec(shape):
    return pl.BlockSpec(shape, lambda *_: (0,) * len(shape), pipeline_mode=pl.Buffered(1))


def _params():
    return pltpu.CompilerParams(dimension_semantics=("parallel", "parallel"), vmem_limit_bytes=VMEM_LIMIT)


def _mod_kernel(c_ref, w_ref, b_ref, o_ref):
    s = _silu(c_ref[...]).astype(BF16)
    o_ref[0] = _dot(s, w_ref[0].astype(BF16)) + b_ref[0]


def _modulation(cc, w_mod, b_mod):
    depth, d, n = w_mod.shape
    rows = cc.shape[0]
    tn = 1536
    return pl.pallas_call(
        _mod_kernel,
        grid=(depth, n // tn),
        in_specs=[
            pl.BlockSpec((rows, d), lambda l, j: (0, 0)),
            pl.BlockSpec((1, d, tn), lambda l, j: (l, 0, j)),
            pl.BlockSpec((1, 1, tn), lambda l, j: (l, 0, j)),
        ],
        out_specs=pl.BlockSpec((1, rows, tn), lambda l, j: (l, 0, j)),
        out_shape=jax.ShapeDtypeStruct((depth, rows, n), F32),
        compiler_params=_params(),
        name="modulation",
    )(cc, w_mod, b_mod.reshape(depth, 1, n))


def _pre_kernel(*refs, n_q, n_k, n_v, n_u, qk_norm, rope):
    x_ref, mod_ref, gains_ref, w_ref = refs[:4]
    pos = 4
    if qk_norm:
        pm_ref, qg_ref, kg_ref = refs[pos:pos + 3]
        pos += 3
    if rope:
        cos_ref, sin_ref = refs[pos:pos + 2]
        pos += 2
    qt_ref, k_ref, vt_ref = refs[pos:pos + 3]
    u_ref = refs[pos + 3] if n_u else None

    x = x_ref[0]
    h = _rms_normalise(x) * gains_ref[0:1, :]
    h = h * (1.0 + mod_ref[0, 1:2, :]) + mod_ref[0, 0:1, :]
    p = _dot(h.astype(BF16), w_ref[...])

    def head_chunk(c, gain_ref):
        if qk_norm:
            c2 = c * c
            hi = c2.astype(BF16)
            lo = (c2 - hi.astype(F32)).astype(BF16)
            ms = _dot(hi, pm_ref[...]) + _dot(lo, pm_ref[...])
            c = c * lax.rsqrt(ms + EPS) * gain_ref[...]
        if rope:
            c = c * cos_ref[...] + pltpu.roll(c, LANES // 2, 1) * sin_ref[...]
        return c

    for j in range(n_q // LANES):
        sl = slice(j * LANES, (j + 1) * LANES)
        c = head_chunk(p[:, sl], qg_ref if qk_norm else None) * Q_SCALE
        qt_ref[0, sl, :] = c.T.astype(BF16)
    for j in range(n_k // LANES):
        sl = slice(j * LANES, (j + 1) * LANES)
        c = head_chunk(p[:, n_q + j * LANES:n_q + (j + 1) * LANES], kg_ref if qk_norm else None)
        k_ref[0, :, sl] = c.astype(BF16)
    for j in range(n_v // LANES):
        sl = slice(j * LANES, (j + 1) * LANES)
        vt_ref[0, sl, :] = p[:, n_q + n_k + j * LANES:n_q + n_k + (j + 1) * LANES].T.astype(BF16)
    if n_u:
        u_ref[0] = p[:, n_q + n_k + n_v:]


def _pre(x, mod, gains, w, norm_args, rope_args, *, widths, tm):
    n_q, n_k, n_v, n_u = widths
    b, s, d = x.shape
    per_batch_mod = mod.shape[0] > 1
    qk_norm = norm_args is not None
    rope = rope_args is not None
    in_specs = [
        pl.BlockSpec((1, tm, d), lambda i, t: (i, t, 0)),
        pl.BlockSpec((1, 8, d), (lambda i, t: (i, 0, 0)) if per_batch_mod else (lambda i, t: (0, 0, 0))),
        _const_spec(gains.shape),
        _const_spec(w.shape),
    ]
    args = [x, mod, gains, w]
    if qk_norm:
        in_specs += [_const_spec(a.shape) for a in norm_args]
        args += list(norm_args)
    if rope:
        in_specs += [pl.BlockSpec((tm, LANES), lambda i, t: (t, 0))] * 2
        args += list(rope_args)
    out_shape = [jax.ShapeDtypeStruct((b, n_q, s), BF16), jax.ShapeDtypeStruct((b, s, n_k), BF16),
                 jax.ShapeDtypeStruct((b, n_v, s), BF16)]
    out_specs = [pl.BlockSpec((1, n_q, tm), lambda i, t: (i, 0, t)),
                 pl.BlockSpec((1, tm, n_k), lambda i, t: (i, t, 0)),
                 pl.BlockSpec((1, n_v, tm), lambda i, t: (i, 0, t))]
    if n_u:
        out_shape.append(jax.ShapeDtypeStruct((b, s, n_u), F32))
        out_specs.append(pl.BlockSpec((1, tm, n_u), lambda i, t: (i, t, 0)))
    return pl.pallas_call(
        functools.partial(_pre_kernel, n_q=n_q, n_k=n_k, n_v=n_v, n_u=n_u, qk_norm=qk_norm, rope=rope),
        grid=(b, s // tm),
        in_specs=in_specs,
        out_specs=out_specs,
        out_shape=out_shape,
        compiler_params=_params(),
        name="pre_even" if qk_norm else "pre_odd",
    )(*args)


def _attn_kernel(*refs, mode, n_pairs, n_sub, tq, seq, use_sink):
    qt_ref = refs[0]
    pos = 1
    if mode != "ctx":
        k_ref, vt_ref = refs[pos:pos + 2]
        pos += 2
    kc_ref, vct_ref = refs[pos:pos + 2]
    pos += 2
    if use_sink:
        sink_ref = refs[pos]
        pos += 1
    o_ref = refs[pos]

    sub = QUERY_SUB
    row = lax.broadcasted_iota(jnp.int32, (LANES, 1), 0)
    slot_a = (row // 32) % 2 == 0
    band = sub + 2 * WINDOW
    if mode == "window":
        rel = (lax.broadcasted_iota(jnp.int32, (band, sub), 0)
               - lax.broadcasted_iota(jnp.int32, (band, sub), 1))

    def block_geometry(sb):
        q0 = pl.program_id(1) * tq + sb * sub
        start = pl.multiple_of(jnp.clip(q0 - WINDOW, 0, seq - band), LANES)
        bias = jnp.where(jnp.abs(rel + (start - q0)) <= WINDOW, 0.0, -jnp.inf).astype(F32)
        return start, jnp.concatenate([bias] * 4, axis=1)

    def scores(sb, item, geometry):
        pair = (2 * item) // GQA_GROUP
        psl = slice(pair * LANES, (pair + 1) * LANES)
        cols = []
        for c in (2 * item, 2 * item + 1):
            qt = qt_ref[0, c * LANES:(c + 1) * LANES, sb * sub:(sb + 1) * sub]
            zero = jnp.zeros_like(qt)
            cols += [jnp.where(slot_a, qt, zero), jnp.where(slot_a, zero, qt)]
        rhs = jnp.concatenate(cols, axis=1)
        out = []
        if mode == "dense":
            out.append(_dot(k_ref[0, :, psl], rhs))
        elif mode == "window":
            start, bias = geometry
            out.append(_dot(k_ref[0, pl.ds(start, band), psl], rhs) + bias)
        out.append(_dot(kc_ref[0, :, psl], rhs))
        return out

    def finish(sb, item, geometry, s_list):
        pair = (2 * item) // GQA_GROUP
        psl = slice(pair * LANES, (pair + 1) * LANES)
        heads = _chunk_heads(2 * item) + _chunk_heads(2 * item + 1)
        values = []
        if mode == "dense":
            values.append(vt_ref[0, psl, :])
        elif mode == "window":
            values.append(vt_ref[0, psl, pl.ds(geometry[0], band)])
        values.append(vct_ref[0, psl, :])

        m = functools.reduce(jnp.maximum, [s.max(axis=0, keepdims=True) for s in s_list])
        denom = None
        if use_sink:
            sink = jnp.concatenate([jnp.full((1, sub), sink_ref[h], F32) for h in heads], axis=1) * LOG2E
            m = jnp.maximum(m, sink)
            denom = jnp.exp2(sink - m)
        o4 = None
        for s, v_t in zip(s_list, values):
            p = jnp.exp2(s - m)
            col = p.sum(axis=0, keepdims=True)
            denom = col if denom is None else denom + col
            pv = _dot(v_t, p.astype(BF16))
            o4 = pv if o4 is None else o4 + pv
        o4 = o4 * (1.0 / denom)
        for j, h in enumerate(heads):
            rows = slice(0, HEAD_DIM) if j % 2 == 0 else slice(HEAD_DIM, 2 * HEAD_DIM)
            o_ref[0, h * HEAD_DIM:(h + 1) * HEAD_DIM, sb * sub:(sb + 1) * sub] = (
                o4[rows, j * sub:(j + 1) * sub].astype(BF16))

    items = [(sb, it) for sb in range(n_sub) for it in range(2 * n_pairs)]
    geometries = [block_geometry(sb) if mode == "window" else None for sb in range(n_sub)]
    pending = scores(*items[0], geometries[items[0][0]])
    for i, (sb, it) in enumerate(items):
        upcoming = None
        if i + 1 < len(items):
            nsb, nit = items[i + 1]
            upcoming = scores(nsb, nit, geometries[nsb])
        finish(sb, it, geometries[sb], pending)
        pending = upcoming


def _attention(qt, kv, kv_ctx, sink, *, mode, tq):
    b, wq, sq = qt.shape
    n_pairs = wq // (2 * GQA_GROUP * HEAD_DIM)
    kc, vct = kv_ctx
    c_len = kc.shape[1]
    use_sink = sink is not None
    in_specs = [pl.BlockSpec((1, wq, tq), lambda i, t: (i, 0, t))]
    args = [qt]
    seq = sq
    if mode != "ctx":
        k, vt = kv
        seq = k.shape[1]
        in_specs += [pl.BlockSpec((1, seq, k.shape[2]), lambda i, t: (i, 0, 0)),
                     pl.BlockSpec((1, vt.shape[1], seq), lambda i, t: (i, 0, 0))]
        args += [k, vt]
    in_specs += [pl.BlockSpec((1, c_len, kc.shape[2]), lambda i, t: (i, 0, 0)),
                 pl.BlockSpec((1, vct.shape[1], c_len), lambda i, t: (i, 0, 0))]
    args += [kc, vct]
    if use_sink:
        in_specs.append(pl.BlockSpec(memory_space=pltpu.SMEM))
        args.append(sink)
    return pl.pallas_call(
        functools.partial(_attn_kernel, mode=mode, n_pairs=n_pairs, n_sub=tq // QUERY_SUB, tq=tq, seq=seq,
                          use_sink=use_sink),
        grid=(b, sq // tq),
        in_specs=in_specs,
        out_specs=pl.BlockSpec((1, wq, tq), lambda i, t: (i, 0, t)),
        out_shape=jax.ShapeDtypeStruct((b, wq, sq), BF16),
        compiler_params=_params(),
        name="attn_" + mode,
    )(*args)


def _post_kernel(*refs, pool, tm, seq):
    x_ref, at_ref, mod_ref, gains_ref, wo_ref, wfi_ref, wfo_ref = refs[:7]
    pos = 7
    if pool:
        u_ref, uprev_ref, unext_ref, wpool_ref, pscale_ref = refs[pos:pos + 5]
        pos += 5
    o_ref = refs[pos]
    hid_ref = refs[pos + 1]
    if pool:
        ext_ref = refs[pos + 2]

    n_a = at_ref.shape[1]
    y = _dot_tn(at_ref[0], wo_ref[0:n_a, :])
    if pool:
        t = pl.program_id(1)
        last = pl.num_programs(1) - 1
        halo = jnp.zeros((POOL_HALO, ext_ref.shape[1]), F32)
        ext_ref[0:POOL_HALO, :] = jnp.where(t > 0, uprev_ref[0], halo)
        ext_ref[POOL_HALO:POOL_HALO + tm, :] = u_ref[0]
        ext_ref[POOL_HALO + tm:, :] = jnp.where(t < last, unext_ref[0], halo)
        tok = t * tm + lax.broadcasted_iota(jnp.int32, (tm, 1), 0)
        mixed = []
        for g, w in enumerate(POOL_WINDOWS):
            sl = slice(g * LANES, (g + 1) * LANES)
            acc = None
            for j in range(-(w // 2), w - w // 2):
                term = ext_ref[POOL_HALO + j:POOL_HALO + j + tm, sl]
                acc = term if acc is None else acc + term
            cnt = jnp.minimum(tok + (w - w // 2), seq) - jnp.maximum(tok - w // 2, 0)
            diff = acc / cnt.astype(F32) - ext_ref[POOL_HALO:POOL_HALO + tm, sl]
            mixed.append((_dot(diff.astype(BF16), wpool_ref[g]) * pscale_ref[:, sl]).astype(BF16))
        y = y + _dot(jnp.concatenate(mixed, axis=-1), wo_ref[n_a:, :])

    x1 = x_ref[0] + mod_ref[0, 2:3, :] * (_rms_normalise(y) * gains_ref[1:2, :])
    h = _rms_normalise(x1) * gains_ref[2:3, :]
    h = (h * (1.0 + mod_ref[0, 4:5, :]) + mod_ref[0, 3:4, :]).astype(BF16)
    for i in range(FFN_HIDDEN // FFN_CHUNK):
        gu = _dot(h, wfi_ref[:, 2 * i * FFN_CHUNK:2 * (i + 1) * FFN_CHUNK])
        gate, up = gu[:, :FFN_CHUNK], gu[:, FFN_CHUNK:]
        hid_ref[:, i * FFN_CHUNK:(i + 1) * FFN_CHUNK] = (_silu(gate) * up).astype(BF16)
    z = _dot(hid_ref[...], wfo_ref[...])
    o_ref[0] = x1 + mod_ref[0, 5:6, :] * (_rms_normalise(z) * gains_ref[3:4, :])


def _post(x, a_t, mod, gains, wo, wfi, wfo, pool_args, *, tm):
    b, s, d = x.shape
    per_batch_mod = mod.shape[0] > 1
    pool = pool_args is not None
    in_specs = [
        pl.BlockSpec((1, tm, d), lambda i, t: (i, t, 0)),
        pl.BlockSpec((1, a_t.shape[1], tm), lambda i, t: (i, 0, t)),
        pl.BlockSpec((1, 8, d), (lambda i, t: (i, 0, 0)) if per_batch_mod else (lambda i, t: (0, 0, 0))),
        _const_spec(gains.shape), _const_spec(wo.shape), _const_spec(wfi.shape), _const_spec(wfo.shape),
    ]
    args = [x, a_t, mod, gains, wo, wfi, wfo]
    scratch = [pltpu.VMEM((tm, FFN_HIDDEN), BF16)]
    if pool:
        u, wpool, pscale = pool_args
        nu = u.shape[2]
        per_tile = tm // POOL_HALO
        n_halo_blocks = s // POOL_HALO
        in_specs += [
            pl.BlockSpec((1, tm, nu), lambda i, t: (i, t, 0)),
            pl.BlockSpec((1, POOL_HALO, nu), lambda i, t: (i, jnp.maximum(t * per_tile - 1, 0), 0)),
            pl.BlockSpec((1, POOL_HALO, nu),
                         lambda i, t: (i, jnp.minimum((t + 1) * per_tile, n_halo_blocks - 1), 0)),
            _const_spec(wpool.shape), _const_spec(pscale.shape),
        ]
        args += [u, u, u, wpool, pscale]
        scratch.append(pltpu.VMEM((tm + 2 * POOL_HALO, nu), F32))
    return pl.pallas_call(
        functools.partial(_post_kernel, pool=pool, tm=tm, seq=s),
        grid=(b, s // tm),
        in_specs=in_specs,
        out_specs=pl.BlockSpec((1, tm, d), lambda i, t: (i, t, 0)),
        out_shape=jax.ShapeDtypeStruct((b, s, d), F32),
        scratch_shapes=scratch,
        compiler_params=_params(),
        name="post_even" if pool else "post_odd",
    )(*args)


def _pad_rows(a, rows):
    return jnp.pad(a, [(0, 0)] * (a.ndim - 2) + [(0, rows - a.shape[-2]), (0, 0)])


def kernel(x, c, ctx, c_ctx, w_mod, b_mod, g_pre_mix, g_post_mix, g_pre_ffn, g_post_ffn, we_in, we_out,
           we_q_gain, we_k_gain, we_pool, we_pool_scale, wo_in, wo_out, wo_sink, w_ffn_in, w_ffn_out):
    batch, seq, d = x.shape
    c_len = ctx.shape[1]
    depth = w_mod.shape[0]
    tm = TOKEN_TILE
    tm_ctx = c_len

    cc = jnp.concatenate([c, c_ctx[None, :]], axis=0)
    cc = _pad_rows(cc, -(-(batch + 1) // 8) * 8)
    mods = _modulation(cc, w_mod, b_mod)

    cos, sin = _rope_tables(seq)
    _, lane_dim = _lane_slot_dim()
    head_mean = jnp.asarray(_head_mean_matrix(), BF16)
    ffn_cols = _ffn_in_cols()

    for l in range(depth):
        even = l % 2 == 0
        i = l // 2
        with_ctx = l < depth - 1
        mod_x = _pad_rows(mods[l, :batch].reshape(batch, 6, d), 8)
        mod_c = _pad_rows(mods[l, batch:batch + 1].reshape(1, 6, d), 8)
        gains = _pad_rows(jnp.stack([g_pre_mix[l], g_post_mix[l], g_pre_ffn[l], g_post_ffn[l]]), 8)
        wfi = w_ffn_in[l][:, ffn_cols].astype(BF16)
        wfo = w_ffn_out[l].astype(BF16)

        if even:
            n_heads, n_kv = 8, 2
            w_full, w_out = we_in[i], we_out[i].astype(BF16)
            norm_args = (head_mean, we_q_gain[i][lane_dim][None, :], we_k_gain[i][lane_dim][None, :])
            sink = None
            mode = "dense"
        else:
            n_heads, n_kv = 16, 4
            w_full, w_out = wo_in[i], wo_out[i].astype(BF16)
            norm_args = None
            sink = wo_sink[i]
            mode = "window"
        q_w, kv_w = n_heads * HEAD_DIM, n_kv * HEAD_DIM
        cols = np.concatenate([_q_cols(n_heads), _k_cols(n_kv, q_w), np.arange(q_w + kv_w, w_full.shape[1])])
        w_in = w_full[:, cols].astype(BF16)
        widths = (q_w, kv_w, kv_w, w_full.shape[1] - q_w - 2 * kv_w)

        lat = _pre(x, mod_x, gains, w_in, norm_args, (cos, sin), widths=widths, tm=tm)
        con = _pre(ctx, mod_c, gains, w_in, norm_args, None, widths=widths, tm=tm_ctx)
        a_t = _attention(lat[0], (lat[1], lat[2]), (con[1], con[2]), sink, mode=mode, tq=QUERY_TILE)
        pool_w = (we_pool[i].astype(BF16), we_pool_scale[i][None, :]) if even else None
        x = _post(x, a_t, mod_x, gains, w_out, wfi, wfo, (lat[3],) + pool_w if even else None, tm=tm)
        if with_ctx:
            ac_t = _attention(con[0], None, (con[1], con[2]), sink, mode="ctx", tq=c_len)
            ctx = _post(ctx, ac_t, mod_c, gains, w_out, wfi, wfo,
                        (con[3],) + pool_w if even else None, tm=tm_ctx)
    return x
```

```python
import functools
import math

import numpy as np
import jax
import jax.numpy as jnp
from jax import lax
from jax.experimental import pallas as pl
from jax.experimental.pallas import tpu as pltpu

D_MODEL = 1024
HEAD_DIM = 64
GQA_GROUP = 4
GRID_W = 64
ROPE_THETA = 10000.0
EPS = 1e-6
WINDOW = 128
POOL_WINDOWS = (2, 4, 8, 16)
POOL_HALO = 8
FFN_HIDDEN = 2816
FFN_CHUNK = 256
LANES = 128
LOG2E = math.log2(math.e)
Q_SCALE = LOG2E / math.sqrt(HEAD_DIM)
VMEM_LIMIT = 56 * 1024 * 1024
TOKEN_TILE = 512
QUERY_TILE = 256
QUERY_SUB = LANES
VT_ONES_ROWS = 16
VT_CHUNK = LANES + VT_ONES_ROWS

F32 = jnp.float32
BF16 = jnp.bfloat16


def _lane_slot_dim():
    lane = np.arange(LANES)
    part = lane // 32
    i = lane % 32
    dim = np.where(i < 16, i, 32 + (i - 16)) + np.where(part >= 2, 16, 0)
    return part % 2, dim


def _chunk_heads(c):
    head_a = (c // GQA_GROUP) * 2 * GQA_GROUP + c % GQA_GROUP
    return head_a, head_a + GQA_GROUP


def _q_cols(n_heads):
    slot, dim = _lane_slot_dim()
    cols = []
    for c in range(n_heads // 2):
        head_a, head_b = _chunk_heads(c)
        cols.append(np.where(slot == 0, head_a, head_b) * HEAD_DIM + dim)
    return np.concatenate(cols)


def _k_cols(n_kv, base):
    slot, dim = _lane_slot_dim()
    return np.concatenate([base + (2 * m + slot) * HEAD_DIM + dim for m in range(n_kv // 2)])


def _ffn_in_cols():
    cols = []
    for i in range(FFN_HIDDEN // FFN_CHUNK):
        cols.append(np.arange(i * FFN_CHUNK, (i + 1) * FFN_CHUNK))
        cols.append(FFN_HIDDEN + np.arange(i * FFN_CHUNK, (i + 1) * FFN_CHUNK))
    return np.concatenate(cols)


def _head_mean_matrix():
    slot, _ = _lane_slot_dim()
    return (slot[:, None] == slot[None, :]).astype(np.float32) / HEAD_DIM


def _rope_tables(seq):
    quarter = HEAD_DIM // 4
    freqs = ROPE_THETA ** (-jnp.arange(quarter, dtype=F32) / quarter)
    t = jnp.arange(seq, dtype=jnp.int32)
    rows = (t // GRID_W).astype(F32)[:, None] * freqs[None, :]
    cols = (t % GRID_W).astype(F32)[:, None] * freqs[None, :]
    ang = jnp.concatenate([rows, cols], axis=-1)
    cos = jnp.tile(jnp.cos(ang), (1, 4))
    sin = jnp.sin(ang)
    sin = jnp.concatenate([-sin, -sin, sin, sin], axis=-1)
    return cos, sin


def _rms_normalise(x):
    return x * lax.rsqrt(jnp.mean(x * x, axis=-1, keepdims=True) + EPS)


def _silu(x):
    return x * (1.0 / (1.0 + jnp.exp(-x)))


def _dot(a, b):
    return jnp.dot(a, b, preferred_element_type=F32)


def _dot_tn(a_t, b):
    return lax.dot_general(a_t, b, (((0,), (0,)), ((), ())), preferred_element_type=F32)


def _const_spec(shape):
    return pl.BlockSpec(shape, lambda *_: (0,) * len(shape), pipeline_mode=pl.Buffered(1))


def _params():
    return pltpu.CompilerParams(dimension_semantics=("parallel", "parallel"), vmem_limit_bytes=VMEM_LIMIT)


def _mod_kernel(c_ref, w_ref, b_ref, o_ref):
    s = _silu(c_ref[...]).astype(BF16)
    o_ref[0] = _dot(s, w_ref[0].astype(BF16)) + b_ref[0]


def _modulation(cc, w_mod, b_mod):
    depth, d, n = w_mod.shape
    rows = cc.shape[0]
    tn = 1536
    return pl.pallas_call(
        _mod_kernel,
        grid=(depth, n // tn),
        in_specs=[
            pl.BlockSpec((rows, d), lambda l, j: (0, 0)),
            pl.BlockSpec((1, d, tn), lambda l, j: (l, 0, j)),
            pl.BlockSpec((1, 1, tn), lambda l, j: (l, 0, j)),
        ],
        out_specs=pl.BlockSpec((1, rows, tn), lambda l, j: (l, 0, j)),
        out_shape=jax.ShapeDtypeStruct((depth, rows, n), F32),
        compiler_params=_params(),
        name="modulation",
    )(cc, w_mod, b_mod.reshape(depth, 1, n))


def _pre_kernel(*refs, n_q, n_k, n_v, n_u, qk_norm, rope):
    x_ref, mod_ref, gains_ref, w_ref = refs[:4]
    pos = 4
    if qk_norm:
        pm_ref, qg_ref, kg_ref = refs[pos:pos + 3]
        pos += 3
    if rope:
        cos_ref, sin_ref = refs[pos:pos + 2]
        pos += 2
    qt_ref, k_ref, vt_ref = refs[pos:pos + 3]
    u_ref = refs[pos + 3] if n_u else None

    x = x_ref[0]
    h = _rms_normalise(x) * gains_ref[0:1, :]
    h = h * (1.0 + mod_ref[0, 1:2, :]) + mod_ref[0, 0:1, :]
    p = _dot(h.astype(BF16), w_ref[...])

    def head_chunk(c, gain_ref):
        if qk_norm:
            c2 = c * c
            hi = c2.astype(BF16)
            lo = (c2 - hi.astype(F32)).astype(BF16)
            ms = _dot(hi, pm_ref[...]) + _dot(lo, pm_ref[...])
            c = c * lax.rsqrt(ms + EPS) * gain_ref[...]
        if rope:
            c = c * cos_ref[...] + pltpu.roll(c, LANES // 2, 1) * sin_ref[...]
        return c

    for j in range(n_q // LANES):
        sl = slice(j * LANES, (j + 1) * LANES)
        c = head_chunk(p[:, sl], qg_ref if qk_norm else None) * Q_SCALE
        qt_ref[0, sl, :] = c.T.astype(BF16)
    for j in range(n_k // LANES):
        sl = slice(j * LANES, (j + 1) * LANES)
        c = head_chunk(p[:, n_q + j * LANES:n_q + (j + 1) * LANES], kg_ref if qk_norm else None)
        k_ref[0, :, sl] = c.astype(BF16)
    ones = jnp.ones((VT_ONES_ROWS, x.shape[0]), BF16)
    for j in range(n_v // LANES):
        r0 = j * VT_CHUNK
        vt_ref[0, r0:r0 + LANES, :] = (
            p[:, n_q + n_k + j * LANES:n_q + n_k + (j + 1) * LANES].T.astype(BF16))
        vt_ref[0, r0 + LANES:r0 + VT_CHUNK, :] = ones
    if n_u:
        u_ref[0] = p[:, n_q + n_k + n_v:]


def _pre(x, mod, gains, w, norm_args, rope_args, *, widths, tm):
    n_q, n_k, n_v, n_u = widths
    b, s, d = x.shape
    per_batch_mod = mod.shape[0] > 1
    qk_norm = norm_args is not None
    rope = rope_args is not None
    in_specs = [
        pl.BlockSpec((1, tm, d), lambda i, t: (i, t, 0)),
        pl.BlockSpec((1, 8, d), (lambda i, t: (i, 0, 0)) if per_batch_mod else (lambda i, t: (0, 0, 0))),
        _const_spec(gains.shape),
        _const_spec(w.shape),
    ]
    args = [x, mod, gains, w]
    if qk_norm:
        in_specs += [_const_spec(a.shape) for a in norm_args]
        args += list(norm_args)
    if rope:
        in_specs += [pl.BlockSpec((tm, LANES), lambda i, t: (t, 0))] * 2
        args += list(rope_args)
    vt_rows = n_v // LANES * VT_CHUNK
    out_shape = [jax.ShapeDtypeStruct((b, n_q, s), BF16), jax.ShapeDtypeStruct((b, s, n_k), BF16),
                 jax.ShapeDtypeStruct((b, vt_rows, s), BF16)]
    out_specs = [pl.BlockSpec((1, n_q, tm), lambda i, t: (i, 0, t)),
                 pl.BlockSpec((1, tm, n_k), lambda i, t: (i, t, 0)),
                 pl.BlockSpec((1, vt_rows, tm), lambda i, t: (i, 0, t))]
    if n_u:
        out_shape.append(jax.ShapeDtypeStruct((b, s, n_u), F32))
        out_specs.append(pl.BlockSpec((1, tm, n_u), lambda i, t: (i, t, 0)))
    return pl.pallas_call(
        functools.partial(_pre_kernel, n_q=n_q, n_k=n_k, n_v=n_v, n_u=n_u, qk_norm=qk_norm, rope=rope),
        grid=(b, s // tm),
        in_specs=in_specs,
        out_specs=out_specs,
        out_shape=out_shape,
        compiler_params=_params(),
        name="pre_even" if qk_norm else "pre_odd",
    )(*args)


def _attn_kernel(*refs, mode, n_pairs, n_sub, tq, seq, use_sink):
    qt_ref = refs[0]
    pos = 1
    if mode != "ctx":
        k_ref, vt_ref = refs[pos:pos + 2]
        pos += 2
    kc_ref, vct_ref = refs[pos:pos + 2]
    pos += 2
    if use_sink:
        sink_ref = refs[pos]
        pos += 1
    o_ref = refs[pos]

    sub = QUERY_SUB
    row = lax.broadcasted_iota(jnp.int32, (LANES, 1), 0)
    slot_a = (row // 32) % 2 == 0
    band = sub + 2 * WINDOW
    if mode == "window":
        rel = (lax.broadcasted_iota(jnp.int32, (band, sub), 0)
               - lax.broadcasted_iota(jnp.int32, (band, sub), 1))

    def block_geometry(sb):
        q0 = pl.program_id(1) * tq + sb * sub
        start = pl.multiple_of(jnp.clip(q0 - WINDOW, 0, seq - band), LANES)
        bias = jnp.where(jnp.abs(rel + (start - q0)) <= WINDOW, 0.0, -jnp.inf).astype(F32)
        return start, jnp.concatenate([bias] * 4, axis=1)

    def scores(sb, item, geometry):
        pair = (2 * item) // GQA_GROUP
        psl = slice(pair * LANES, (pair + 1) * LANES)
        cols = []
        for c in (2 * item, 2 * item + 1):
            qt = qt_ref[0, c * LANES:(c + 1) * LANES, sb * sub:(sb + 1) * sub]
            zero = jnp.zeros_like(qt)
            cols += [jnp.where(slot_a, qt, zero), jnp.where(slot_a, zero, qt)]
        rhs = jnp.concatenate(cols, axis=1)
        out = []
        if mode == "dense":
            out.append(_dot(k_ref[0, :, psl], rhs))
        elif mode == "window":
            start, bias = geometry
            out.append(_dot(k_ref[0, pl.ds(start, band), psl], rhs) + bias)
        out.append(_dot(kc_ref[0, :, psl], rhs))
        return out

    def finish(sb, item, geometry, s_list):
        pair = (2 * item) // GQA_GROUP
        psl = slice(pair * LANES, (pair + 1) * LANES)
        heads = _chunk_heads(2 * item) + _chunk_heads(2 * item + 1)
        vsl = slice(pair * VT_CHUNK, (pair + 1) * VT_CHUNK)
        values = []
        if mode == "dense":
            values.append(vt_ref[0, vsl, :])
        elif mode == "window":
            values.append(vt_ref[0, vsl, pl.ds(geometry[0], band)])
        values.append(vct_ref[0, vsl, :])

        m = functools.reduce(jnp.maximum, [s.max(axis=0, keepdims=True) for s in s_list])
        if use_sink:
            sink = jnp.concatenate([jnp.full((1, sub), sink_ref[h], F32) for h in heads], axis=1) * LOG2E
            m = jnp.maximum(m, sink)
        o4 = None
        for s, v_t in zip(s_list, values):
            p = jnp.exp2(s - m).astype(BF16)
            pv = _dot(v_t, p)
            o4 = pv if o4 is None else o4 + pv
        denom = o4[LANES:LANES + 1, :]
        if use_sink:
            denom = denom + jnp.exp2(sink - m)
        o4 = o4[:LANES, :] * (1.0 / denom)
        for j, h in enumerate(heads):
            rows = slice(0, HEAD_DIM) if j % 2 == 0 else slice(HEAD_DIM, 2 * HEAD_DIM)
            o_ref[0, h * HEAD_DIM:(h + 1) * HEAD_DIM, sb * sub:(sb + 1) * sub] = (
                o4[rows, j * sub:(j + 1) * sub].astype(BF16))

    items = [(sb, it) for sb in range(n_sub) for it in range(2 * n_pairs)]
    geometries = [block_geometry(sb) if mode == "window" else None for sb in range(n_sub)]
    pending = scores(*items[0], geometries[items[0][0]])
    for i, (sb, it) in enumerate(items):
        upcoming = None
        if i + 1 < len(items):
            nsb, nit = items[i + 1]
            upcoming = scores(nsb, nit, geometries[nsb])
        finish(sb, it, geometries[sb], pending)
        pending = upcoming


def _attention(qt, kv, kv_ctx, sink, *, mode, tq):
    b, wq, sq = qt.shape
    n_pairs = wq // (2 * GQA_GROUP * HEAD_DIM)
    kc, vct = kv_ctx
    c_len = kc.shape[1]
    use_sink = sink is not None
    in_specs = [pl.BlockSpec((1, wq, tq), lambda i, t: (i, 0, t))]
    args = [qt]
    seq = sq
    if mode != "ctx":
        k, vt = kv
        seq = k.shape[1]
        in_specs += [pl.BlockSpec((1, seq, k.shape[2]), lambda i, t: (i, 0, 0)),
                     pl.BlockSpec((1, vt.shape[1], seq), lambda i, t: (i, 0, 0))]
        args += [k, vt]
    in_specs += [pl.BlockSpec((1, c_len, kc.shape[2]), lambda i, t: (i, 0, 0)),
                 pl.BlockSpec((1, vct.shape[1], c_len), lambda i, t: (i, 0, 0))]
    args += [kc, vct]
    if use_sink:
        in_specs.append(pl.BlockSpec(memory_space=pltpu.SMEM))
        args.append(sink)
    return pl.pallas_call(
        functools.partial(_attn_kernel, mode=mode, n_pairs=n_pairs, n_sub=tq // QUERY_SUB, tq=tq, seq=seq,
                          use_sink=use_sink),
        grid=(b, sq // tq),
        in_specs=in_specs,
        out_specs=pl.BlockSpec((1, wq, tq), lambda i, t: (i, 0, t)),
        out_shape=jax.ShapeDtypeStruct((b, wq, sq), BF16),
        compiler_params=_params(),
        name="attn_" + mode,
    )(*args)


def _post_kernel(*refs, pool, tm, seq):
    x_ref, at_ref, mod_ref, gains_ref, wo_ref, wfi_ref, wfo_ref = refs[:7]
    pos = 7
    if pool:
        u_ref, uprev_ref, unext_ref, wpool_ref, pscale_ref = refs[pos:pos + 5]
        pos += 5
    o_ref = refs[pos]
    hid_ref = refs[pos + 1]
    if pool:
        ext_ref = refs[pos + 2]

    n_a = at_ref.shape[1]
    if pool:
        t = pl.program_id(1)
        last = pl.num_programs(1) - 1
        halo = jnp.zeros((POOL_HALO, ext_ref.shape[1]), F32)
        ext_ref[0:POOL_HALO, :] = jnp.where(t > 0, uprev_ref[0], halo)
        ext_ref[POOL_HALO:POOL_HALO + tm, :] = u_ref[0]
        ext_ref[POOL_HALO + tm:, :] = jnp.where(t < last, unext_ref[0], halo)

    def pool_mix(r0, rows):
        n_ext = rows + 2 * POOL_HALO
        tok = t * tm + r0 + lax.broadcasted_iota(jnp.int32, (rows, 1), 0)

        def ahead(v, k):
            return pltpu.roll(v, n_ext - k, 0)

        def behind(v, k):
            return pltpu.roll(v, k, 0)

        mixed = []
        for g, w in enumerate(POOL_WINDOWS):
            sl = slice(g * LANES, (g + 1) * LANES)
            e = ext_ref[r0:r0 + n_ext, sl]
            pair = e + ahead(e, 1)
            if w == 2:
                total = behind(pair, 1)
            elif w == 4:
                total = pair + behind(pair, 2)
            else:
                quad = pair + ahead(pair, 2)
                if w == 8:
                    total = quad + behind(quad, 4)
                else:
                    octet = quad + ahead(quad, 4)
                    total = octet + behind(octet, 8)
            total = total[POOL_HALO:POOL_HALO + rows]
            cnt = jnp.minimum(tok + (w - w // 2), seq) - jnp.maximum(tok - w // 2, 0)
            diff = total / cnt.astype(F32) - e[POOL_HALO:POOL_HALO + rows]
            mixed.append((_dot(diff.astype(BF16), wpool_ref[g]) * pscale_ref[:, sl]).astype(BF16))
        return jnp.concatenate(mixed, axis=-1)

    def mixer_out(r0, rows):
        y = _dot_tn(at_ref[0, :, r0:r0 + rows], wo_ref[0:n_a, :])
        if pool:
            y = y + _dot(pool_mix(r0, rows), wo_ref[n_a:, :])
        return y

    def residual_and_norm(r0, rows, y):
        x1 = x_ref[0, r0:r0 + rows, :] + mod_ref[0, 2:3, :] * (_rms_normalise(y) * gains_ref[1:2, :])
        h = _rms_normalise(x1) * gains_ref[2:3, :]
        return x1, (h * (1.0 + mod_ref[0, 4:5, :]) + mod_ref[0, 3:4, :]).astype(BF16)

    def ffn_hidden(r0, rows, h):
        for i in range(FFN_HIDDEN // FFN_CHUNK):
            gu = _dot(h, wfi_ref[:, 2 * i * FFN_CHUNK:2 * (i + 1) * FFN_CHUNK])
            gate, up = gu[:, :FFN_CHUNK], gu[:, FFN_CHUNK:]
            hid_ref[r0:r0 + rows, i * FFN_CHUNK:(i + 1) * FFN_CHUNK] = (_silu(gate) * up).astype(BF16)

    def ffn_out(r0, rows, x1):
        z = _dot(hid_ref[r0:r0 + rows, :], wfo_ref[...])
        o_ref[0, r0:r0 + rows, :] = x1 + mod_ref[0, 5:6, :] * (_rms_normalise(z) * gains_ref[3:4, :])

    rows = tm // 2
    y_a = mixer_out(0, rows)
    y_b = mixer_out(rows, rows)
    x1_a, h_a = residual_and_norm(0, rows, y_a)
    ffn_hidden(0, rows, h_a)
    x1_b, h_b = residual_and_norm(rows, rows, y_b)
    ffn_out(0, rows, x1_a)
    ffn_hidden(rows, rows, h_b)
    ffn_out(rows, rows, x1_b)


def _post(x, a_t, mod, gains, wo, wfi, wfo, pool_args, *, tm):
    b, s, d = x.shape
    per_batch_mod = mod.shape[0] > 1
    pool = pool_args is not None
    in_specs = [
        pl.BlockSpec((1, tm, d), lambda i, t: (i, t, 0)),
        pl.BlockSpec((1, a_t.shape[1], tm), lambda i, t: (i, 0, t)),
        pl.BlockSpec((1, 8, d), (lambda i, t: (i, 0, 0)) if per_batch_mod else (lambda i, t: (0, 0, 0))),
        _const_spec(gains.shape), _const_spec(wo.shape), _const_spec(wfi.shape), _const_spec(wfo.shape),
    ]
    args = [x, a_t, mod, gains, wo, wfi, wfo]
    scratch = [pltpu.VMEM((tm, FFN_HIDDEN), BF16)]
    if pool:
        u, wpool, pscale = pool_args
        nu = u.shape[2]
        per_tile = tm // POOL_HALO
        n_halo_blocks = s // POOL_HALO
        in_specs += [
            pl.BlockSpec((1, tm, nu), lambda i, t: (i, t, 0)),
            pl.BlockSpec((1, POOL_HALO, nu), lambda i, t: (i, jnp.maximum(t * per_tile - 1, 0), 0)),
            pl.BlockSpec((1, POOL_HALO, nu),
                         lambda i, t: (i, jnp.minimum((t + 1) * per_tile, n_halo_blocks - 1), 0)),
            _const_spec(wpool.shape), _const_spec(pscale.shape),
        ]
        args += [u, u, u, wpool, pscale]
        scratch.append(pltpu.VMEM((tm + 2 * POOL_HALO, nu), F32))
    return pl.pallas_call(
        functools.partial(_post_kernel, pool=pool, tm=tm, seq=s),
        grid=(b, s // tm),
        in_specs=in_specs,
        out_specs=pl.BlockSpec((1, tm, d), lambda i, t: (i, t, 0)),
        out_shape=jax.ShapeDtypeStruct((b, s, d), F32),
        scratch_shapes=scratch,
        compiler_params=_params(),
        name="post_even" if pool else "post_odd",
    )(*args)


def _pad_rows(a, rows):
    return jnp.pad(a, [(0, 0)] * (a.ndim - 2) + [(0, rows - a.shape[-2]), (0, 0)])


def kernel(x, c, ctx, c_ctx, w_mod, b_mod, g_pre_mix, g_post_mix, g_pre_ffn, g_post_ffn, we_in, we_out,
           we_q_gain, we_k_gain, we_pool, we_pool_scale, wo_in, wo_out, wo_sink, w_ffn_in, w_ffn_out):
    batch, seq, d = x.shape
    c_len = ctx.shape[1]
    depth = w_mod.shape[0]
    tm = TOKEN_TILE
    tm_ctx = c_len

    cc = jnp.concatenate([c, c_ctx[None, :]], axis=0)
    cc = _pad_rows(cc, -(-(batch + 1) // 8) * 8)
    mods = _modulation(cc, w_mod, b_mod)

    cos, sin = _rope_tables(seq)
    _, lane_dim = _lane_slot_dim()
    head_mean = jnp.asarray(_head_mean_matrix(), BF16)
    ffn_cols = _ffn_in_cols()

    for l in range(depth):
        even = l % 2 == 0
        i = l // 2
        with_ctx = l < depth - 1
        mod_x = _pad_rows(mods[l, :batch].reshape(batch, 6, d), 8)
        mod_c = _pad_rows(mods[l, batch:batch + 1].reshape(1, 6, d), 8)
        gains = _pad_rows(jnp.stack([g_pre_mix[l], g_post_mix[l], g_pre_ffn[l], g_post_ffn[l]]), 8)
        wfi = w_ffn_in[l][:, ffn_cols].astype(BF16)
        wfo = w_ffn_out[l].astype(BF16)

        if even:
            n_heads, n_kv = 8, 2
            w_full, w_out = we_in[i], we_out[i].astype(BF16)
            norm_args = (head_mean, we_q_gain[i][lane_dim][None, :], we_k_gain[i][lane_dim][None, :])
            sink = None
            mode = "dense"
        else:
            n_heads, n_kv = 16, 4
            w_full, w_out = wo_in[i], wo_out[i].astype(BF16)
            norm_args = None
            sink = wo_sink[i]
            mode = "window"
        q_w, kv_w = n_heads * HEAD_DIM, n_kv * HEAD_DIM
        cols = np.concatenate([_q_cols(n_heads), _k_cols(n_kv, q_w), np.arange(q_w + kv_w, w_full.shape[1])])
        w_in = w_full[:, cols].astype(BF16)
        widths = (q_w, kv_w, kv_w, w_full.shape[1] - q_w - 2 * kv_w)

        lat = _pre(x, mod_x, gains, w_in, norm_args, (cos, sin), widths=widths, tm=tm)
        con = _pre(ctx, mod_c, gains, w_in, norm_args, None, widths=widths, tm=tm_ctx)
        a_t = _attention(lat[0], (lat[1], lat[2]), (con[1], con[2]), sink, mode=mode, tq=QUERY_TILE)
        pool_w = (we_pool[i].astype(BF16), we_pool_scale[i][None, :]) if even else None
        x = _post(x, a_t, mod_x, gains, w_out, wfi, wfo, (lat[3],) + pool_w if even else None, tm=tm)
        if with_ctx:
            ac_t = _attention(con[0], None, (con[1], con[2]), sink, mode="ctx", tq=c_len)
            ctx = _post(ctx, ac_t, mod_c, gains, w_out, wfi, wfo,
                        (con[3],) + pool_w if even else None, tm=tm_ctx)
    return x
```

```python
import functools
import math

import numpy as np
import jax
import jax.numpy as jnp
from jax import lax
from jax.experimental import pallas as pl
from jax.experimental.pallas import tpu as pltpu

D_MODEL = 1024
HEAD_DIM = 64
GQA_GROUP = 4
GRID_W = 64
ROPE_THETA = 10000.0
EPS = 1e-6
WINDOW = 128
POOL_WINDOWS = (2, 4, 8, 16)
POOL_HALO = 8
FFN_HIDDEN = 2816
FFN_CHUNK = 256
LANES = 128
LOG2E = math.log2(math.e)
Q_SCALE = LOG2E / math.sqrt(HEAD_DIM)
VMEM_LIMIT = 56 * 1024 * 1024
TOKEN_TILE = 512
QUERY_TILE = 512
QUERY_SUB = LANES
KEY_BLOCK = 256
VT_ONES_ROWS = 16
VT_CHUNK = LANES + VT_ONES_ROWS

F32 = jnp.float32
BF16 = jnp.bfloat16


def _lane_slot_dim():
    lane = np.arange(LANES)
    part = lane // 32
    i = lane % 32
    dim = np.where(i < 16, i, 32 + (i - 16)) + np.where(part >= 2, 16, 0)
    return part % 2, dim


def _chunk_heads(c):
    head_a = (c // GQA_GROUP) * 2 * GQA_GROUP + c % GQA_GROUP
    return head_a, head_a + GQA_GROUP


def _q_cols(n_heads):
    slot, dim = _lane_slot_dim()
    cols = []
    for c in range(n_heads // 2):
        head_a, head_b = _chunk_heads(c)
        cols.append(np.where(slot == 0, head_a, head_b) * HEAD_DIM + dim)
    return np.concatenate(cols)


def _k_cols(n_kv, base):
    slot, dim = _lane_slot_dim()
    return np.concatenate([base + (2 * m + slot) * HEAD_DIM + dim for m in range(n_kv // 2)])


def _interleave_gate_up(w):
    d = w.shape[0]
    w = w.astype(BF16).reshape(d, 2, FFN_HIDDEN // FFN_CHUNK, FFN_CHUNK)
    return w.transpose(0, 2, 1, 3).reshape(d, 2 * FFN_HIDDEN)


def _head_mean_matrix():
    slot, _ = _lane_slot_dim()
    return (slot[:, None] == slot[None, :]).astype(np.float32) / HEAD_DIM


def _rope_tables(seq):
    quarter = HEAD_DIM // 4
    freqs = ROPE_THETA ** (-jnp.arange(quarter, dtype=F32) / quarter)
    t = jnp.arange(seq, dtype=jnp.int32)
    rows = (t // GRID_W).astype(F32)[:, None] * freqs[None, :]
    cols = (t % GRID_W).astype(F32)[:, None] * freqs[None, :]
    ang = jnp.concatenate([rows, cols], axis=-1)
    cos = jnp.tile(jnp.cos(ang), (1, 4))
    sin = jnp.sin(ang)
    sin = jnp.concatenate([-sin, -sin, sin, sin], axis=-1)
    return cos, sin


def _rms_normalise(x):
    return x * lax.rsqrt(jnp.mean(x * x, axis=-1, keepdims=True) + EPS)


def _silu(x):
    return x * (1.0 / (1.0 + jnp.exp(-x)))


def _dot(a, b):
    return jnp.dot(a, b, preferred_element_type=F32)


def _dot_tn(a_t, b):
    return lax.dot_general(a_t, b, (((0,), (0,)), ((), ())), preferred_element_type=F32)


def _const_spec(shape):
    return pl.BlockSpec(shape, lambda *_: (0,) * len(shape), pipeline_mode=pl.Buffered(1))


def _params():
    return pltpu.CompilerParams(dimension_semantics=("parallel", "parallel"), vmem_limit_bytes=VMEM_LIMIT)


def _mod_kernel(c_ref, w_ref, b_ref, o_ref):
    s = _silu(c_ref[...]).astype(BF16)
    o_ref[0] = _dot(s, w_ref[0].astype(BF16)) + b_ref[0]


def _modulation(cc, w_mod, b_mod):
    depth, d, n = w_mod.shape
    rows = cc.shape[0]
    tn = 1536
    return pl.pallas_call(
        _mod_kernel,
        grid=(depth, n // tn),
        in_specs=[
            pl.BlockSpec((rows, d), lambda l, j: (0, 0)),
            pl.BlockSpec((1, d, tn), lambda l, j: (l, 0, j)),
            pl.BlockSpec((1, 1, tn), lambda l, j: (l, 0, j)),
        ],
        out_specs=pl.BlockSpec((1, rows, tn), lambda l, j: (l, 0, j)),
        out_shape=jax.ShapeDtypeStruct((depth, rows, n), F32),
        compiler_params=_params(),
        name="modulation",
    )(cc, w_mod, b_mod.reshape(depth, 1, n))


def _pre_kernel(*refs, n_q, n_k, n_v, n_u, qk_norm, rope):
    x_ref, mod_ref, gains_ref, w_ref = refs[:4]
    pos = 4
    if qk_norm:
        pm_ref, qg_ref, kg_ref = refs[pos:pos + 3]
        pos += 3
    if rope:
        cos_ref, sin_ref = refs[pos:pos + 2]
        pos += 2
    qt_ref, k_ref, vt_ref = refs[pos:pos + 3]
    u_ref = refs[pos + 3] if n_u else None

    x = x_ref[0]
    h = _rms_normalise(x) * gains_ref[0:1, :]
    h = h * (1.0 + mod_ref[0, 1:2, :]) + mod_ref[0, 0:1, :]
    p = _dot(h.astype(BF16), w_ref[...])

    def head_chunk(c, gain_ref):
        if qk_norm:
            c2 = c * c
            hi = c2.astype(BF16)
            lo = (c2 - hi.astype(F32)).astype(BF16)
            ms = _dot(hi, pm_ref[...]) + _dot(lo, pm_ref[...])
            c = c * lax.rsqrt(ms + EPS) * gain_ref[...]
        if rope:
            c = c * cos_ref[...] + pltpu.roll(c, LANES // 2, 1) * sin_ref[...]
        return c

    for j in range(n_q // LANES):
        sl = slice(j * LANES, (j + 1) * LANES)
        c = head_chunk(p[:, sl], qg_ref if qk_norm else None) * Q_SCALE
        qt_ref[0, sl, :] = c.T.astype(BF16)
    for j in range(n_k // LANES):
        sl = slice(j * LANES, (j + 1) * LANES)
        c = head_chunk(p[:, n_q + j * LANES:n_q + (j + 1) * LANES], kg_ref if qk_norm else None)
        k_ref[0, :, sl] = c.astype(BF16)
    ones = jnp.ones((VT_ONES_ROWS, x.shape[0]), BF16)
    for j in range(n_v // LANES):
        r0 = j * VT_CHUNK
        vt_ref[0, r0:r0 + LANES, :] = (
            p[:, n_q + n_k + j * LANES:n_q + n_k + (j + 1) * LANES].T.astype(BF16))
        vt_ref[0, r0 + LANES:r0 + VT_CHUNK, :] = ones
    if n_u:
        u_ref[0] = p[:, n_q + n_k + n_v:]


def _pre(x, mod, gains, w, norm_args, rope_args, *, widths, tm):
    n_q, n_k, n_v, n_u = widths
    b, s, d = x.shape
    per_batch_mod = mod.shape[0] > 1
    qk_norm = norm_args is not None
    rope = rope_args is not None
    in_specs = [
        pl.BlockSpec((1, tm, d), lambda i, t: (i, t, 0)),
        pl.BlockSpec((1, 8, d), (lambda i, t: (i, 0, 0)) if per_batch_mod else (lambda i, t: (0, 0, 0))),
        _const_spec(gains.shape),
        _const_spec(w.shape),
    ]
    args = [x, mod, gains, w]
    if qk_norm:
        in_specs += [_const_spec(a.shape) for a in norm_args]
        args += list(norm_args)
    if rope:
        in_specs += [pl.BlockSpec((tm, LANES), lambda i, t: (t, 0))] * 2
        args += list(rope_args)
    vt_rows = n_v // LANES * VT_CHUNK
    out_shape = [jax.ShapeDtypeStruct((b, n_q, s), BF16), jax.ShapeDtypeStruct((b, s, n_k), BF16),
                 jax.ShapeDtypeStruct((b, vt_rows, s), BF16)]
    out_specs = [pl.BlockSpec((1, n_q, tm), lambda i, t: (i, 0, t)),
                 pl.BlockSpec((1, tm, n_k), lambda i, t: (i, t, 0)),
                 pl.BlockSpec((1, vt_rows, tm), lambda i, t: (i, 0, t))]
    if n_u:
        out_shape.append(jax.ShapeDtypeStruct((b, s, n_u), F32))
        out_specs.append(pl.BlockSpec((1, tm, n_u), lambda i, t: (i, t, 0)))
    return pl.pallas_call(
        functools.partial(_pre_kernel, n_q=n_q, n_k=n_k, n_v=n_v, n_u=n_u, qk_norm=qk_norm, rope=rope),
        grid=(b, s // tm),
        in_specs=in_specs,
        out_specs=out_specs,
        out_shape=out_shape,
        compiler_params=_params(),
        name="pre_even" if qk_norm else "pre_odd",
    )(*args)


def _attn_kernel(*refs, mode, n_pairs, n_sub, tq, seq, use_sink):
    qt_ref = refs[0]
    pos = 1
    if mode != "ctx":
        k_ref, vt_ref = refs[pos:pos + 2]
        pos += 2
    kc_ref, vct_ref = refs[pos:pos + 2]
    pos += 2
    if use_sink:
        sink_ref = refs[pos]
        pos += 1
    o_ref, s_ref, p_ref = refs[pos:pos + 3]

    sub = QUERY_SUB
    row = lax.broadcasted_iota(jnp.int32, (LANES, 1), 0)
    slot_a = (row // 32) % 2 == 0
    band = sub + 2 * WINDOW
    if mode == "window":
        rel = (lax.broadcasted_iota(jnp.int32, (band, sub), 0)
               - lax.broadcasted_iota(jnp.int32, (band, sub), 1))

    def block_geometry(sb):
        q0 = pl.program_id(1) * tq + sb * sub
        start = pl.multiple_of(jnp.clip(q0 - WINDOW, 0, seq - band), LANES)
        bias = jnp.where(jnp.abs(rel + (start - q0)) <= WINDOW, 0.0, -jnp.inf).astype(F32)
        return start, jnp.concatenate([bias] * 4, axis=1)

    c_len = kc_ref.shape[1]
    first = {"dense": seq, "window": band, "ctx": 0}[mode]
    blocks = [(True, r, min(KEY_BLOCK, first - r), r) for r in range(0, first, KEY_BLOCK)]
    blocks += [(False, r, min(KEY_BLOCK, c_len - r), first + r) for r in range(0, c_len, KEY_BLOCK)]

    def item_heads(item):
        return _chunk_heads(2 * item) + _chunk_heads(2 * item + 1)

    def query_operand(sb, item):
        cols = []
        for c in (2 * item, 2 * item + 1):
            qt = qt_ref[0, c * LANES:(c + 1) * LANES, sb * sub:(sb + 1) * sub]
            zero = jnp.zeros_like(qt)
            cols += [jnp.where(slot_a, qt, zero), jnp.where(slot_a, zero, qt)]
        return jnp.concatenate(cols, axis=1)

    def score_block(slot, item, geometry, rhs, blk, m):
        latent, r0, rows, b0 = blk
        psl = slice((2 * item) // GQA_GROUP * LANES, ((2 * item) // GQA_GROUP + 1) * LANES)
        if not latent:
            s = _dot(kc_ref[0, r0:r0 + rows, psl], rhs)
        elif mode == "dense":
            s = _dot(k_ref[0, r0:r0 + rows, psl], rhs)
        else:
            start, bias = geometry
            s = _dot(k_ref[0, pl.ds(pl.multiple_of(start + r0, LANES), rows), psl], rhs) + bias[r0:r0 + rows]
        s_ref[slot, b0:b0 + rows, :] = s
        top = s.max(axis=0, keepdims=True)
        return top if m is None else jnp.maximum(m, top)

    def exp_block(slot, blk, m):
        _, _, rows, b0 = blk
        p_ref[slot, b0:b0 + rows, :] = jnp.exp2(s_ref[slot, b0:b0 + rows, :] - m).astype(BF16)

    def value_block(slot, item, geometry, blk, acc):
        latent, r0, rows, b0 = blk
        pair = (2 * item) // GQA_GROUP
        vsl = slice(pair * VT_CHUNK, (pair + 1) * VT_CHUNK)
        if not latent:
            v_t = vct_ref[0, vsl, r0:r0 + rows]
        elif mode == "dense":
            v_t = vt_ref[0, vsl, r0:r0 + rows]
        else:
            v_t = vt_ref[0, vsl, pl.ds(pl.multiple_of(geometry[0] + r0, LANES), rows)]
        pv = _dot(v_t, p_ref[slot, b0:b0 + rows, :])
        return pv if acc is None else acc + pv

    def with_sink(item, m):
        if not use_sink:
            return m, None
        sink = jnp.concatenate([jnp.full((1, sub), sink_ref[h], F32) for h in item_heads(item)], axis=1) * LOG2E
        m = jnp.maximum(m, sink)
        return m, jnp.exp2(sink - m)

    def store_output(sb, item, o4, sink_term):
        denom = o4[LANES:LANES + 1, :]
        if use_sink:
            denom = denom + sink_term
        o4 = o4[:LANES, :] * (1.0 / denom)
        for j, h in enumerate(item_heads(item)):
            rows = slice(0, HEAD_DIM) if j % 2 == 0 else slice(HEAD_DIM, 2 * HEAD_DIM)
            o_ref[0, h * HEAD_DIM:(h + 1) * HEAD_DIM, sb * sub:(sb + 1) * sub] = (
                o4[rows, j * sub:(j + 1) * sub].astype(BF16))

    items = [(sb, it) for sb in range(n_sub) for it in range(2 * n_pairs)]
    geometries = [block_geometry(sb) if mode == "window" else None for sb in range(n_sub)]
    n_items = len(items)
    m_cur = sink_cur = sink_prev = None
    for i in range(-1, n_items + 1):
        nxt, prv = i + 1, i - 1
        if nxt < n_items:
            rhs = query_operand(*items[nxt])
        m_next = o_acc = None
        for blk in blocks:
            if nxt < n_items:
                m_next = score_block(nxt % 2, items[nxt][1], geometries[items[nxt][0]], rhs, blk, m_next)
            if 0 <= i < n_items:
                exp_block(i % 2, blk, m_cur)
            if 0 <= prv:
                o_acc = value_block(prv % 2, items[prv][1], geometries[items[prv][0]], blk, o_acc)
        if 0 <= prv:
            store_output(*items[prv], o_acc, sink_prev)
        sink_prev = sink_cur
        if nxt < n_items:
            m_cur, sink_cur = with_sink(items[nxt][1], m_next)


def _attention(qt, kv, kv_ctx, sink, *, mode, tq):
    b, wq, sq = qt.shape
    n_pairs = wq // (2 * GQA_GROUP * HEAD_DIM)
    kc, vct = kv_ctx
    c_len = kc.shape[1]
    use_sink = sink is not None
    in_specs = [pl.BlockSpec((1, wq, tq), lambda i, t: (i, 0, t))]
    args = [qt]
    seq = sq
    if mode != "ctx":
        k, vt = kv
        seq = k.shape[1]
        in_specs += [pl.BlockSpec((1, seq, k.shape[2]), lambda i, t: (i, 0, 0)),
                     pl.BlockSpec((1, vt.shape[1], seq), lambda i, t: (i, 0, 0))]
        args += [k, vt]
    in_specs += [pl.BlockSpec((1, c_len, kc.shape[2]), lambda i, t: (i, 0, 0)),
                 pl.BlockSpec((1, vct.shape[1], c_len), lambda i, t: (i, 0, 0))]
    args += [kc, vct]
    if use_sink:
        in_specs.append(pl.BlockSpec(memory_space=pltpu.SMEM))
        args.append(sink)
    n_keys = c_len + {"dense": seq, "window": QUERY_SUB + 2 * WINDOW, "ctx": 0}[mode]
    return pl.pallas_call(
        functools.partial(_attn_kernel, mode=mode, n_pairs=n_pairs, n_sub=tq // QUERY_SUB, tq=tq, seq=seq,
                          use_sink=use_sink),
        grid=(b, sq // tq),
        in_specs=in_specs,
        out_specs=pl.BlockSpec((1, wq, tq), lambda i, t: (i, 0, t)),
        out_shape=jax.ShapeDtypeStruct((b, wq, sq), BF16),
        scratch_shapes=[pltpu.VMEM((2, n_keys, 4 * QUERY_SUB), F32),
                        pltpu.VMEM((2, n_keys, 4 * QUERY_SUB), BF16)],
        compiler_params=_params(),
        name="attn_" + mode,
    )(*args)


def _post_kernel(*refs, pool, tm, seq):
    x_ref, at_ref, mod_ref, gains_ref, wo_ref, wfi_ref, wfo_ref = refs[:7]
    pos = 7
    if pool:
        u_ref, uprev_ref, unext_ref, wpool_ref, pscale_ref = refs[pos:pos + 5]
        pos += 5
    o_ref = refs[pos]
    hid_ref = refs[pos + 1]
    if pool:
        ext_ref = refs[pos + 2]

    n_a = at_ref.shape[1]
    if pool:
        t = pl.program_id(1)
        last = pl.num_programs(1) - 1
        halo = jnp.zeros((POOL_HALO, ext_ref.shape[1]), F32)
        ext_ref[0:POOL_HALO, :] = jnp.where(t > 0, uprev_ref[0], halo)
        ext_ref[POOL_HALO:POOL_HALO + tm, :] = u_ref[0]
        ext_ref[POOL_HALO + tm:, :] = jnp.where(t < last, unext_ref[0], halo)

    def pool_mix(r0, rows):
        n_ext = rows + 2 * POOL_HALO
        tok = t * tm + r0 + lax.broadcasted_iota(jnp.int32, (rows, 1), 0)

        def ahead(v, k):
            return pltpu.roll(v, n_ext - k, 0)

        def behind(v, k):
            return pltpu.roll(v, k, 0)

        mixed = []
        for g, w in enumerate(POOL_WINDOWS):
            sl = slice(g * LANES, (g + 1) * LANES)
            e = ext_ref[r0:r0 + n_ext, sl]
            pair = e + ahead(e, 1)
            if w == 2:
                total = behind(pair, 1)
            elif w == 4:
                total = pair + behind(pair, 2)
            else:
                quad = pair + ahead(pair, 2)
                if w == 8:
                    total = quad + behind(quad, 4)
                else:
                    octet = quad + ahead(quad, 4)
                    total = octet + behind(octet, 8)
            total = total[POOL_HALO:POOL_HALO + rows]
            cnt = jnp.minimum(tok + (w - w // 2), seq) - jnp.maximum(tok - w // 2, 0)
            diff = total / cnt.astype(F32) - e[POOL_HALO:POOL_HALO + rows]
            mixed.append((_dot(diff.astype(BF16), wpool_ref[g]) * pscale_ref[:, sl]).astype(BF16))
        return jnp.concatenate(mixed, axis=-1)

    def mixer_out(r0, rows):
        y = _dot_tn(at_ref[0, :, r0:r0 + rows], wo_ref[0:n_a, :])
        if pool:
            y = y + _dot(pool_mix(r0, rows), wo_ref[n_a:, :])
        return y

    def residual_and_norm(r0, rows, y):
        x1 = x_ref[0, r0:r0 + rows, :] + mod_ref[0, 2:3, :] * (_rms_normalise(y) * gains_ref[1:2, :])
        h = _rms_normalise(x1) * gains_ref[2:3, :]
        return x1, (h * (1.0 + mod_ref[0, 4:5, :]) + mod_ref[0, 3:4, :]).astype(BF16)

    def ffn_hidden(r0, rows, h):
        for i in range(FFN_HIDDEN // FFN_CHUNK):
            gu = _dot(h, wfi_ref[:, 2 * i * FFN_CHUNK:2 * (i + 1) * FFN_CHUNK])
            gate, up = gu[:, :FFN_CHUNK], gu[:, FFN_CHUNK:]
            hid_ref[r0:r0 + rows, i * FFN_CHUNK:(i + 1) * FFN_CHUNK] = (_silu(gate) * up).astype(BF16)

    def ffn_out(r0, rows, x1):
        z = _dot(hid_ref[r0:r0 + rows, :], wfo_ref[...])
        o_ref[0, r0:r0 + rows, :] = x1 + mod_ref[0, 5:6, :] * (_rms_normalise(z) * gains_ref[3:4, :])

    rows = tm // 2
    y_a = mixer_out(0, rows)
    y_b = mixer_out(rows, rows)
    x1_a, h_a = residual_and_norm(0, rows, y_a)
    ffn_hidden(0, rows, h_a)
    x1_b, h_b = residual_and_norm(rows, rows, y_b)
    ffn_out(0, rows, x1_a)
    ffn_hidden(rows, rows, h_b)
    ffn_out(rows, rows, x1_b)


def _post(x, a_t, mod, gains, wo, wfi, wfo, pool_args, *, tm):
    b, s, d = x.shape
    per_batch_mod = mod.shape[0] > 1
    pool = pool_args is not None
    in_specs = [
        pl.BlockSpec((1, tm, d), lambda i, t: (i, t, 0)),
        pl.BlockSpec((1, a_t.shape[1], tm), lambda i, t: (i, 0, t)),
        pl.BlockSpec((1, 8, d), (lambda i, t: (i, 0, 0)) if per_batch_mod else (lambda i, t: (0, 0, 0))),
        _const_spec(gains.shape), _const_spec(wo.shape), _const_spec(wfi.shape), _const_spec(wfo.shape),
    ]
    args = [x, a_t, mod, gains, wo, wfi, wfo]
    scratch = [pltpu.VMEM((tm, FFN_HIDDEN), BF16)]
    if pool:
        u, wpool, pscale = pool_args
        nu = u.shape[2]
        per_tile = tm // POOL_HALO
        n_halo_blocks = s // POOL_HALO
        in_specs += [
            pl.BlockSpec((1, tm, nu), lambda i, t: (i, t, 0)),
            pl.BlockSpec((1, POOL_HALO, nu), lambda i, t: (i, jnp.maximum(t * per_tile - 1, 0), 0)),
            pl.BlockSpec((1, POOL_HALO, nu),
                         lambda i, t: (i, jnp.minimum((t + 1) * per_tile, n_halo_blocks - 1), 0)),
            _const_spec(wpool.shape), _const_spec(pscale.shape),
        ]
        args += [u, u, u, wpool, pscale]
        scratch.append(pltpu.VMEM((tm + 2 * POOL_HALO, nu), F32))
    return pl.pallas_call(
        functools.partial(_post_kernel, pool=pool, tm=tm, seq=s),
        grid=(b, s // tm),
        in_specs=in_specs,
        out_specs=pl.BlockSpec((1, tm, d), lambda i, t: (i, t, 0)),
        out_shape=jax.ShapeDtypeStruct((b, s, d), F32),
        scratch_shapes=scratch,
        compiler_params=_params(),
        name="post_even" if pool else "post_odd",
    )(*args)


def _pad_rows(a, rows):
    return jnp.pad(a, [(0, 0)] * (a.ndim - 2) + [(0, rows - a.shape[-2]), (0, 0)])


def kernel(x, c, ctx, c_ctx, w_mod, b_mod, g_pre_mix, g_post_mix, g_pre_ffn, g_post_ffn, we_in, we_out,
           we_q_gain, we_k_gain, we_pool, we_pool_scale, wo_in, wo_out, wo_sink, w_ffn_in, w_ffn_out):
    batch, seq, d = x.shape
    c_len = ctx.shape[1]
    depth = w_mod.shape[0]
    tm = TOKEN_TILE
    tm_ctx = c_len

    cc = jnp.concatenate([c, c_ctx[None, :]], axis=0)
    cc = _pad_rows(cc, -(-(batch + 1) // 8) * 8)
    mods = _modulation(cc, w_mod, b_mod)

    cos, sin = _rope_tables(seq)
    _, lane_dim = _lane_slot_dim()
    head_mean = jnp.asarray(_head_mean_matrix(), BF16)

    for l in range(depth):
        even = l % 2 == 0
        i = l // 2
        with_ctx = l < depth - 1
        mod_x = _pad_rows(mods[l, :batch].reshape(batch, 6, d), 8)
        mod_c = _pad_rows(mods[l, batch:batch + 1].reshape(1, 6, d), 8)
        gains = _pad_rows(jnp.stack([g_pre_mix[l], g_post_mix[l], g_pre_ffn[l], g_post_ffn[l]]), 8)
        wfi = _interleave_gate_up(w_ffn_in[l])
        wfo = w_ffn_out[l].astype(BF16)

        if even:
            n_heads, n_kv = 8, 2
            w_full, w_out = we_in[i], we_out[i].astype(BF16)
            norm_args = (head_mean, we_q_gain[i][lane_dim][None, :], we_k_gain[i][lane_dim][None, :])
            sink = None
            mode = "dense"
        else:
            n_heads, n_kv = 16, 4
            w_full, w_out = wo_in[i], wo_out[i].astype(BF16)
            norm_args = None
            sink = wo_sink[i]
            mode = "window"
        q_w, kv_w = n_heads * HEAD_DIM, n_kv * HEAD_DIM
        cols = np.concatenate([_q_cols(n_heads), _k_cols(n_kv, q_w), np.arange(q_w + kv_w, w_full.shape[1])])
        w_in = w_full[:, cols].astype(BF16)
        widths = (q_w, kv_w, kv_w, w_full.shape[1] - q_w - 2 * kv_w)

        lat = _pre(x, mod_x, gains, w_in, norm_args, (cos, sin), widths=widths, tm=tm)
        con = _pre(ctx, mod_c, gains, w_in, norm_args, None, widths=widths, tm=tm_ctx)
        a_t = _attention(lat[0], (lat[1], lat[2]), (con[1], con[2]), sink, mode=mode, tq=QUERY_TILE)
        pool_w = (we_pool[i].astype(BF16), we_pool_scale[i][None, :]) if even else None
        x = _post(x, a_t, mod_x, gains, w_out, wfi, wfo, (lat[3],) + pool_w if even else None, tm=tm)
        if with_ctx:
            ac_t = _attention(con[0], None, (con[1], con[2]), sink, mode="ctx", tq=c_len)
            ctx = _post(ctx, ac_t, mod_c, gains, w_out, wfi, wfo,
                        (con[3],) + pool_w if even else None, tm=tm_ctx)
    return x
```

```python
import functools
import math

import numpy as np
import jax
import jax.numpy as jnp
from jax import lax
from jax.experimental import pallas as pl
from jax.experimental.pallas import tpu as pltpu

D_MODEL = 1024
HEAD_DIM = 64
GQA_GROUP = 4
GRID_W = 64
ROPE_THETA = 10000.0
EPS = 1e-6
WINDOW = 128
POOL_WINDOWS = (2, 4, 8, 16)
POOL_HALO = 8
FFN_HIDDEN = 2816
FFN_CHUNK = 256
LANES = 128
LOG2E = math.log2(math.e)
Q_SCALE = LOG2E / math.sqrt(HEAD_DIM)
VMEM_LIMIT = 56 * 1024 * 1024
TOKEN_TILE = 512
QUERY_TILE = 512
QUERY_SUB = LANES
KEY_BLOCK = 256
VT_ONES_ROWS = 16
VT_CHUNK = LANES + VT_ONES_ROWS

F32 = jnp.float32
BF16 = jnp.bfloat16


def _lane_slot_dim():
    lane = np.arange(LANES)
    part = lane // 32
    i = lane % 32
    dim = np.where(i < 16, i, 32 + (i - 16)) + np.where(part >= 2, 16, 0)
    return part % 2, dim


def _chunk_heads(c):
    head_a = (c // GQA_GROUP) * 2 * GQA_GROUP + c % GQA_GROUP
    return head_a, head_a + GQA_GROUP


def _q_cols(n_heads):
    slot, dim = _lane_slot_dim()
    cols = []
    for c in range(n_heads // 2):
        head_a, head_b = _chunk_heads(c)
        cols.append(np.where(slot == 0, head_a, head_b) * HEAD_DIM + dim)
    return np.concatenate(cols)


def _k_cols(n_kv, base):
    slot, dim = _lane_slot_dim()
    return np.concatenate([base + (2 * m + slot) * HEAD_DIM + dim for m in range(n_kv // 2)])


def _head_mean_matrix():
    slot, _ = _lane_slot_dim()
    mean = (slot[:, None] == slot[None, :]).astype(np.float32) / HEAD_DIM
    return np.concatenate([mean, mean], axis=0)


def _rope_tables(seq):
    quarter = HEAD_DIM // 4
    freqs = ROPE_THETA ** (-jnp.arange(quarter, dtype=F32) / quarter)
    t = jnp.arange(seq, dtype=jnp.int32)
    rows = (t // GRID_W).astype(F32)[:, None] * freqs[None, :]
    cols = (t % GRID_W).astype(F32)[:, None] * freqs[None, :]
    ang = jnp.concatenate([rows, cols], axis=-1)
    cos = jnp.tile(jnp.cos(ang), (1, 4))
    sin = jnp.sin(ang)
    sin = jnp.concatenate([-sin, -sin, sin, sin], axis=-1)
    return cos, sin


def _rms_normalise(x):
    return x * lax.rsqrt(jnp.mean(x * x, axis=-1, keepdims=True) + EPS)


def _silu(x):
    return x * (1.0 / (1.0 + jnp.exp(-x)))


def _dot(a, b):
    return jnp.dot(a, b, preferred_element_type=F32)


def _dot_tn(a_t, b):
    return lax.dot_general(a_t, b, (((0,), (0,)), ((), ())), preferred_element_type=F32)


def _const_spec(shape):
    return pl.BlockSpec(shape, lambda *_: (0,) * len(shape), pipeline_mode=pl.Buffered(1))


def _params():
    return pltpu.CompilerParams(dimension_semantics=("parallel", "parallel"), vmem_limit_bytes=VMEM_LIMIT)


def _mod_kernel(c_ref, w_ref, b_ref, o_ref):
    s = _silu(c_ref[...]).astype(BF16)
    o_ref[0] = _dot(s, w_ref[0].astype(BF16)) + b_ref[0]


def _modulation(cc, w_mod, b_mod):
    depth, d, n = w_mod.shape
    rows = cc.shape[0]
    tn = 1536
    return pl.pallas_call(
        _mod_kernel,
        grid=(depth, n // tn),
        in_specs=[
            pl.BlockSpec((rows, d), lambda l, j: (0, 0)),
            pl.BlockSpec((1, d, tn), lambda l, j: (l, 0, j)),
            pl.BlockSpec((1, 1, tn), lambda l, j: (l, 0, j)),
        ],
        out_specs=pl.BlockSpec((1, rows, tn), lambda l, j: (l, 0, j)),
        out_shape=jax.ShapeDtypeStruct((depth, rows, n), F32),
        compiler_params=_params(),
        name="modulation",
    )(cc, w_mod, b_mod.reshape(depth, 1, n))


def _pre_kernel(*refs, n_q, n_k, n_v, n_u, qk_norm, rope):
    x_ref, mod_ref, gains_ref, w_ref = refs[:4]
    pos = 4
    if qk_norm:
        pm_ref, qg_ref, kg_ref = refs[pos:pos + 3]
        pos += 3
    if rope:
        cos_ref, sin_ref = refs[pos:pos + 2]
        pos += 2
    qt_ref, k_ref, vt_ref = refs[pos:pos + 3]
    u_ref = refs[pos + 3] if n_u else None

    def modulated(rows):
        h = _rms_normalise(x_ref[0, rows, :]) * gains_ref[0:1, :]
        return (h * (1.0 + mod_ref[0, 1:2, :]) + mod_ref[0, 0:1, :]).astype(BF16)

    def head_chunk(c, gain_ref, rows):
        if qk_norm:
            c2 = c * c
            hi = c2.astype(BF16)
            lo = (c2 - hi.astype(F32)).astype(BF16)
            ms = _dot(jnp.concatenate([hi, lo], axis=1), pm_ref[...])
            c = c * lax.rsqrt(ms + EPS) * gain_ref[...]
        if rope:
            c = c * cos_ref[rows, :] + pltpu.roll(c, LANES // 2, 1) * sin_ref[rows, :]
        return c

    def project(rows, h):
        p = _dot(h, w_ref[...])
        for j in range(n_q // LANES):
            sl = slice(j * LANES, (j + 1) * LANES)
            c = head_chunk(p[:, sl], qg_ref if qk_norm else None, rows) * Q_SCALE
            qt_ref[0, sl, rows] = c.T.astype(BF16)
        for j in range(n_k // LANES):
            sl = slice(j * LANES, (j + 1) * LANES)
            c = head_chunk(p[:, n_q + j * LANES:n_q + (j + 1) * LANES], kg_ref if qk_norm else None, rows)
            k_ref[0, rows, sl] = c.astype(BF16)
        ones = jnp.ones((VT_ONES_ROWS, p.shape[0]), BF16)
        for j in range(n_v // LANES):
            r0 = j * VT_CHUNK
            vt_ref[0, r0:r0 + LANES, rows] = (
                p[:, n_q + n_k + j * LANES:n_q + n_k + (j + 1) * LANES].T.astype(BF16))
            vt_ref[0, r0 + LANES:r0 + VT_CHUNK, rows] = ones
        if n_u:
            u_ref[0, rows, :] = p[:, n_q + n_k + n_v:]

    half = x_ref.shape[1] // 2
    rows_a, rows_b = slice(0, half), slice(half, 2 * half)
    h_a = modulated(rows_a)
    h_b = modulated(rows_b)
    project(rows_a, h_a)
    project(rows_b, h_b)


def _pre(x, mod, gains, w, norm_args, rope_args, *, widths, tm):
    n_q, n_k, n_v, n_u = widths
    b, s, d = x.shape
    per_batch_mod = mod.shape[0] > 1
    qk_norm = norm_args is not None
    rope = rope_args is not None
    in_specs = [
        pl.BlockSpec((1, tm, d), lambda i, t: (i, t, 0)),
        pl.BlockSpec((1, 8, d), (lambda i, t: (i, 0, 0)) if per_batch_mod else (lambda i, t: (0, 0, 0))),
        _const_spec(gains.shape),
        _const_spec(w.shape),
    ]
    args = [x, mod, gains, w]
    if qk_norm:
        in_specs += [_const_spec(a.shape) for a in norm_args]
        args += list(norm_args)
    if rope:
        in_specs += [pl.BlockSpec((tm, LANES), lambda i, t: (t, 0))] * 2
        args += list(rope_args)
    vt_rows = n_v // LANES * VT_CHUNK
    out_shape = [jax.ShapeDtypeStruct((b, n_q, s), BF16), jax.ShapeDtypeStruct((b, s, n_k), BF16),
                 jax.ShapeDtypeStruct((b, vt_rows, s), BF16)]
    out_specs = [pl.BlockSpec((1, n_q, tm), lambda i, t: (i, 0, t)),
                 pl.BlockSpec((1, tm, n_k), lambda i, t: (i, t, 0)),
                 pl.BlockSpec((1, vt_rows, tm), lambda i, t: (i, 0, t))]
    if n_u:
        out_shape.append(jax.ShapeDtypeStruct((b, s, n_u), F32))
        out_specs.append(pl.BlockSpec((1, tm, n_u), lambda i, t: (i, t, 0)))
    return pl.pallas_call(
        functools.partial(_pre_kernel, n_q=n_q, n_k=n_k, n_v=n_v, n_u=n_u, qk_norm=qk_norm, rope=rope),
        grid=(b, s // tm),
        in_specs=in_specs,
        out_specs=out_specs,
        out_shape=out_shape,
        compiler_params=_params(),
        name="pre_even" if qk_norm else "pre_odd",
    )(*args)


def _attn_kernel(*refs, mode, n_pairs, n_sub, tq, seq, use_sink):
    qt_ref = refs[0]
    pos = 1
    if mode != "ctx":
        k_ref, vt_ref = refs[pos:pos + 2]
        pos += 2
    kc_ref, vct_ref = refs[pos:pos + 2]
    pos += 2
    if use_sink:
        sink_ref = refs[pos]
        pos += 1
    o_ref, s_ref, p_ref = refs[pos:pos + 3]

    sub = QUERY_SUB
    row = lax.broadcasted_iota(jnp.int32, (LANES, 1), 0)
    slot_a = (row // 32) % 2 == 0
    band = sub + 2 * WINDOW
    if mode == "window":
        rel = (lax.broadcasted_iota(jnp.int32, (band, sub), 0)
               - lax.broadcasted_iota(jnp.int32, (band, sub), 1))

    def block_geometry(sb):
        q0 = pl.program_id(1) * tq + sb * sub
        start = pl.multiple_of(jnp.clip(q0 - WINDOW, 0, seq - band), LANES)
        bias = jnp.where(jnp.abs(rel + (start - q0)) <= WINDOW, 0.0, -jnp.inf).astype(F32)
        return start, jnp.concatenate([bias] * 4, axis=1)

    c_len = kc_ref.shape[1]
    first = {"dense": seq, "window": band, "ctx": 0}[mode]
    blocks = [(True, r, min(KEY_BLOCK, first - r), r) for r in range(0, first, KEY_BLOCK)]
    blocks += [(False, r, min(KEY_BLOCK, c_len - r), first + r) for r in range(0, c_len, KEY_BLOCK)]

    def item_heads(item):
        return _chunk_heads(2 * item) + _chunk_heads(2 * item + 1)

    def query_operand(sb, item):
        cols = []
        for c in (2 * item, 2 * item + 1):
            qt = qt_ref[0, c * LANES:(c + 1) * LANES, sb * sub:(sb + 1) * sub]
            zero = jnp.zeros_like(qt)
            cols += [jnp.where(slot_a, qt, zero), jnp.where(slot_a, zero, qt)]
        return jnp.concatenate(cols, axis=1)

    def score_block(slot, item, geometry, rhs, blk, m):
        latent, r0, rows, b0 = blk
        psl = slice((2 * item) // GQA_GROUP * LANES, ((2 * item) // GQA_GROUP + 1) * LANES)
        if not latent:
            s = _dot(kc_ref[0, r0:r0 + rows, psl], rhs)
        elif mode == "dense":
            s = _dot(k_ref[0, r0:r0 + rows, psl], rhs)
        else:
            start, bias = geometry
            s = _dot(k_ref[0, pl.ds(pl.multiple_of(start + r0, LANES), rows), psl], rhs) + bias[r0:r0 + rows]
        s_ref[slot, b0:b0 + rows, :] = s
        top = s.max(axis=0, keepdims=True)
        return top if m is None else jnp.maximum(m, top)

    def exp_block(slot, blk, m):
        _, _, rows, b0 = blk
        p_ref[slot, b0:b0 + rows, :] = jnp.exp2(s_ref[slot, b0:b0 + rows, :] - m).astype(BF16)

    def value_block(slot, item, geometry, blk, acc):
        latent, r0, rows, b0 = blk
        pair = (2 * item) // GQA_GROUP
        vsl = slice(pair * VT_CHUNK, (pair + 1) * VT_CHUNK)
        if not latent:
            v_t = vct_ref[0, vsl, r0:r0 + rows]
        elif mode == "dense":
            v_t = vt_ref[0, vsl, r0:r0 + rows]
        else:
            v_t = vt_ref[0, vsl, pl.ds(pl.multiple_of(geometry[0] + r0, LANES), rows)]
        pv = _dot(v_t, p_ref[slot, b0:b0 + rows, :])
        return pv if acc is None else acc + pv

    def with_sink(item, m):
        if not use_sink:
            return m, None
        sink = jnp.concatenate([jnp.full((1, sub), sink_ref[h], F32) for h in item_heads(item)], axis=1) * LOG2E
        m = jnp.maximum(m, sink)
        return m, jnp.exp2(sink - m)

    def store_output(sb, item, o4, sink_term):
        denom = o4[LANES:LANES + 1, :]
        if use_sink:
            denom = denom + sink_term
        o4 = o4[:LANES, :] * (1.0 / denom)
        for j, h in enumerate(item_heads(item)):
            rows = slice(0, HEAD_DIM) if j % 2 == 0 else slice(HEAD_DIM, 2 * HEAD_DIM)
            o_ref[0, h * HEAD_DIM:(h + 1) * HEAD_DIM, sb * sub:(sb + 1) * sub] = (
                o4[rows, j * sub:(j + 1) * sub].astype(BF16))

    items = [(sb, it) for sb in range(n_sub) for it in range(2 * n_pairs)]
    geometries = [block_geometry(sb) if mode == "window" else None for sb in range(n_sub)]
    n_items = len(items)
    m_cur = sink_cur = sink_prev = None
    for i in range(-1, n_items + 1):
        nxt, prv = i + 1, i - 1
        if nxt < n_items:
            rhs = query_operand(*items[nxt])
        m_next = o_acc = None
        for blk in blocks:
            if nxt < n_items:
                m_next = score_block(nxt % 2, items[nxt][1], geometries[items[nxt][0]], rhs, blk, m_next)
            if 0 <= i < n_items:
                exp_block(i % 2, blk, m_cur)
            if 0 <= prv:
                o_acc = value_block(prv % 2, items[prv][1], geometries[items[prv][0]], blk, o_acc)
        if 0 <= prv:
            store_output(*items[prv], o_acc, sink_prev)
        sink_prev = sink_cur
        if nxt < n_items:
            m_cur, sink_cur = with_sink(items[nxt][1], m_next)


def _attention(qt, kv, kv_ctx, sink, *, mode, tq):
    b, wq, sq = qt.shape
    n_pairs = wq // (2 * GQA_GROUP * HEAD_DIM)
    kc, vct = kv_ctx
    c_len = kc.shape[1]
    use_sink = sink is not None
    in_specs = [pl.BlockSpec((1, wq, tq), lambda i, t: (i, 0, t))]
    args = [qt]
    seq = sq
    if mode != "ctx":
        k, vt = kv
        seq = k.shape[1]
        in_specs += [pl.BlockSpec((1, seq, k.shape[2]), lambda i, t: (i, 0, 0)),
                     pl.BlockSpec((1, vt.shape[1], seq), lambda i, t: (i, 0, 0))]
        args += [k, vt]
    in_specs += [pl.BlockSpec((1, c_len, kc.shape[2]), lambda i, t: (i, 0, 0)),
                 pl.BlockSpec((1, vct.shape[1], c_len), lambda i, t: (i, 0, 0))]
    args += [kc, vct]
    if use_sink:
        in_specs.append(pl.BlockSpec(memory_space=pltpu.SMEM))
        args.append(sink)
    n_keys = c_len + {"dense": seq, "window": QUERY_SUB + 2 * WINDOW, "ctx": 0}[mode]
    return pl.pallas_call(
        functools.partial(_attn_kernel, mode=mode, n_pairs=n_pairs, n_sub=tq // QUERY_SUB, tq=tq, seq=seq,
                          use_sink=use_sink),
        grid=(b, sq // tq),
        in_specs=in_specs,
        out_specs=pl.BlockSpec((1, wq, tq), lambda i, t: (i, 0, t)),
        out_shape=jax.ShapeDtypeStruct((b, wq, sq), BF16),
        scratch_shapes=[pltpu.VMEM((2, n_keys, 4 * QUERY_SUB), F32),
                        pltpu.VMEM((2, n_keys, 4 * QUERY_SUB), BF16)],
        compiler_params=_params(),
        name="attn_" + mode,
    )(*args)


def _post_kernel(*refs, pool, tm, seq):
    x_ref, at_ref, mod_ref, gains_ref, wo_ref, wfi_ref, wfo_ref = refs[:7]
    pos = 7
    if pool:
        u_ref, uprev_ref, unext_ref, wpool_ref, pscale_ref = refs[pos:pos + 5]
        pos += 5
    o_ref = refs[pos]
    hid_ref = refs[pos + 1]
    if pool:
        ext_ref = refs[pos + 2]

    n_a = at_ref.shape[1]
    if pool:
        t = pl.program_id(1)
        last = pl.num_programs(1) - 1
        halo = jnp.zeros((POOL_HALO, ext_ref.shape[1]), F32)
        ext_ref[0:POOL_HALO, :] = jnp.where(t > 0, uprev_ref[0], halo)
        ext_ref[POOL_HALO:POOL_HALO + tm, :] = u_ref[0]
        ext_ref[POOL_HALO + tm:, :] = jnp.where(t < last, unext_ref[0], halo)

    def pool_mix(r0, rows):
        n_ext = rows + 2 * POOL_HALO
        tok = t * tm + r0 + lax.broadcasted_iota(jnp.int32, (rows, 1), 0)

        def ahead(v, k):
            return pltpu.roll(v, n_ext - k, 0)

        def behind(v, k):
            return pltpu.roll(v, k, 0)

        mixed = []
        for g, w in enumerate(POOL_WINDOWS):
            sl = slice(g * LANES, (g + 1) * LANES)
            e = ext_ref[r0:r0 + n_ext, sl]
            pair = e + ahead(e, 1)
            if w == 2:
                total = behind(pair, 1)
            elif w == 4:
                total = pair + behind(pair, 2)
            else:
                quad = pair + ahead(pair, 2)
                if w == 8:
                    total = quad + behind(quad, 4)
                else:
                    octet = quad + ahead(quad, 4)
                    total = octet + behind(octet, 8)
            total = total[POOL_HALO:POOL_HALO + rows]
            cnt = jnp.minimum(tok + (w - w // 2), seq) - jnp.maximum(tok - w // 2, 0)
            diff = total / cnt.astype(F32) - e[POOL_HALO:POOL_HALO + rows]
            mixed.append((_dot(diff.astype(BF16), wpool_ref[g]) * pscale_ref[:, sl]).astype(BF16))
        return jnp.concatenate(mixed, axis=-1)

    def mixer_out(r0, rows):
        y = _dot_tn(at_ref[0, :, r0:r0 + rows], wo_ref[0:n_a, :])
        if pool:
            y = y + _dot(pool_mix(r0, rows), wo_ref[n_a:, :])
        return y

    def residual_and_norm(r0, rows, y):
        x1 = x_ref[0, r0:r0 + rows, :] + mod_ref[0, 2:3, :] * (_rms_normalise(y) * gains_ref[1:2, :])
        h = _rms_normalise(x1) * gains_ref[2:3, :]
        return x1, (h * (1.0 + mod_ref[0, 4:5, :]) + mod_ref[0, 3:4, :]).astype(BF16)

    def ffn_hidden(r0, rows, h):
        for i in range(FFN_HIDDEN // FFN_CHUNK):
            gate = _dot(h, wfi_ref[:, i * FFN_CHUNK:(i + 1) * FFN_CHUNK])
            up = _dot(h, wfi_ref[:, FFN_HIDDEN + i * FFN_CHUNK:FFN_HIDDEN + (i + 1) * FFN_CHUNK])
            hid_ref[r0:r0 + rows, i * FFN_CHUNK:(i + 1) * FFN_CHUNK] = (_silu(gate) * up).astype(BF16)

    def ffn_out(r0, rows, x1):
        z = _dot(hid_ref[r0:r0 + rows, :], wfo_ref[...])
        o_ref[0, r0:r0 + rows, :] = x1 + mod_ref[0, 5:6, :] * (_rms_normalise(z) * gains_ref[3:4, :])

    rows = tm // 2
    y_a = mixer_out(0, rows)
    y_b = mixer_out(rows, rows)
    x1_a, h_a = residual_and_norm(0, rows, y_a)
    ffn_hidden(0, rows, h_a)
    x1_b, h_b = residual_and_norm(rows, rows, y_b)
    ffn_out(0, rows, x1_a)
    ffn_hidden(rows, rows, h_b)
    ffn_out(rows, rows, x1_b)


def _post(x, a_t, mod, gains, wo, wfi, wfo, pool_args, *, tm):
    b, s, d = x.shape
    per_batch_mod = mod.shape[0] > 1
    pool = pool_args is not None
    in_specs = [
        pl.BlockSpec((1, tm, d), lambda i, t: (i, t, 0)),
        pl.BlockSpec((1, a_t.shape[1], tm), lambda i, t: (i, 0, t)),
        pl.BlockSpec((1, 8, d), (lambda i, t: (i, 0, 0)) if per_batch_mod else (lambda i, t: (0, 0, 0))),
        _const_spec(gains.shape), _const_spec(wo.shape), _const_spec(wfi.shape), _const_spec(wfo.shape),
    ]
    args = [x, a_t, mod, gains, wo, wfi, wfo]
    scratch = [pltpu.VMEM((tm, FFN_HIDDEN), BF16)]
    if pool:
        u, wpool, pscale = pool_args
        nu = u.shape[2]
        per_tile = tm // POOL_HALO
        n_halo_blocks = s // POOL_HALO
        in_specs += [
            pl.BlockSpec((1, tm, nu), lambda i, t: (i, t, 0)),
            pl.BlockSpec((1, POOL_HALO, nu), lambda i, t: (i, jnp.maximum(t * per_tile - 1, 0), 0)),
            pl.BlockSpec((1, POOL_HALO, nu),
                         lambda i, t: (i, jnp.minimum((t + 1) * per_tile, n_halo_blocks - 1), 0)),
            _const_spec(wpool.shape), _const_spec(pscale.shape),
        ]
        args += [u, u, u, wpool, pscale]
        scratch.append(pltpu.VMEM((tm + 2 * POOL_HALO, nu), F32))
    return pl.pallas_call(
        functools.partial(_post_kernel, pool=pool, tm=tm, seq=s),
        grid=(b, s // tm),
        in_specs=in_specs,
        out_specs=pl.BlockSpec((1, tm, d), lambda i, t: (i, t, 0)),
        out_shape=jax.ShapeDtypeStruct((b, s, d), F32),
        scratch_shapes=scratch,
        compiler_params=_params(),
        name="post_even" if pool else "post_odd",
    )(*args)


def _pad_rows(a, rows):
    return jnp.pad(a, [(0, 0)] * (a.ndim - 2) + [(0, rows - a.shape[-2]), (0, 0)])


def kernel(x, c, ctx, c_ctx, w_mod, b_mod, g_pre_mix, g_post_mix, g_pre_ffn, g_post_ffn, we_in, we_out,
           we_q_gain, we_k_gain, we_pool, we_pool_scale, wo_in, wo_out, wo_sink, w_ffn_in, w_ffn_out):
    batch, seq, d = x.shape
    c_len = ctx.shape[1]
    depth = w_mod.shape[0]
    tm = TOKEN_TILE
    tm_ctx = c_len

    cc = jnp.concatenate([c, c_ctx[None, :]], axis=0)
    cc = _pad_rows(cc, -(-(batch + 1) // 8) * 8)
    mods = _modulation(cc, w_mod, b_mod)

    cos, sin = _rope_tables(seq)
    _, lane_dim = _lane_slot_dim()
    head_mean = jnp.asarray(_head_mean_matrix(), BF16)

    for l in range(depth):
        even = l % 2 == 0
        i = l // 2
        with_ctx = l < depth - 1
        mod_x = _pad_rows(mods[l, :batch].reshape(batch, 6, d), 8)
        mod_c = _pad_rows(mods[l, batch:batch + 1].reshape(1, 6, d), 8)
        gains = _pad_rows(jnp.stack([g_pre_mix[l], g_post_mix[l], g_pre_ffn[l], g_post_ffn[l]]), 8)
        wfi = w_ffn_in[l].astype(BF16)
        wfo = w_ffn_out[l].astype(BF16)

        if even:
            n_heads, n_kv = 8, 2
            w_full, w_out = we_in[i], we_out[i].astype(BF16)
            norm_args = (head_mean, we_q_gain[i][lane_dim][None, :], we_k_gain[i][lane_dim][None, :])
            sink = None
            mode = "dense"
        else:
            n_heads, n_kv = 16, 4
            w_full, w_out = wo_in[i], wo_out[i].astype(BF16)
            norm_args = None
            sink = wo_sink[i]
            mode = "window"
        q_w, kv_w = n_heads * HEAD_DIM, n_kv * HEAD_DIM
        cols = np.concatenate([_q_cols(n_heads), _k_cols(n_kv, q_w), np.arange(q_w + kv_w, w_full.shape[1])])
        w_in = w_full[:, cols].astype(BF16)
        widths = (q_w, kv_w, kv_w, w_full.shape[1] - q_w - 2 * kv_w)

        lat = _pre(x, mod_x, gains, w_in, norm_args, (cos, sin), widths=widths, tm=tm)
        con = _pre(ctx, mod_c, gains, w_in, norm_args, None, widths=widths, tm=tm_ctx)
        a_t = _attention(lat[0], (lat[1], lat[2]), (con[1], con[2]), sink, mode=mode, tq=QUERY_TILE)
        pool_w = (we_pool[i].astype(BF16), we_pool_scale[i][None, :]) if even else None
        x = _post(x, a_t, mod_x, gains, w_out, wfi, wfo, (lat[3],) + pool_w if even else None, tm=tm)
        if with_ctx:
            ac_t = _attention(con[0], None, (con[1], con[2]), sink, mode="ctx", tq=c_len)
            ctx = _post(ctx, ac_t, mod_c, gains, w_out, wfi, wfo,
                        (con[3],) + pool_w if even else None, tm=tm_ctx)
    return x
```

```python
import functools
import math

import numpy as np
import jax
import jax.numpy as jnp
from jax import lax
from jax.experimental import pallas as pl
from jax.experimental.pallas import tpu as pltpu

D_MODEL = 1024
HEAD_DIM = 64
GQA_GROUP = 4
GRID_W = 64
ROPE_THETA = 10000.0
EPS = 1e-6
WINDOW = 128
POOL_WINDOWS = (2, 4, 8, 16)
POOL_HALO = 8
FFN_HIDDEN = 2816
FFN_CHUNK = 256
LANES = 128
LOG2E = math.log2(math.e)
Q_SCALE = LOG2E / math.sqrt(HEAD_DIM)
VMEM_LIMIT = 56 * 1024 * 1024
TOKEN_TILE = 512
POST_SUB_TILE = 256
PRE_TILE = 1024
PRE_SUB_TILE = 256
QUERY_TILE = 1024
QUERY_SUB = LANES
KEY_BLOCK = 256
VT_ONES_ROWS = 16
VT_CHUNK = LANES + VT_ONES_ROWS

F32 = jnp.float32
BF16 = jnp.bfloat16


def _lane_slot_dim():
    lane = np.arange(LANES)
    part = lane // 32
    i = lane % 32
    dim = np.where(i < 16, i, 32 + (i - 16)) + np.where(part >= 2, 16, 0)
    return part % 2, dim


def _chunk_heads(c):
    head_a = (c // GQA_GROUP) * 2 * GQA_GROUP + c % GQA_GROUP
    return head_a, head_a + GQA_GROUP


def _q_cols(n_heads):
    slot, dim = _lane_slot_dim()
    cols = []
    for c in range(n_heads // 2):
        head_a, head_b = _chunk_heads(c)
        cols.append(np.where(slot == 0, head_a, head_b) * HEAD_DIM + dim)
    return np.concatenate(cols)


def _k_cols(n_kv, base):
    slot, dim = _lane_slot_dim()
    return np.concatenate([base + (2 * m + slot) * HEAD_DIM + dim for m in range(n_kv // 2)])


def _head_mean_matrix():
    slot, _ = _lane_slot_dim()
    mean = (slot[:, None] == slot[None, :]).astype(np.float32) / HEAD_DIM
    return np.concatenate([mean, mean], axis=0)


def _rope_tables(seq):
    quarter = HEAD_DIM // 4
    freqs = ROPE_THETA ** (-jnp.arange(quarter, dtype=F32) / quarter)
    t = jnp.arange(seq, dtype=jnp.int32)
    rows = (t // GRID_W).astype(F32)[:, None] * freqs[None, :]
    cols = (t % GRID_W).astype(F32)[:, None] * freqs[None, :]
    ang = jnp.concatenate([rows, cols], axis=-1)
    cos = jnp.tile(jnp.cos(ang), (1, 4))
    sin = jnp.sin(ang)
    sin = jnp.concatenate([-sin, -sin, sin, sin], axis=-1)
    return cos, sin


def _rms_normalise(x):
    return x * lax.rsqrt(jnp.mean(x * x, axis=-1, keepdims=True) + EPS)


def _silu(x):
    return x * (1.0 / (1.0 + jnp.exp(-x)))


def _dot(a, b):
    return jnp.dot(a, b, preferred_element_type=F32)


def _dot_tn(a_t, b):
    return lax.dot_general(a_t, b, (((0,), (0,)), ((), ())), preferred_element_type=F32)


def _const_spec(shape):
    return pl.BlockSpec(shape, lambda *_: (0,) * len(shape), pipeline_mode=pl.Buffered(1))


def _params():
    return pltpu.CompilerParams(dimension_semantics=("parallel", "parallel"), vmem_limit_bytes=VMEM_LIMIT)


def _mod_kernel(c_ref, w_ref, b_ref, o_ref):
    s = _silu(c_ref[...]).astype(BF16)
    o_ref[0] = _dot(s, w_ref[0].astype(BF16)) + b_ref[0]


def _modulation(cc, w_mod, b_mod):
    depth, d, n = w_mod.shape
    rows = cc.shape[0]
    tn = 1536
    return pl.pallas_call(
        _mod_kernel,
        grid=(depth, n // tn),
        in_specs=[
            pl.BlockSpec((rows, d), lambda l, j: (0, 0)),
            pl.BlockSpec((1, d, tn), lambda l, j: (l, 0, j)),
            pl.BlockSpec((1, 1, tn), lambda l, j: (l, 0, j)),
        ],
        out_specs=pl.BlockSpec((1, rows, tn), lambda l, j: (l, 0, j)),
        out_shape=jax.ShapeDtypeStruct((depth, rows, n), F32),
        compiler_params=_params(),
        name="modulation",
    )(cc, w_mod, b_mod.reshape(depth, 1, n))


def _pre_kernel(*refs, n_q, n_k, n_v, n_u, qk_norm, rope):
    x_ref, mod_ref, gains_ref, w_ref = refs[:4]
    pos = 4
    if qk_norm:
        pm_ref, qg_ref, kg_ref = refs[pos:pos + 3]
        pos += 3
    if rope:
        cos_ref, sin_ref = refs[pos:pos + 2]
        pos += 2
    qt_ref, k_ref, vt_ref = refs[pos:pos + 3]
    u_ref = refs[pos + 3] if n_u else None

    def modulated(rows):
        h = _rms_normalise(x_ref[0, rows, :]) * gains_ref[0:1, :]
        return (h * (1.0 + mod_ref[0, 1:2, :]) + mod_ref[0, 0:1, :]).astype(BF16)

    def head_chunk(c, gain_ref, rows):
        if qk_norm:
            c2 = c * c
            hi = c2.astype(BF16)
            lo = (c2 - hi.astype(F32)).astype(BF16)
            ms = _dot(jnp.concatenate([hi, lo], axis=1), pm_ref[...])
            c = c * lax.rsqrt(ms + EPS) * gain_ref[...]
        if rope:
            c = c * cos_ref[rows, :] + pltpu.roll(c, LANES // 2, 1) * sin_ref[rows, :]
        return c

    def project(rows, h):
        p = _dot(h, w_ref[...])
        for j in range(n_q // LANES):
            sl = slice(j * LANES, (j + 1) * LANES)
            c = head_chunk(p[:, sl], qg_ref if qk_norm else None, rows) * Q_SCALE
            qt_ref[0, sl, rows] = c.T.astype(BF16)
        for j in range(n_k // LANES):
            sl = slice(j * LANES, (j + 1) * LANES)
            c = head_chunk(p[:, n_q + j * LANES:n_q + (j + 1) * LANES], kg_ref if qk_norm else None, rows)
            k_ref[0, rows, sl] = c.astype(BF16)
        ones = jnp.ones((VT_ONES_ROWS, p.shape[0]), BF16)
        for j in range(n_v // LANES):
            r0 = j * VT_CHUNK
            vt_ref[0, r0:r0 + LANES, rows] = (
                p[:, n_q + n_k + j * LANES:n_q + n_k + (j + 1) * LANES].T.astype(BF16))
            vt_ref[0, r0 + LANES:r0 + VT_CHUNK, rows] = ones
        if n_u:
            u_ref[0, rows, :] = p[:, n_q + n_k + n_v:]

    tm = x_ref.shape[1]
    step = min(tm, PRE_SUB_TILE)
    parts = [slice(r, r + step) for r in range(0, tm, step)]
    h_next = modulated(parts[0])
    for i, rows in enumerate(parts):
        h_cur = h_next
        if i + 1 < len(parts):
            h_next = modulated(parts[i + 1])
        project(rows, h_cur)


def _pre(x, mod, gains, w, norm_args, rope_args, *, widths, tm):
    n_q, n_k, n_v, n_u = widths
    b, s, d = x.shape
    per_batch_mod = mod.shape[0] > 1
    qk_norm = norm_args is not None
    rope = rope_args is not None
    in_specs = [
        pl.BlockSpec((1, tm, d), lambda i, t: (i, t, 0)),
        pl.BlockSpec((1, 8, d), (lambda i, t: (i, 0, 0)) if per_batch_mod else (lambda i, t: (0, 0, 0))),
        _const_spec(gains.shape),
        _const_spec(w.shape),
    ]
    args = [x, mod, gains, w]
    if qk_norm:
        in_specs += [_const_spec(a.shape) for a in norm_args]
        args += list(norm_args)
    if rope:
        in_specs += [pl.BlockSpec((tm, LANES), lambda i, t: (t, 0))] * 2
        args += list(rope_args)
    vt_rows = n_v // LANES * VT_CHUNK
    out_shape = [jax.ShapeDtypeStruct((b, n_q, s), BF16), jax.ShapeDtypeStruct((b, s, n_k), BF16),
                 jax.ShapeDtypeStruct((b, vt_rows, s), BF16)]
    out_specs = [pl.BlockSpec((1, n_q, tm), lambda i, t: (i, 0, t)),
                 pl.BlockSpec((1, tm, n_k), lambda i, t: (i, t, 0)),
                 pl.BlockSpec((1, vt_rows, tm), lambda i, t: (i, 0, t))]
    if n_u:
        out_shape.append(jax.ShapeDtypeStruct((b, s, n_u), F32))
        out_specs.append(pl.BlockSpec((1, tm, n_u), lambda i, t: (i, t, 0)))
    return pl.pallas_call(
        functools.partial(_pre_kernel, n_q=n_q, n_k=n_k, n_v=n_v, n_u=n_u, qk_norm=qk_norm, rope=rope),
        grid=(b, s // tm),
        in_specs=in_specs,
        out_specs=out_specs,
        out_shape=out_shape,
        compiler_params=_params(),
        name="pre_even" if qk_norm else "pre_odd",
    )(*args)


def _attn_kernel(*refs, mode, n_pairs, n_sub, tq, seq, use_sink):
    qt_ref = refs[0]
    pos = 1
    if mode != "ctx":
        k_ref, vt_ref = refs[pos:pos + 2]
        pos += 2
    kc_ref, vct_ref = refs[pos:pos + 2]
    pos += 2
    if use_sink:
        sink_ref = refs[pos]
        pos += 1
    o_ref, s_ref, p_ref = refs[pos:pos + 3]

    sub = QUERY_SUB
    row = lax.broadcasted_iota(jnp.int32, (LANES, 1), 0)
    slot_a = (row // 32) % 2 == 0
    band = sub + 2 * WINDOW
    if mode == "window":
        rel = (lax.broadcasted_iota(jnp.int32, (band, sub), 0)
               - lax.broadcasted_iota(jnp.int32, (band, sub), 1))

    def block_geometry(sb):
        q0 = pl.program_id(1) * tq + sb * sub
        start = pl.multiple_of(jnp.clip(q0 - WINDOW, 0, seq - band), LANES)
        bias = jnp.where(jnp.abs(rel + (start - q0)) <= WINDOW, 0.0, -jnp.inf).astype(F32)
        return start, jnp.concatenate([bias] * 4, axis=1)

    c_len = kc_ref.shape[1]
    first = {"dense": seq, "window": band, "ctx": 0}[mode]
    blocks = [(True, r, min(KEY_BLOCK, first - r), r) for r in range(0, first, KEY_BLOCK)]
    blocks += [(False, r, min(KEY_BLOCK, c_len - r), first + r) for r in range(0, c_len, KEY_BLOCK)]

    def item_heads(item):
        return _chunk_heads(2 * item) + _chunk_heads(2 * item + 1)

    def query_operand(sb, item):
        cols = []
        for c in (2 * item, 2 * item + 1):
            qt = qt_ref[0, c * LANES:(c + 1) * LANES, sb * sub:(sb + 1) * sub]
            zero = jnp.zeros_like(qt)
            cols += [jnp.where(slot_a, qt, zero), jnp.where(slot_a, zero, qt)]
        return jnp.concatenate(cols, axis=1)

    def score_block(slot, item, geometry, rhs, blk, m):
        latent, r0, rows, b0 = blk
        psl = slice((2 * item) // GQA_GROUP * LANES, ((2 * item) // GQA_GROUP + 1) * LANES)
        if not latent:
            s = _dot(kc_ref[0, r0:r0 + rows, psl], rhs)
        elif mode == "dense":
            s = _dot(k_ref[0, r0:r0 + rows, psl], rhs)
        else:
            start, bias = geometry
            s = _dot(k_ref[0, pl.ds(pl.multiple_of(start + r0, LANES), rows), psl], rhs) + bias[r0:r0 + rows]
        s_ref[slot, b0:b0 + rows, :] = s
        top = s.max(axis=0, keepdims=True)
        return top if m is None else jnp.maximum(m, top)

    def exp_block(slot, blk, m):
        _, _, rows, b0 = blk
        p_ref[slot, b0:b0 + rows, :] = jnp.exp2(s_ref[slot, b0:b0 + rows, :] - m).astype(BF16)

    def value_block(slot, item, geometry, blk, acc):
        latent, r0, rows, b0 = blk
        pair = (2 * item) // GQA_GROUP
        vsl = slice(pair * VT_CHUNK, (pair + 1) * VT_CHUNK)
        if not latent:
            v_t = vct_ref[0, vsl, r0:r0 + rows]
        elif mode == "dense":
            v_t = vt_ref[0, vsl, r0:r0 + rows]
        else:
            v_t = vt_ref[0, vsl, pl.ds(pl.multiple_of(geometry[0] + r0, LANES), rows)]
        pv = _dot(v_t, p_ref[slot, b0:b0 + rows, :])
        return pv if acc is None else acc + pv

    def with_sink(item, m):
        if not use_sink:
            return m, None
        sink = jnp.concatenate([jnp.full((1, sub), sink_ref[h], F32) for h in item_heads(item)], axis=1) * LOG2E
        m = jnp.maximum(m, sink)
        return m, jnp.exp2(sink - m)

    def store_output(sb, item, o4, sink_term):
        denom = o4[LANES:LANES + 1, :]
        if use_sink:
            denom = denom + sink_term
        o4 = o4[:LANES, :] * (1.0 / denom)
        for j, h in enumerate(item_heads(item)):
            rows = slice(0, HEAD_DIM) if j % 2 == 0 else slice(HEAD_DIM, 2 * HEAD_DIM)
            o_ref[0, h * HEAD_DIM:(h + 1) * HEAD_DIM, sb * sub:(sb + 1) * sub] = (
                o4[rows, j * sub:(j + 1) * sub].astype(BF16))

    items = [(sb, it) for sb in range(n_sub) for it in range(2 * n_pairs)]
    geometries = [block_geometry(sb) if mode == "window" else None for sb in range(n_sub)]
    n_items = len(items)
    m_cur = sink_cur = sink_prev = None
    for i in range(-1, n_items + 1):
        nxt, prv = i + 1, i - 1
        if nxt < n_items:
            rhs = query_operand(*items[nxt])
        m_next = o_acc = None
        for blk in blocks:
            if nxt < n_items:
                m_next = score_block(nxt % 2, items[nxt][1], geometries[items[nxt][0]], rhs, blk, m_next)
            if 0 <= i < n_items:
                exp_block(i % 2, blk, m_cur)
            if 0 <= prv:
                o_acc = value_block(prv % 2, items[prv][1], geometries[items[prv][0]], blk, o_acc)
        if 0 <= prv:
            store_output(*items[prv], o_acc, sink_prev)
        sink_prev = sink_cur
        if nxt < n_items:
            m_cur, sink_cur = with_sink(items[nxt][1], m_next)


def _attention(qt, kv, kv_ctx, sink, *, mode, tq):
    b, wq, sq = qt.shape
    n_pairs = wq // (2 * GQA_GROUP * HEAD_DIM)
    kc, vct = kv_ctx
    c_len = kc.shape[1]
    use_sink = sink is not None
    in_specs = [pl.BlockSpec((1, wq, tq), lambda i, t: (i, 0, t))]
    args = [qt]
    seq = sq
    if mode != "ctx":
        k, vt = kv
        seq = k.shape[1]
        in_specs += [pl.BlockSpec((1, seq, k.shape[2]), lambda i, t: (i, 0, 0)),
                     pl.BlockSpec((1, vt.shape[1], seq), lambda i, t: (i, 0, 0))]
        args += [k, vt]
    in_specs += [pl.BlockSpec((1, c_len, kc.shape[2]), lambda i, t: (i, 0, 0)),
                 pl.BlockSpec((1, vct.shape[1], c_len), lambda i, t: (i, 0, 0))]
    args += [kc, vct]
    if use_sink:
        in_specs.append(pl.BlockSpec(memory_space=pltpu.SMEM))
        args.append(sink)
    n_keys = c_len + {"dense": seq, "window": QUERY_SUB + 2 * WINDOW, "ctx": 0}[mode]
    return pl.pallas_call(
        functools.partial(_attn_kernel, mode=mode, n_pairs=n_pairs, n_sub=tq // QUERY_SUB, tq=tq, seq=seq,
                          use_sink=use_sink),
        grid=(b, sq // tq),
        in_specs=in_specs,
        out_specs=pl.BlockSpec((1, wq, tq), lambda i, t: (i, 0, t)),
        out_shape=jax.ShapeDtypeStruct((b, wq, sq), BF16),
        scratch_shapes=[pltpu.VMEM((2, n_keys, 4 * QUERY_SUB), F32),
                        pltpu.VMEM((2, n_keys, 4 * QUERY_SUB), BF16)],
        compiler_params=_params(),
        name="attn_" + mode,
    )(*args)


def _post_kernel(*refs, pool, tm, seq):
    x_ref, at_ref, mod_ref, gains_ref, wo_ref, wfi_ref, wfo_ref = refs[:7]
    pos = 7
    if pool:
        u_ref, uprev_ref, unext_ref, wpool_ref, pscale_ref = refs[pos:pos + 5]
        pos += 5
    o_ref = refs[pos]
    hid_ref = refs[pos + 1]
    if pool:
        ext_ref = refs[pos + 2]

    n_a = at_ref.shape[1]
    if pool:
        t = pl.program_id(1)
        last = pl.num_programs(1) - 1
        halo = jnp.zeros((POOL_HALO, ext_ref.shape[1]), F32)
        ext_ref[0:POOL_HALO, :] = jnp.where(t > 0, uprev_ref[0], halo)
        ext_ref[POOL_HALO:POOL_HALO + tm, :] = u_ref[0]
        ext_ref[POOL_HALO + tm:, :] = jnp.where(t < last, unext_ref[0], halo)

    def pool_mix(r0, rows):
        n_ext = rows + 2 * POOL_HALO
        tok = t * tm + r0 + lax.broadcasted_iota(jnp.int32, (rows, 1), 0)

        def ahead(v, k):
            return pltpu.roll(v, n_ext - k, 0)

        def behind(v, k):
            return pltpu.roll(v, k, 0)

        mixed = []
        for g, w in enumerate(POOL_WINDOWS):
            sl = slice(g * LANES, (g + 1) * LANES)
            e = ext_ref[r0:r0 + n_ext, sl]
            pair = e + ahead(e, 1)
            if w == 2:
                total = behind(pair, 1)
            elif w == 4:
                total = pair + behind(pair, 2)
            else:
                quad = pair + ahead(pair, 2)
                if w == 8:
                    total = quad + behind(quad, 4)
                else:
                    octet = quad + ahead(quad, 4)
                    total = octet + behind(octet, 8)
            total = total[POOL_HALO:POOL_HALO + rows]
            cnt = jnp.minimum(tok + (w - w // 2), seq) - jnp.maximum(tok - w // 2, 0)
            diff = total / cnt.astype(F32) - e[POOL_HALO:POOL_HALO + rows]
            mixed.append((_dot(diff.astype(BF16), wpool_ref[g]) * pscale_ref[:, sl]).astype(BF16))
        return jnp.concatenate(mixed, axis=-1)

    def attn_out(r0, rows):
        return _dot_tn(at_ref[0, :, r0:r0 + rows], wo_ref[0:n_a, :])

    def pool_out(r0, rows, y):
        return y + _dot(pool_mix(r0, rows), wo_ref[n_a:, :]) if pool else y

    def residual_and_norm(r0, rows, y):
        x1 = x_ref[0, r0:r0 + rows, :] + mod_ref[0, 2:3, :] * (_rms_normalise(y) * gains_ref[1:2, :])
        h = _rms_normalise(x1) * gains_ref[2:3, :]
        return x1, (h * (1.0 + mod_ref[0, 4:5, :]) + mod_ref[0, 3:4, :]).astype(BF16)

    def ffn_hidden(r0, rows, h):
        for i in range(FFN_HIDDEN // FFN_CHUNK):
            gate = _dot(h, wfi_ref[:, i * FFN_CHUNK:(i + 1) * FFN_CHUNK])
            up = _dot(h, wfi_ref[:, FFN_HIDDEN + i * FFN_CHUNK:FFN_HIDDEN + (i + 1) * FFN_CHUNK])
            hid_ref[r0:r0 + rows, i * FFN_CHUNK:(i + 1) * FFN_CHUNK] = (_silu(gate) * up).astype(BF16)

    def ffn_out(r0, rows, x1):
        z = _dot(hid_ref[r0:r0 + rows, :], wfo_ref[...])
        o_ref[0, r0:r0 + rows, :] = x1 + mod_ref[0, 5:6, :] * (_rms_normalise(z) * gains_ref[3:4, :])

    rows = min(tm // 2, POST_SUB_TILE)
    starts = list(range(0, tm, rows))
    n = len(starts)
    y = {p: pool_out(starts[p], rows, attn_out(starts[p], rows)) for p in range(min(2, n))}
    normed = {0: residual_and_norm(starts[0], rows, y.pop(0))}
    for p in range(n):
        x1, h = normed.pop(p)
        ffn_hidden(starts[p], rows, h)
        if p + 2 < n:
            y[p + 2] = pool_out(starts[p + 2], rows, attn_out(starts[p + 2], rows))
        if p + 1 < n:
            normed[p + 1] = residual_and_norm(starts[p + 1], rows, y.pop(p + 1))
        ffn_out(starts[p], rows, x1)


def _post(x, a_t, mod, gains, wo, wfi, wfo, pool_args, *, tm):
    b, s, d = x.shape
    per_batch_mod = mod.shape[0] > 1
    pool = pool_args is not None
    in_specs = [
        pl.BlockSpec((1, tm, d), lambda i, t: (i, t, 0)),
        pl.BlockSpec((1, a_t.shape[1], tm), lambda i, t: (i, 0, t)),
        pl.BlockSpec((1, 8, d), (lambda i, t: (i, 0, 0)) if per_batch_mod else (lambda i, t: (0, 0, 0))),
        _const_spec(gains.shape), _const_spec(wo.shape), _const_spec(wfi.shape), _const_spec(wfo.shape),
    ]
    args = [x, a_t, mod, gains, wo, wfi, wfo]
    scratch = [pltpu.VMEM((tm, FFN_HIDDEN), BF16)]
    if pool:
        u, wpool, pscale = pool_args
        nu = u.shape[2]
        per_tile = tm // POOL_HALO
        n_halo_blocks = s // POOL_HALO
        in_specs += [
            pl.BlockSpec((1, tm, nu), lambda i, t: (i, t, 0)),
            pl.BlockSpec((1, POOL_HALO, nu), lambda i, t: (i, jnp.maximum(t * per_tile - 1, 0), 0)),
            pl.BlockSpec((1, POOL_HALO, nu),
                         lambda i, t: (i, jnp.minimum((t + 1) * per_tile, n_halo_blocks - 1), 0)),
            _const_spec(wpool.shape), _const_spec(pscale.shape),
        ]
        args += [u, u, u, wpool, pscale]
        scratch.append(pltpu.VMEM((tm + 2 * POOL_HALO, nu), F32))
    return pl.pallas_call(
        functools.partial(_post_kernel, pool=pool, tm=tm, seq=s),
        grid=(b, s // tm),
        in_specs=in_specs,
        out_specs=pl.BlockSpec((1, tm, d), lambda i, t: (i, t, 0)),
        out_shape=jax.ShapeDtypeStruct((b, s, d), F32),
        scratch_shapes=scratch,
        compiler_params=_params(),
        name="post_even" if pool else "post_odd",
    )(*args)


def _pad_rows(a, rows):
    return jnp.pad(a, [(0, 0)] * (a.ndim - 2) + [(0, rows - a.shape[-2]), (0, 0)])


def kernel(x, c, ctx, c_ctx, w_mod, b_mod, g_pre_mix, g_post_mix, g_pre_ffn, g_post_ffn, we_in, we_out,
           we_q_gain, we_k_gain, we_pool, we_pool_scale, wo_in, wo_out, wo_sink, w_ffn_in, w_ffn_out):
    batch, seq, d = x.shape
    c_len = ctx.shape[1]
    depth = w_mod.shape[0]
    tm = TOKEN_TILE
    tm_ctx = c_len

    cc = jnp.concatenate([c, c_ctx[None, :]], axis=0)
    cc = _pad_rows(cc, -(-(batch + 1) // 8) * 8)
    mods = _modulation(cc, w_mod, b_mod)

    cos, sin = _rope_tables(seq)
    _, lane_dim = _lane_slot_dim()
    head_mean = jnp.asarray(_head_mean_matrix(), BF16)

    for l in range(depth):
        even = l % 2 == 0
        i = l // 2
        with_ctx = l < depth - 1
        mod_x = _pad_rows(mods[l, :batch].reshape(batch, 6, d), 8)
        mod_c = _pad_rows(mods[l, batch:batch + 1].reshape(1, 6, d), 8)
        gains = _pad_rows(jnp.stack([g_pre_mix[l], g_post_mix[l], g_pre_ffn[l], g_post_ffn[l]]), 8)
        wfi = w_ffn_in[l].astype(BF16)
        wfo = w_ffn_out[l].astype(BF16)

        if even:
            n_heads, n_kv = 8, 2
            w_full, w_out = we_in[i], we_out[i].astype(BF16)
            norm_args = (head_mean, we_q_gain[i][lane_dim][None, :], we_k_gain[i][lane_dim][None, :])
            sink = None
            mode = "dense"
        else:
            n_heads, n_kv = 16, 4
            w_full, w_out = wo_in[i], wo_out[i].astype(BF16)
            norm_args = None
            sink = wo_sink[i]
            mode = "window"
        q_w, kv_w = n_heads * HEAD_DIM, n_kv * HEAD_DIM
        cols = np.concatenate([_q_cols(n_heads), _k_cols(n_kv, q_w), np.arange(q_w + kv_w, w_full.shape[1])])
        w_in = w_full[:, cols].astype(BF16)
        widths = (q_w, kv_w, kv_w, w_full.shape[1] - q_w - 2 * kv_w)

        lat = _pre(x, mod_x, gains, w_in, norm_args, (cos, sin), widths=widths, tm=PRE_TILE)
        con = _pre(ctx, mod_c, gains, w_in, norm_args, None, widths=widths, tm=tm_ctx)
        a_t = _attention(lat[0], (lat[1], lat[2]), (con[1], con[2]), sink, mode=mode, tq=QUERY_TILE)
        pool_w = (we_pool[i].astype(BF16), we_pool_scale[i][None, :]) if even else None
        x = _post(x, a_t, mod_x, gains, w_out, wfi, wfo, (lat[3],) + pool_w if even else None, tm=tm)
        if with_ctx:
            ac_t = _attention(con[0], None, (con[1], con[2]), sink, mode="ctx", tq=c_len)
            ctx = _post(ctx, ac_t, mod_c, gains, w_out, wfi, wfo,
                        (con[3],) + pool_w if even else None, tm=tm_ctx)
    return x
```

```python
import functools
import math

import numpy as np
import jax
import jax.numpy as jnp
from jax import lax
from jax.experimental import pallas as pl
from jax.experimental.pallas import tpu as pltpu

D_MODEL = 1024
HEAD_DIM = 64
GQA_GROUP = 4
GRID_W = 64
ROPE_THETA = 10000.0
EPS = 1e-6
WINDOW = 128
POOL_WINDOWS = (2, 4, 8, 16)
POOL_HALO = 8
FFN_HIDDEN = 2816
FFN_CHUNK = 256
LANES = 128
SUBLANES = 8
ROPE_HALF = HEAD_DIM // 2
LOG2E = math.log2(math.e)
Q_SCALE = LOG2E / math.sqrt(HEAD_DIM)
VMEM_LIMIT = 56 * 1024 * 1024
MOD_COLS_TILE = 1536
TOKEN_TILE = 512
POST_SUB_TILE = 256
PRE_TILE = 1024
PRE_SUB_TILE = 256
QUERY_TILE = 1024
QUERY_SUB = LANES
KEY_BLOCK = 256
VT_ONES_ROWS = 16
VT_CHUNK = LANES + VT_ONES_ROWS

F32 = jnp.float32
BF16 = jnp.bfloat16


def _lane_slot_dim():
    quarter = HEAD_DIM // 4
    lane = np.arange(LANES)
    part = lane // ROPE_HALF
    i = lane % ROPE_HALF
    dim = np.where(i < quarter, i, ROPE_HALF + (i - quarter)) + np.where(part >= 2, quarter, 0)
    return part % 2, dim


def _chunk_heads(c):
    head_a = (c // GQA_GROUP) * 2 * GQA_GROUP + c % GQA_GROUP
    return head_a, head_a + GQA_GROUP


def _q_cols(n_heads):
    slot, dim = _lane_slot_dim()
    cols = []
    for c in range(n_heads // 2):
        head_a, head_b = _chunk_heads(c)
        cols.append(np.where(slot == 0, head_a, head_b) * HEAD_DIM + dim)
    return np.concatenate(cols)


def _k_cols(n_kv, base):
    slot, dim = _lane_slot_dim()
    return np.concatenate([base + (2 * m + slot) * HEAD_DIM + dim for m in range(n_kv // 2)])


def _head_mean_matrix():
    slot, _ = _lane_slot_dim()
    mean = (slot[:, None] == slot[None, :]).astype(np.float32) / HEAD_DIM
    return np.concatenate([mean, mean], axis=0)


def _rope_tables(seq):
    quarter = HEAD_DIM // 4
    freqs = ROPE_THETA ** (-jnp.arange(quarter, dtype=F32) / quarter)
    t = jnp.arange(seq, dtype=jnp.int32)
    rows = (t // GRID_W).astype(F32)[:, None] * freqs[None, :]
    cols = (t % GRID_W).astype(F32)[:, None] * freqs[None, :]
    ang = jnp.concatenate([rows, cols], axis=-1)
    cos = jnp.tile(jnp.cos(ang), (1, 4))
    sin = jnp.sin(ang)
    sin = jnp.concatenate([-sin, -sin, sin, sin], axis=-1)
    return cos, sin


def _rms_normalise(x):
    return x * lax.rsqrt(jnp.mean(x * x, axis=-1, keepdims=True) + EPS)


def _silu(x):
    return x * (1.0 / (1.0 + jnp.exp(-x)))


def _dot(a, b):
    return jnp.dot(a, b, preferred_element_type=F32)


def _dot_tn(a_t, b):
    return lax.dot_general(a_t, b, (((0,), (0,)), ((), ())), preferred_element_type=F32)


def _const_spec(shape):
    return pl.BlockSpec(shape, lambda *_: (0,) * len(shape), pipeline_mode=pl.Buffered(1))


def _params():
    return pltpu.CompilerParams(dimension_semantics=("parallel", "parallel"), vmem_limit_bytes=VMEM_LIMIT)


def _mod_kernel(c_ref, w_ref, b_ref, o_ref):
    s = _silu(c_ref[...]).astype(BF16)
    o_ref[0] = _dot(s, w_ref[0].astype(BF16)) + b_ref[0]


def _modulation(cc, w_mod, b_mod):
    depth, d, n = w_mod.shape
    rows = cc.shape[0]
    tn = MOD_COLS_TILE
    return pl.pallas_call(
        _mod_kernel,
        grid=(depth, n // tn),
        in_specs=[
            pl.BlockSpec((rows, d), lambda l, j: (0, 0)),
            pl.BlockSpec((1, d, tn), lambda l, j: (l, 0, j)),
            pl.BlockSpec((1, 1, tn), lambda l, j: (l, 0, j)),
        ],
        out_specs=pl.BlockSpec((1, rows, tn), lambda l, j: (l, 0, j)),
        out_shape=jax.ShapeDtypeStruct((depth, rows, n), F32),
        compiler_params=_params(),
        name="modulation",
    )(cc, w_mod, b_mod.reshape(depth, 1, n))


def _pre_kernel(*refs, n_q, n_k, n_v, n_u, qk_norm, rope):
    x_ref, mod_ref, gains_ref, w_ref = refs[:4]
    pos = 4
    if qk_norm:
        pm_ref, qg_ref, kg_ref = refs[pos:pos + 3]
        pos += 3
    if rope:
        cos_ref, sin_ref = refs[pos:pos + 2]
        pos += 2
    qt_ref, k_ref, vt_ref = refs[pos:pos + 3]
    u_ref = refs[pos + 3] if n_u else None

    def modulated(rows):
        h = _rms_normalise(x_ref[0, rows, :]) * gains_ref[0:1, :]
        return (h * (1.0 + mod_ref[0, 1:2, :]) + mod_ref[0, 0:1, :]).astype(BF16)

    def head_chunk(c, gain_ref, rows):
        if qk_norm:
            c2 = c * c
            hi = c2.astype(BF16)
            lo = (c2 - hi.astype(F32)).astype(BF16)
            ms = _dot(jnp.concatenate([hi, lo], axis=1), pm_ref[...])
            c = c * lax.rsqrt(ms + EPS) * gain_ref[...]
        if rope:
            c = c * cos_ref[rows, :] + pltpu.roll(c, LANES // 2, 1) * sin_ref[rows, :]
        return c

    def project(rows, h):
        p = _dot(h, w_ref[...])
        for j in range(n_q // LANES):
            sl = slice(j * LANES, (j + 1) * LANES)
            c = head_chunk(p[:, sl], qg_ref if qk_norm else None, rows) * Q_SCALE
            qt_ref[0, sl, rows] = c.T.astype(BF16)
        for j in range(n_k // LANES):
            sl = slice(j * LANES, (j + 1) * LANES)
            c = head_chunk(p[:, n_q + j * LANES:n_q + (j + 1) * LANES], kg_ref if qk_norm else None, rows)
            k_ref[0, rows, sl] = c.astype(BF16)
        ones = jnp.ones((VT_ONES_ROWS, p.shape[0]), BF16)
        for j in range(n_v // LANES):
            r0 = j * VT_CHUNK
            vt_ref[0, r0:r0 + LANES, rows] = (
                p[:, n_q + n_k + j * LANES:n_q + n_k + (j + 1) * LANES].T.astype(BF16))
            vt_ref[0, r0 + LANES:r0 + VT_CHUNK, rows] = ones
        if n_u:
            u_ref[0, rows, :] = p[:, n_q + n_k + n_v:]

    tm = x_ref.shape[1]
    step = min(tm, PRE_SUB_TILE)
    parts = [slice(r, r + step) for r in range(0, tm, step)]
    h_next = modulated(parts[0])
    for i, rows in enumerate(parts):
        h_cur = h_next
        if i + 1 < len(parts):
            h_next = modulated(parts[i + 1])
        project(rows, h_cur)


def _pre(x, mod, gains, w, norm_args, rope_args, *, widths, tm):
    n_q, n_k, n_v, n_u = widths
    b, s, d = x.shape
    per_batch_mod = mod.shape[0] > 1
    qk_norm = norm_args is not None
    rope = rope_args is not None
    in_specs = [
        pl.BlockSpec((1, tm, d), lambda i, t: (i, t, 0)),
        pl.BlockSpec((1, SUBLANES, d), (lambda i, t: (i, 0, 0)) if per_batch_mod else (lambda i, t: (0, 0, 0))),
        _const_spec(gains.shape),
        _const_spec(w.shape),
    ]
    args = [x, mod, gains, w]
    if qk_norm:
        in_specs += [_const_spec(a.shape) for a in norm_args]
        args += list(norm_args)
    if rope:
        in_specs += [pl.BlockSpec((tm, LANES), lambda i, t: (t, 0))] * 2
        args += list(rope_args)
    vt_rows = n_v // LANES * VT_CHUNK
    out_shape = [jax.ShapeDtypeStruct((b, n_q, s), BF16), jax.ShapeDtypeStruct((b, s, n_k), BF16),
                 jax.ShapeDtypeStruct((b, vt_rows, s), BF16)]
    out_specs = [pl.BlockSpec((1, n_q, tm), lambda i, t: (i, 0, t)),
                 pl.BlockSpec((1, tm, n_k), lambda i, t: (i, t, 0)),
                 pl.BlockSpec((1, vt_rows, tm), lambda i, t: (i, 0, t))]
    if n_u:
        out_shape.append(jax.ShapeDtypeStruct((b, s, n_u), F32))
        out_specs.append(pl.BlockSpec((1, tm, n_u), lambda i, t: (i, t, 0)))
    return pl.pallas_call(
        functools.partial(_pre_kernel, n_q=n_q, n_k=n_k, n_v=n_v, n_u=n_u, qk_norm=qk_norm, rope=rope),
        grid=(b, s // tm),
        in_specs=in_specs,
        out_specs=out_specs,
        out_shape=out_shape,
        compiler_params=_params(),
        name="pre_even" if qk_norm else "pre_odd",
    )(*args)


def _attn_kernel(*refs, mode, n_pairs, n_sub, tq, seq, use_sink):
    qt_ref = refs[0]
    pos = 1
    if mode != "ctx":
        k_ref, vt_ref = refs[pos:pos + 2]
        pos += 2
    kc_ref, vct_ref = refs[pos:pos + 2]
    pos += 2
    if use_sink:
        sink_ref = refs[pos]
        pos += 1
    o_ref, s_ref, p_ref = refs[pos:pos + 3]

    sub = QUERY_SUB
    row = lax.broadcasted_iota(jnp.int32, (LANES, 1), 0)
    slot_a = (row // ROPE_HALF) % 2 == 0
    band = sub + 2 * WINDOW
    if mode == "window":
        rel = (lax.broadcasted_iota(jnp.int32, (band, sub), 0)
               - lax.broadcasted_iota(jnp.int32, (band, sub), 1))

    def block_geometry(sb):
        q0 = pl.program_id(1) * tq + sb * sub
        start = pl.multiple_of(jnp.clip(q0 - WINDOW, 0, seq - band), LANES)
        bias = jnp.where(jnp.abs(rel + (start - q0)) <= WINDOW, 0.0, -jnp.inf).astype(F32)
        return start, jnp.concatenate([bias] * 4, axis=1)

    c_len = kc_ref.shape[1]
    first = {"dense": seq, "window": band, "ctx": 0}[mode]
    blocks = [(True, r, min(KEY_BLOCK, first - r), r) for r in range(0, first, KEY_BLOCK)]
    blocks += [(False, r, min(KEY_BLOCK, c_len - r), first + r) for r in range(0, c_len, KEY_BLOCK)]

    def item_heads(item):
        return _chunk_heads(2 * item) + _chunk_heads(2 * item + 1)

    def query_operand(sb, item):
        cols = []
        for c in (2 * item, 2 * item + 1):
            qt = qt_ref[0, c * LANES:(c + 1) * LANES, sb * sub:(sb + 1) * sub]
            zero = jnp.zeros_like(qt)
            cols += [jnp.where(slot_a, qt, zero), jnp.where(slot_a, zero, qt)]
        return jnp.concatenate(cols, axis=1)

    def score_block(slot, item, geometry, rhs, blk, m):
        latent, r0, rows, b0 = blk
        psl = slice((2 * item) // GQA_GROUP * LANES, ((2 * item) // GQA_GROUP + 1) * LANES)
        if not latent:
            s = _dot(kc_ref[0, r0:r0 + rows, psl], rhs)
        elif mode == "dense":
            s = _dot(k_ref[0, r0:r0 + rows, psl], rhs)
        else:
            start, bias = geometry
            s = _dot(k_ref[0, pl.ds(pl.multiple_of(start + r0, LANES), rows), psl], rhs) + bias[r0:r0 + rows]
        s_ref[slot, b0:b0 + rows, :] = s
        top = s.max(axis=0, keepdims=True)
        return top if m is None else jnp.maximum(m, top)

    def exp_block(slot, blk, m):
        _, _, rows, b0 = blk
        p_ref[slot, b0:b0 + rows, :] = jnp.exp2(s_ref[slot, b0:b0 + rows, :] - m).astype(BF16)

    def value_block(slot, item, geometry, blk, acc):
        latent, r0, rows, b0 = blk
        pair = (2 * item) // GQA_GROUP
        vsl = slice(pair * VT_CHUNK, (pair + 1) * VT_CHUNK)
        if not latent:
            v_t = vct_ref[0, vsl, r0:r0 + rows]
        elif mode == "dense":
            v_t = vt_ref[0, vsl, r0:r0 + rows]
        else:
            v_t = vt_ref[0, vsl, pl.ds(pl.multiple_of(geometry[0] + r0, LANES), rows)]
        pv = _dot(v_t, p_ref[slot, b0:b0 + rows, :])
        return pv if acc is None else acc + pv

    def with_sink(item, m):
        if not use_sink:
            return m, None
        sink = jnp.concatenate([jnp.full((1, sub), sink_ref[h], F32) for h in item_heads(item)], axis=1) * LOG2E
        m = jnp.maximum(m, sink)
        return m, jnp.exp2(sink - m)

    def store_output(sb, item, o4, sink_term):
        denom = o4[LANES:LANES + 1, :]
        if use_sink:
            denom = denom + sink_term
        o4 = o4[:LANES, :] * (1.0 / denom)
        for j, h in enumerate(item_heads(item)):
            rows = slice(0, HEAD_DIM) if j % 2 == 0 else slice(HEAD_DIM, 2 * HEAD_DIM)
            o_ref[0, h * HEAD_DIM:(h + 1) * HEAD_DIM, sb * sub:(sb + 1) * sub] = (
                o4[rows, j * sub:(j + 1) * sub].astype(BF16))

    items = [(sb, it) for sb in range(n_sub) for it in range(2 * n_pairs)]
    geometries = [block_geometry(sb) if mode == "window" else None for sb in range(n_sub)]
    n_items = len(items)
    m_cur = sink_cur = sink_prev = None
    for i in range(-1, n_items + 1):
        nxt, prv = i + 1, i - 1
        if nxt < n_items:
            rhs = query_operand(*items[nxt])
        m_next = o_acc = None
        for blk in blocks:
            if nxt < n_items:
                m_next = score_block(nxt % 2, items[nxt][1], geometries[items[nxt][0]], rhs, blk, m_next)
            if 0 <= i < n_items:
                exp_block(i % 2, blk, m_cur)
            if 0 <= prv:
                o_acc = value_block(prv % 2, items[prv][1], geometries[items[prv][0]], blk, o_acc)
        if 0 <= prv:
            store_output(*items[prv], o_acc, sink_prev)
        sink_prev = sink_cur
        if nxt < n_items:
            m_cur, sink_cur = with_sink(items[nxt][1], m_next)


def _attention(qt, kv, kv_ctx, sink, *, mode, tq):
    b, wq, sq = qt.shape
    n_pairs = wq // (2 * GQA_GROUP * HEAD_DIM)
    kc, vct = kv_ctx
    c_len = kc.shape[1]
    use_sink = sink is not None
    in_specs = [pl.BlockSpec((1, wq, tq), lambda i, t: (i, 0, t))]
    args = [qt]
    seq = sq
    if mode != "ctx":
        k, vt = kv
        seq = k.shape[1]
        in_specs += [pl.BlockSpec((1, seq, k.shape[2]), lambda i, t: (i, 0, 0)),
                     pl.BlockSpec((1, vt.shape[1], seq), lambda i, t: (i, 0, 0))]
        args += [k, vt]
    in_specs += [pl.BlockSpec((1, c_len, kc.shape[2]), lambda i, t: (i, 0, 0)),
                 pl.BlockSpec((1, vct.shape[1], c_len), lambda i, t: (i, 0, 0))]
    args += [kc, vct]
    if use_sink:
        in_specs.append(pl.BlockSpec(memory_space=pltpu.SMEM))
        args.append(sink)
    n_keys = c_len + {"dense": seq, "window": QUERY_SUB + 2 * WINDOW, "ctx": 0}[mode]
    return pl.pallas_call(
        functools.partial(_attn_kernel, mode=mode, n_pairs=n_pairs, n_sub=tq // QUERY_SUB, tq=tq, seq=seq,
                          use_sink=use_sink),
        grid=(b, sq // tq),
        in_specs=in_specs,
        out_specs=pl.BlockSpec((1, wq, tq), lambda i, t: (i, 0, t)),
        out_shape=jax.ShapeDtypeStruct((b, wq, sq), BF16),
        scratch_shapes=[pltpu.VMEM((2, n_keys, 4 * QUERY_SUB), F32),
                        pltpu.VMEM((2, n_keys, 4 * QUERY_SUB), BF16)],
        compiler_params=_params(),
        name="attn_" + mode,
    )(*args)


def _post_kernel(*refs, pool, tm, seq):
    x_ref, at_ref, mod_ref, gains_ref, wo_ref, wfi_ref, wfo_ref = refs[:7]
    pos = 7
    if pool:
        u_ref, uprev_ref, unext_ref, wpool_ref, pscale_ref = refs[pos:pos + 5]
        pos += 5
    o_ref = refs[pos]
    hid_ref = refs[pos + 1]
    if pool:
        ext_ref = refs[pos + 2]

    n_a = at_ref.shape[1]
    if pool:
        t = pl.program_id(1)
        last = pl.num_programs(1) - 1
        halo = jnp.zeros((POOL_HALO, ext_ref.shape[1]), F32)
        ext_ref[0:POOL_HALO, :] = jnp.where(t > 0, uprev_ref[0], halo)
        ext_ref[POOL_HALO:POOL_HALO + tm, :] = u_ref[0]
        ext_ref[POOL_HALO + tm:, :] = jnp.where(t < last, unext_ref[0], halo)

    def pool_mix(r0, rows):
        n_ext = rows + 2 * POOL_HALO
        tok = t * tm + r0 + lax.broadcasted_iota(jnp.int32, (rows, 1), 0)

        def ahead(v, k):
            return pltpu.roll(v, n_ext - k, 0)

        def behind(v, k):
            return pltpu.roll(v, k, 0)

        mixed = []
        for g, w in enumerate(POOL_WINDOWS):
            sl = slice(g * LANES, (g + 1) * LANES)
            e = ext_ref[r0:r0 + n_ext, sl]
            run, length = e, 1
            while length < w // 2:
                run, length = run + ahead(run, length), 2 * length
            total = run + behind(run, w // 2)
            total = total[POOL_HALO:POOL_HALO + rows]
            cnt = jnp.minimum(tok + (w - w // 2), seq) - jnp.maximum(tok - w // 2, 0)
            diff = total / cnt.astype(F32) - e[POOL_HALO:POOL_HALO + rows]
            mixed.append((_dot(diff.astype(BF16), wpool_ref[g]) * pscale_ref[:, sl]).astype(BF16))
        return jnp.concatenate(mixed, axis=-1)

    def attn_out(r0, rows):
        return _dot_tn(at_ref[0, :, r0:r0 + rows], wo_ref[0:n_a, :])

    def pool_out(r0, rows, y):
        return y + _dot(pool_mix(r0, rows), wo_ref[n_a:, :]) if pool else y

    def residual_and_norm(r0, rows, y):
        x1 = x_ref[0, r0:r0 + rows, :] + mod_ref[0, 2:3, :] * (_rms_normalise(y) * gains_ref[1:2, :])
        h = _rms_normalise(x1) * gains_ref[2:3, :]
        return x1, (h * (1.0 + mod_ref[0, 4:5, :]) + mod_ref[0, 3:4, :]).astype(BF16)

    def ffn_hidden(r0, rows, h):
        for i in range(FFN_HIDDEN // FFN_CHUNK):
            gate = _dot(h, wfi_ref[:, i * FFN_CHUNK:(i + 1) * FFN_CHUNK])
            up = _dot(h, wfi_ref[:, FFN_HIDDEN + i * FFN_CHUNK:FFN_HIDDEN + (i + 1) * FFN_CHUNK])
            hid_ref[r0:r0 + rows, i * FFN_CHUNK:(i + 1) * FFN_CHUNK] = (_silu(gate) * up).astype(BF16)

    def ffn_out(r0, rows, x1):
        z = _dot(hid_ref[r0:r0 + rows, :], wfo_ref[...])
        o_ref[0, r0:r0 + rows, :] = x1 + mod_ref[0, 5:6, :] * (_rms_normalise(z) * gains_ref[3:4, :])

    rows = min(tm // 2, POST_SUB_TILE)
    starts = list(range(0, tm, rows))
    n = len(starts)
    y = {p: pool_out(starts[p], rows, attn_out(starts[p], rows)) for p in range(min(2, n))}
    normed = {0: residual_and_norm(starts[0], rows, y.pop(0))}
    for p in range(n):
        x1, h = normed.pop(p)
        ffn_hidden(starts[p], rows, h)
        if p + 2 < n:
            y[p + 2] = pool_out(starts[p + 2], rows, attn_out(starts[p + 2], rows))
        if p + 1 < n:
            normed[p + 1] = residual_and_norm(starts[p + 1], rows, y.pop(p + 1))
        ffn_out(starts[p], rows, x1)


def _post(x, a_t, mod, gains, wo, wfi, wfo, pool_args, *, tm):
    b, s, d = x.shape
    per_batch_mod = mod.shape[0] > 1
    pool = pool_args is not None
    in_specs = [
        pl.BlockSpec((1, tm, d), lambda i, t: (i, t, 0)),
        pl.BlockSpec((1, a_t.shape[1], tm), lambda i, t: (i, 0, t)),
        pl.BlockSpec((1, SUBLANES, d), (lambda i, t: (i, 0, 0)) if per_batch_mod else (lambda i, t: (0, 0, 0))),
        _const_spec(gains.shape), _const_spec(wo.shape), _const_spec(wfi.shape), _const_spec(wfo.shape),
    ]
    args = [x, a_t, mod, gains, wo, wfi, wfo]
    scratch = [pltpu.VMEM((tm, FFN_HIDDEN), BF16)]
    if pool:
        u, wpool, pscale = pool_args
        nu = u.shape[2]
        per_tile = tm // POOL_HALO
        n_halo_blocks = s // POOL_HALO
        in_specs += [
            pl.BlockSpec((1, tm, nu), lambda i, t: (i, t, 0)),
            pl.BlockSpec((1, POOL_HALO, nu), lambda i, t: (i, jnp.maximum(t * per_tile - 1, 0), 0)),
            pl.BlockSpec((1, POOL_HALO, nu),
                         lambda i, t: (i, jnp.minimum((t + 1) * per_tile, n_halo_blocks - 1), 0)),
            _const_spec(wpool.shape), _const_spec(pscale.shape),
        ]
        args += [u, u, u, wpool, pscale]
        scratch.append(pltpu.VMEM((tm + 2 * POOL_HALO, nu), F32))
    return pl.pallas_call(
        functools.partial(_post_kernel, pool=pool, tm=tm, seq=s),
        grid=(b, s // tm),
        in_specs=in_specs,
        out_specs=pl.BlockSpec((1, tm, d), lambda i, t: (i, t, 0)),
        out_shape=jax.ShapeDtypeStruct((b, s, d), F32),
        scratch_shapes=scratch,
        compiler_params=_params(),
        name="post_even" if pool else "post_odd",
    )(*args)


def _pad_rows(a, rows):
    return jnp.pad(a, [(0, 0)] * (a.ndim - 2) + [(0, rows - a.shape[-2]), (0, 0)])


def kernel(x, c, ctx, c_ctx, w_mod, b_mod, g_pre_mix, g_post_mix, g_pre_ffn, g_post_ffn, we_in, we_out,
           we_q_gain, we_k_gain, we_pool, we_pool_scale, wo_in, wo_out, wo_sink, w_ffn_in, w_ffn_out):
    batch, seq, d = x.shape
    c_len = ctx.shape[1]
    depth = w_mod.shape[0]
    tm = TOKEN_TILE
    tm_ctx = c_len

    cc = jnp.concatenate([c, c_ctx[None, :]], axis=0)
    cc = _pad_rows(cc, -(-(batch + 1) // SUBLANES) * SUBLANES)
    mods = _modulation(cc, w_mod, b_mod)

    cos, sin = _rope_tables(seq)
    _, lane_dim = _lane_slot_dim()
    head_mean = jnp.asarray(_head_mean_matrix(), BF16)

    for l in range(depth):
        even = l % 2 == 0
        i = l // 2
        with_ctx = l < depth - 1
        mod_x = _pad_rows(mods[l, :batch].reshape(batch, 6, d), SUBLANES)
        mod_c = _pad_rows(mods[l, batch:batch + 1].reshape(1, 6, d), SUBLANES)
        gains = _pad_rows(jnp.stack([g_pre_mix[l], g_post_mix[l], g_pre_ffn[l], g_post_ffn[l]]), SUBLANES)
        wfi = w_ffn_in[l].astype(BF16)
        wfo = w_ffn_out[l].astype(BF16)

        if even:
            n_heads, n_kv = 8, 2
            w_full, w_out = we_in[i], we_out[i].astype(BF16)
            norm_args = (head_mean, we_q_gain[i][lane_dim][None, :], we_k_gain[i][lane_dim][None, :])
            sink = None
            mode = "dense"
        else:
            n_heads, n_kv = 16, 4
            w_full, w_out = wo_in[i], wo_out[i].astype(BF16)
            norm_args = None
            sink = wo_sink[i]
            mode = "window"
        q_w, kv_w = n_heads * HEAD_DIM, n_kv * HEAD_DIM
        cols = np.concatenate([_q_cols(n_heads), _k_cols(n_kv, q_w), np.arange(q_w + kv_w, w_full.shape[1])])
        w_in = w_full[:, cols].astype(BF16)
        widths = (q_w, kv_w, kv_w, w_full.shape[1] - q_w - 2 * kv_w)

        lat = _pre(x, mod_x, gains, w_in, norm_args, (cos, sin), widths=widths, tm=PRE_TILE)
        con = _pre(ctx, mod_c, gains, w_in, norm_args, None, widths=widths, tm=tm_ctx)
        a_t = _attention(lat[0], (lat[1], lat[2]), (con[1], con[2]), sink, mode=mode, tq=QUERY_TILE)
        pool_w = (we_pool[i].astype(BF16), we_pool_scale[i][None, :]) if even else None
        x = _post(x, a_t, mod_x, gains, w_out, wfi, wfo, (lat[3],) + pool_w if even else None, tm=tm)
        if with_ctx:
            ac_t = _attention(con[0], None, (con[1], con[2]), sink, mode="ctx", tq=c_len)
            ctx = _post(ctx, ac_t, mod_c, gains, w_out, wfi, wfo,
                        (con[3],) + pool_w if even else None, tm=tm_ctx)
    return x
```

```python
import functools
import math

import numpy as np
import jax
import jax.numpy as jnp
from jax import lax
from jax.experimental import pallas as pl
from jax.experimental.pallas import tpu as pltpu

D_MODEL = 1024
HEAD_DIM = 64
GQA_GROUP = 4
GRID_W = 64
ROPE_THETA = 10000.0
EPS = 1e-6
WINDOW = 128
POOL_WINDOWS = (2, 4, 8, 16)
POOL_HALO = 8
FFN_HIDDEN = 2816
FFN_CHUNK = 256
LANES = 128
SUBLANES = 8
ROPE_HALF = HEAD_DIM // 2
LOG2E = math.log2(math.e)
Q_SCALE = LOG2E / math.sqrt(HEAD_DIM)
VMEM_LIMIT = 56 * 1024 * 1024
MOD_COLS_TILE = 1536
TOKEN_TILE = 512
POST_SUB_TILE = 256
PRE_TILE = 1024
PRE_SUB_TILE = 256
QUERY_TILE = 1024
QUERY_SUB = LANES
ITEM_CHUNKS = 2
KEY_BLOCK = 256
VT_ONES_ROWS = 16
VT_CHUNK = LANES + VT_ONES_ROWS

F32 = jnp.float32
BF16 = jnp.bfloat16


def _lane_slot_dim():
    quarter = HEAD_DIM // 4
    lane = np.arange(LANES)
    part = lane // ROPE_HALF
    i = lane % ROPE_HALF
    dim = np.where(i < quarter, i, ROPE_HALF + (i - quarter)) + np.where(part >= 2, quarter, 0)
    return part % 2, dim


def _chunk_heads(c):
    head_a = (c // GQA_GROUP) * 2 * GQA_GROUP + c % GQA_GROUP
    return head_a, head_a + GQA_GROUP


def _q_cols(n_heads):
    slot, dim = _lane_slot_dim()
    cols = []
    for c in range(n_heads // 2):
        head_a, head_b = _chunk_heads(c)
        cols.append(np.where(slot == 0, head_a, head_b) * HEAD_DIM + dim)
    return np.concatenate(cols)


def _k_cols(n_kv, base):
    slot, dim = _lane_slot_dim()
    return np.concatenate([base + (2 * m + slot) * HEAD_DIM + dim for m in range(n_kv // 2)])


def _head_mean_matrix():
    slot, _ = _lane_slot_dim()
    mean = (slot[:, None] == slot[None, :]).astype(np.float32) / HEAD_DIM
    return np.concatenate([mean, mean], axis=0)


def _rope_tables(seq):
    quarter = HEAD_DIM // 4
    freqs = ROPE_THETA ** (-jnp.arange(quarter, dtype=F32) / quarter)
    t = jnp.arange(seq, dtype=jnp.int32)
    rows = (t // GRID_W).astype(F32)[:, None] * freqs[None, :]
    cols = (t % GRID_W).astype(F32)[:, None] * freqs[None, :]
    ang = jnp.concatenate([rows, cols], axis=-1)
    cos = jnp.tile(jnp.cos(ang), (1, 4))
    sin = jnp.sin(ang)
    sin = jnp.concatenate([-sin, -sin, sin, sin], axis=-1)
    return cos, sin


def _rms_normalise(x):
    return x * lax.rsqrt(jnp.mean(x * x, axis=-1, keepdims=True) + EPS)


def _silu(x):
    return x * (1.0 / (1.0 + jnp.exp(-x)))


def _dot(a, b):
    return jnp.dot(a, b, preferred_element_type=F32)


def _dot_tn(a_t, b):
    return lax.dot_general(a_t, b, (((0,), (0,)), ((), ())), preferred_element_type=F32)


def _const_spec(shape):
    return pl.BlockSpec(shape, lambda *_: (0,) * len(shape), pipeline_mode=pl.Buffered(1))


def _params():
    return pltpu.CompilerParams(dimension_semantics=("parallel", "parallel"), vmem_limit_bytes=VMEM_LIMIT)


def _mod_kernel(c_ref, w_ref, b_ref, o_ref):
    s = _silu(c_ref[...]).astype(BF16)
    o_ref[0] = _dot(s, w_ref[0].astype(BF16)) + b_ref[0]


def _modulation(cc, w_mod, b_mod):
    depth, d, n = w_mod.shape
    rows = cc.shape[0]
    tn = MOD_COLS_TILE
    return pl.pallas_call(
        _mod_kernel,
        grid=(depth, n // tn),
        in_specs=[
            pl.BlockSpec((rows, d), lambda l, j: (0, 0)),
            pl.BlockSpec((1, d, tn), lambda l, j: (l, 0, j)),
            pl.BlockSpec((1, 1, tn), lambda l, j: (l, 0, j)),
        ],
        out_specs=pl.BlockSpec((1, rows, tn), lambda l, j: (l, 0, j)),
        out_shape=jax.ShapeDtypeStruct((depth, rows, n), F32),
        compiler_params=_params(),
        name="modulation",
    )(cc, w_mod, b_mod.reshape(depth, 1, n))


def _pre_kernel(*refs, n_q, n_k, n_v, n_u, qk_norm, rope):
    x_ref, mod_ref, gains_ref, w_ref = refs[:4]
    pos = 4
    if qk_norm:
        pm_ref, qg_ref, kg_ref = refs[pos:pos + 3]
        pos += 3
    if rope:
        cos_ref, sin_ref = refs[pos:pos + 2]
        pos += 2
    qt_ref, k_ref, vt_ref = refs[pos:pos + 3]
    u_ref = refs[pos + 3] if n_u else None

    def modulated(rows):
        h = _rms_normalise(x_ref[0, rows, :]) * gains_ref[0:1, :]
        return (h * (1.0 + mod_ref[0, 1:2, :]) + mod_ref[0, 0:1, :]).astype(BF16)

    def head_chunk(c, gain_ref, rows):
        if qk_norm:
            c2 = c * c
            hi = c2.astype(BF16)
            lo = (c2 - hi.astype(F32)).astype(BF16)
            ms = _dot(jnp.concatenate([hi, lo], axis=1), pm_ref[...])
            c = c * lax.rsqrt(ms + EPS) * gain_ref[...]
        if rope:
            c = c * cos_ref[rows, :] + pltpu.roll(c, LANES // 2, 1) * sin_ref[rows, :]
        return c

    def project(rows, h):
        p = _dot(h, w_ref[...])
        for j in range(n_q // LANES):
            sl = slice(j * LANES, (j + 1) * LANES)
            c = head_chunk(p[:, sl], qg_ref if qk_norm else None, rows) * Q_SCALE
            qt_ref[0, sl, rows] = c.T.astype(BF16)
        for j in range(n_k // LANES):
            sl = slice(j * LANES, (j + 1) * LANES)
            c = head_chunk(p[:, n_q + j * LANES:n_q + (j + 1) * LANES], kg_ref if qk_norm else None, rows)
            k_ref[0, rows, sl] = c.astype(BF16)
        ones = jnp.ones((VT_ONES_ROWS, p.shape[0]), BF16)
        for j in range(n_v // LANES):
            r0 = j * VT_CHUNK
            vt_ref[0, r0:r0 + LANES, rows] = (
                p[:, n_q + n_k + j * LANES:n_q + n_k + (j + 1) * LANES].T.astype(BF16))
            vt_ref[0, r0 + LANES:r0 + VT_CHUNK, rows] = ones
        if n_u:
            u_ref[0, rows, :] = p[:, n_q + n_k + n_v:]

    tm = x_ref.shape[1]
    step = min(tm, PRE_SUB_TILE)
    parts = [slice(r, r + step) for r in range(0, tm, step)]
    h_next = modulated(parts[0])
    for i, rows in enumerate(parts):
        h_cur = h_next
        if i + 1 < len(parts):
            h_next = modulated(parts[i + 1])
        project(rows, h_cur)


def _pre(x, mod, gains, w, norm_args, rope_args, *, widths, tm):
    n_q, n_k, n_v, n_u = widths
    b, s, d = x.shape
    per_batch_mod = mod.shape[0] > 1
    qk_norm = norm_args is not None
    rope = rope_args is not None
    in_specs = [
        pl.BlockSpec((1, tm, d), lambda i, t: (i, t, 0)),
        pl.BlockSpec((1, SUBLANES, d), (lambda i, t: (i, 0, 0)) if per_batch_mod else (lambda i, t: (0, 0, 0))),
        _const_spec(gains.shape),
        _const_spec(w.shape),
    ]
    args = [x, mod, gains, w]
    if qk_norm:
        in_specs += [_const_spec(a.shape) for a in norm_args]
        args += list(norm_args)
    if rope:
        in_specs += [pl.BlockSpec((tm, LANES), lambda i, t: (t, 0))] * 2
        args += list(rope_args)
    vt_rows = n_v // LANES * VT_CHUNK
    out_shape = [jax.ShapeDtypeStruct((b, n_q, s), BF16), jax.ShapeDtypeStruct((b, s, n_k), BF16),
                 jax.ShapeDtypeStruct((b, vt_rows, s), BF16)]
    out_specs = [pl.BlockSpec((1, n_q, tm), lambda i, t: (i, 0, t)),
                 pl.BlockSpec((1, tm, n_k), lambda i, t: (i, t, 0)),
                 pl.BlockSpec((1, vt_rows, tm), lambda i, t: (i, 0, t))]
    if n_u:
        out_shape.append(jax.ShapeDtypeStruct((b, s, n_u), F32))
        out_specs.append(pl.BlockSpec((1, tm, n_u), lambda i, t: (i, t, 0)))
    return pl.pallas_call(
        functools.partial(_pre_kernel, n_q=n_q, n_k=n_k, n_v=n_v, n_u=n_u, qk_norm=qk_norm, rope=rope),
        grid=(b, s // tm),
        in_specs=in_specs,
        out_specs=out_specs,
        out_shape=out_shape,
        compiler_params=_params(),
        name="pre_even" if qk_norm else "pre_odd",
    )(*args)


def _attn_kernel(*refs, mode, n_pairs, n_sub, tq, seq, use_sink, cpi):
    qt_ref = refs[0]
    pos = 1
    if mode != "ctx":
        k_ref, vt_ref = refs[pos:pos + 2]
        pos += 2
    kc_ref, vct_ref = refs[pos:pos + 2]
    pos += 2
    if use_sink:
        sink_ref = refs[pos]
        pos += 1
    o_ref, s_ref, p_ref = refs[pos:pos + 3]

    sub = QUERY_SUB
    row = lax.broadcasted_iota(jnp.int32, (LANES, 1), 0)
    slot_a = (row // ROPE_HALF) % 2 == 0
    band = sub + 2 * WINDOW
    if mode == "window":
        rel = (lax.broadcasted_iota(jnp.int32, (band, sub), 0)
               - lax.broadcasted_iota(jnp.int32, (band, sub), 1))

    def block_geometry(sb):
        q0 = pl.program_id(1) * tq + sb * sub
        start = pl.multiple_of(jnp.clip(q0 - WINDOW, 0, seq - band), LANES)
        bias = jnp.where(jnp.abs(rel + (start - q0)) <= WINDOW, 0.0, -jnp.inf).astype(F32)
        return start, jnp.concatenate([bias] * (2 * cpi), axis=1)

    c_len = kc_ref.shape[1]
    first = {"dense": seq, "window": band, "ctx": 0}[mode]
    blocks = [(True, r, min(KEY_BLOCK, first - r), r) for r in range(0, first, KEY_BLOCK)]
    blocks += [(False, r, min(KEY_BLOCK, c_len - r), first + r) for r in range(0, c_len, KEY_BLOCK)]

    def item_chunks(item):
        return range(cpi * item, cpi * (item + 1))

    def item_pair(item):
        return cpi * item // GQA_GROUP

    def item_heads(item):
        return sum((_chunk_heads(c) for c in item_chunks(item)), ())

    def query_operand(sb, item):
        cols = []
        for c in item_chunks(item):
            qt = qt_ref[0, c * LANES:(c + 1) * LANES, sb * sub:(sb + 1) * sub]
            zero = jnp.zeros_like(qt)
            cols += [jnp.where(slot_a, qt, zero), jnp.where(slot_a, zero, qt)]
        return jnp.concatenate(cols, axis=1)

    def score_block(slot, item, geometry, rhs, blk, m):
        latent, r0, rows, b0 = blk
        psl = slice(item_pair(item) * LANES, (item_pair(item) + 1) * LANES)
        if not latent:
            s = _dot(kc_ref[0, r0:r0 + rows, psl], rhs)
        elif mode == "dense":
            s = _dot(k_ref[0, r0:r0 + rows, psl], rhs)
        else:
            start, bias = geometry
            s = _dot(k_ref[0, pl.ds(pl.multiple_of(start + r0, LANES), rows), psl], rhs) + bias[r0:r0 + rows]
        s_ref[slot, b0:b0 + rows, :] = s
        top = s.max(axis=0, keepdims=True)
        return top if m is None else jnp.maximum(m, top)

    def exp_block(slot, blk, m):
        _, _, rows, b0 = blk
        p_ref[slot, b0:b0 + rows, :] = jnp.exp2(s_ref[slot, b0:b0 + rows, :] - m).astype(BF16)

    def value_block(slot, item, geometry, blk, acc):
        latent, r0, rows, b0 = blk
        pair = item_pair(item)
        vsl = slice(pair * VT_CHUNK, (pair + 1) * VT_CHUNK)
        if not latent:
            v_t = vct_ref[0, vsl, r0:r0 + rows]
        elif mode == "dense":
            v_t = vt_ref[0, vsl, r0:r0 + rows]
        else:
            v_t = vt_ref[0, vsl, pl.ds(pl.multiple_of(geometry[0] + r0, LANES), rows)]
        pv = _dot(v_t, p_ref[slot, b0:b0 + rows, :])
        return pv if acc is None else acc + pv

    def with_sink(item, m):
        if not use_sink:
            return m, None
        sink = jnp.concatenate([jnp.full((1, sub), sink_ref[h], F32) for h in item_heads(item)], axis=1) * LOG2E
        m = jnp.maximum(m, sink)
        return m, jnp.exp2(sink - m)

    def store_output(sb, item, o4, sink_term):
        denom = o4[LANES:LANES + 1, :]
        if use_sink:
            denom = denom + sink_term
        o4 = o4[:LANES, :] * (1.0 / denom)
        for j, h in enumerate(item_heads(item)):
            rows = slice(0, HEAD_DIM) if j % 2 == 0 else slice(HEAD_DIM, 2 * HEAD_DIM)
            o_ref[0, h * HEAD_DIM:(h + 1) * HEAD_DIM, sb * sub:(sb + 1) * sub] = (
                o4[rows, j * sub:(j + 1) * sub].astype(BF16))

    items = [(sb, it) for sb in range(n_sub) for it in range(GQA_GROUP * n_pairs // cpi)]
    geometries = [block_geometry(sb) if mode == "window" else None for sb in range(n_sub)]
    n_items = len(items)
    m_cur = sink_cur = sink_prev = None
    for i in range(-1, n_items + 1):
        nxt, prv = i + 1, i - 1
        if nxt < n_items:
            rhs = query_operand(*items[nxt])
        m_next = o_acc = None
        for blk in blocks:
            if nxt < n_items:
                m_next = score_block(nxt % 2, items[nxt][1], geometries[items[nxt][0]], rhs, blk, m_next)
            if 0 <= i < n_items:
                exp_block(i % 2, blk, m_cur)
            if 0 <= prv:
                o_acc = value_block(prv % 2, items[prv][1], geometries[items[prv][0]], blk, o_acc)
        if 0 <= prv:
            store_output(*items[prv], o_acc, sink_prev)
        sink_prev = sink_cur
        if nxt < n_items:
            m_cur, sink_cur = with_sink(items[nxt][1], m_next)


def _attention(qt, kv, kv_ctx, sink, *, mode, tq):
    b, wq, sq = qt.shape
    n_pairs = wq // (2 * GQA_GROUP * HEAD_DIM)
    kc, vct = kv_ctx
    c_len = kc.shape[1]
    use_sink = sink is not None
    in_specs = [pl.BlockSpec((1, wq, tq), lambda i, t: (i, 0, t))]
    args = [qt]
    seq = sq
    if mode != "ctx":
        k, vt = kv
        seq = k.shape[1]
        in_specs += [pl.BlockSpec((1, seq, k.shape[2]), lambda i, t: (i, 0, 0)),
                     pl.BlockSpec((1, vt.shape[1], seq), lambda i, t: (i, 0, 0))]
        args += [k, vt]
    in_specs += [pl.BlockSpec((1, c_len, kc.shape[2]), lambda i, t: (i, 0, 0)),
                 pl.BlockSpec((1, vct.shape[1], c_len), lambda i, t: (i, 0, 0))]
    args += [kc, vct]
    if use_sink:
        in_specs.append(pl.BlockSpec(memory_space=pltpu.SMEM))
        args.append(sink)
    n_keys = c_len + {"dense": seq, "window": QUERY_SUB + 2 * WINDOW, "ctx": 0}[mode]
    cpi = ITEM_CHUNKS
    return pl.pallas_call(
        functools.partial(_attn_kernel, mode=mode, n_pairs=n_pairs, n_sub=tq // QUERY_SUB, tq=tq, seq=seq,
                          use_sink=use_sink, cpi=cpi),
        grid=(b, sq // tq),
        in_specs=in_specs,
        out_specs=pl.BlockSpec((1, wq, tq), lambda i, t: (i, 0, t)),
        out_shape=jax.ShapeDtypeStruct((b, wq, sq), BF16),
        scratch_shapes=[pltpu.VMEM((2, n_keys, 2 * cpi * QUERY_SUB), F32),
                        pltpu.VMEM((2, n_keys, 2 * cpi * QUERY_SUB), BF16)],
        compiler_params=_params(),
        name="attn_" + mode,
    )(*args)


def _post_kernel(*refs, pool, tm, seq):
    x_ref, at_ref, mod_ref, gains_ref, wo_ref, wfi_ref, wfo_ref = refs[:7]
    pos = 7
    if pool:
        u_ref, uprev_ref, unext_ref, wpool_ref, pscale_ref = refs[pos:pos + 5]
        pos += 5
    o_ref = refs[pos]
    hid_ref = refs[pos + 1]
    if pool:
        ext_ref = refs[pos + 2]

    n_a = at_ref.shape[1]
    if pool:
        t = pl.program_id(1)
        last = pl.num_programs(1) - 1
        halo = jnp.zeros((POOL_HALO, ext_ref.shape[1]), F32)
        ext_ref[0:POOL_HALO, :] = jnp.where(t > 0, uprev_ref[0], halo)
        ext_ref[POOL_HALO:POOL_HALO + tm, :] = u_ref[0]
        ext_ref[POOL_HALO + tm:, :] = jnp.where(t < last, unext_ref[0], halo)

    def pool_mix(r0, rows):
        n_ext = rows + 2 * POOL_HALO
        tok = t * tm + r0 + lax.broadcasted_iota(jnp.int32, (rows, 1), 0)

        def ahead(v, k):
            return pltpu.roll(v, n_ext - k, 0)

        def behind(v, k):
            return pltpu.roll(v, k, 0)

        mixed = []
        for g, w in enumerate(POOL_WINDOWS):
            sl = slice(g * LANES, (g + 1) * LANES)
            e = ext_ref[r0:r0 + n_ext, sl]
            run, length = e, 1
            while length < w // 2:
                run, length = run + ahead(run, length), 2 * length
            total = run + behind(run, w // 2)
            total = total[POOL_HALO:POOL_HALO + rows]
            cnt = jnp.minimum(tok + (w - w // 2), seq) - jnp.maximum(tok - w // 2, 0)
            diff = total / cnt.astype(F32) - e[POOL_HALO:POOL_HALO + rows]
            mixed.append((_dot(diff.astype(BF16), wpool_ref[g]) * pscale_ref[:, sl]).astype(BF16))
        return jnp.concatenate(mixed, axis=-1)

    def attn_out(r0, rows):
        return _dot_tn(at_ref[0, :, r0:r0 + rows], wo_ref[0:n_a, :])

    def pool_out(r0, rows, y):
        return y + _dot(pool_mix(r0, rows), wo_ref[n_a:, :]) if pool else y

    def residual_and_norm(r0, rows, y):
        x1 = x_ref[0, r0:r0 + rows, :] + mod_ref[0, 2:3, :] * (_rms_normalise(y) * gains_ref[1:2, :])
        h = _rms_normalise(x1) * gains_ref[2:3, :]
        return x1, (h * (1.0 + mod_ref[0, 4:5, :]) + mod_ref[0, 3:4, :]).astype(BF16)

    def ffn_hidden(r0, rows, h):
        for i in range(FFN_HIDDEN // FFN_CHUNK):
            gate = _dot(h, wfi_ref[:, i * FFN_CHUNK:(i + 1) * FFN_CHUNK])
            up = _dot(h, wfi_ref[:, FFN_HIDDEN + i * FFN_CHUNK:FFN_HIDDEN + (i + 1) * FFN_CHUNK])
            hid_ref[r0:r0 + rows, i * FFN_CHUNK:(i + 1) * FFN_CHUNK] = (_silu(gate) * up).astype(BF16)

    def ffn_out(r0, rows, x1):
        z = _dot(hid_ref[r0:r0 + rows, :], wfo_ref[...])
        o_ref[0, r0:r0 + rows, :] = x1 + mod_ref[0, 5:6, :] * (_rms_normalise(z) * gains_ref[3:4, :])

    rows = min(tm // 2, POST_SUB_TILE)
    starts = list(range(0, tm, rows))
    n = len(starts)
    y = {p: pool_out(starts[p], rows, attn_out(starts[p], rows)) for p in range(min(2, n))}
    normed = {0: residual_and_norm(starts[0], rows, y.pop(0))}
    for p in range(n):
        x1, h = normed.pop(p)
        ffn_hidden(starts[p], rows, h)
        if p + 2 < n:
            y[p + 2] = pool_out(starts[p + 2], rows, attn_out(starts[p + 2], rows))
        if p + 1 < n:
            normed[p + 1] = residual_and_norm(starts[p + 1], rows, y.pop(p + 1))
        ffn_out(starts[p], rows, x1)


def _post(x, a_t, mod, gains, wo, wfi, wfo, pool_args, *, tm):
    b, s, d = x.shape
    per_batch_mod = mod.shape[0] > 1
    pool = pool_args is not None
    in_specs = [
        pl.BlockSpec((1, tm, d), lambda i, t: (i, t, 0)),
        pl.BlockSpec((1, a_t.shape[1], tm), lambda i, t: (i, 0, t)),
        pl.BlockSpec((1, SUBLANES, d), (lambda i, t: (i, 0, 0)) if per_batch_mod else (lambda i, t: (0, 0, 0))),
        _const_spec(gains.shape), _const_spec(wo.shape), _const_spec(wfi.shape), _const_spec(wfo.shape),
    ]
    args = [x, a_t, mod, gains, wo, wfi, wfo]
    scratch = [pltpu.VMEM((tm, FFN_HIDDEN), BF16)]
    if pool:
        u, wpool, pscale = pool_args
        nu = u.shape[2]
        per_tile = tm // POOL_HALO
        n_halo_blocks = s // POOL_HALO
        in_specs += [
            pl.BlockSpec((1, tm, nu), lambda i, t: (i, t, 0)),
            pl.BlockSpec((1, POOL_HALO, nu), lambda i, t: (i, jnp.maximum(t * per_tile - 1, 0), 0)),
            pl.BlockSpec((1, POOL_HALO, nu),
                         lambda i, t: (i, jnp.minimum((t + 1) * per_tile, n_halo_blocks - 1), 0)),
            _const_spec(wpool.shape), _const_spec(pscale.shape),
        ]
        args += [u, u, u, wpool, pscale]
        scratch.append(pltpu.VMEM((tm + 2 * POOL_HALO, nu), F32))
    return pl.pallas_call(
        functools.partial(_post_kernel, pool=pool, tm=tm, seq=s),
        grid=(b, s // tm),
        in_specs=in_specs,
        out_specs=pl.BlockSpec((1, tm, d), lambda i, t: (i, t, 0)),
        out_shape=jax.ShapeDtypeStruct((b, s, d), F32),
        scratch_shapes=scratch,
        compiler_params=_params(),
        name="post_even" if pool else "post_odd",
    )(*args)


def _pad_rows(a, rows):
    return jnp.pad(a, [(0, 0)] * (a.ndim - 2) + [(0, rows - a.shape[-2]), (0, 0)])


def kernel(x, c, ctx, c_ctx, w_mod, b_mod, g_pre_mix, g_post_mix, g_pre_ffn, g_post_ffn, we_in, we_out,
           we_q_gain, we_k_gain, we_pool, we_pool_scale, wo_in, wo_out, wo_sink, w_ffn_in, w_ffn_out):
    batch, seq, d = x.shape
    c_len = ctx.shape[1]
    depth = w_mod.shape[0]
    tm = TOKEN_TILE
    tm_ctx = c_len
    assert d == D_MODEL and w_ffn_out.shape[1] == FFN_HIDDEN and w_mod.shape[2] % MOD_COLS_TILE == 0
    assert seq % PRE_TILE == 0 and seq % TOKEN_TILE == 0 and seq % QUERY_TILE == 0 and seq % GRID_W == 0
    assert seq >= QUERY_SUB + 2 * WINDOW and c_len % PRE_SUB_TILE == 0 and c_len % QUERY_SUB == 0
    assert max(POOL_WINDOWS) // 2 <= POOL_HALO

    cc = jnp.concatenate([c, c_ctx[None, :]], axis=0)
    cc = _pad_rows(cc, -(-(batch + 1) // SUBLANES) * SUBLANES)
    mods = _modulation(cc, w_mod, b_mod)

    cos, sin = _rope_tables(seq)
    _, lane_dim = _lane_slot_dim()
    head_mean = jnp.asarray(_head_mean_matrix(), BF16)

    for l in range(depth):
        even = l % 2 == 0
        i = l // 2
        with_ctx = l < depth - 1
        mod_x = _pad_rows(mods[l, :batch].reshape(batch, 6, d), SUBLANES)
        mod_c = _pad_rows(mods[l, batch:batch + 1].reshape(1, 6, d), SUBLANES)
        gains = _pad_rows(jnp.stack([g_pre_mix[l], g_post_mix[l], g_pre_ffn[l], g_post_ffn[l]]), SUBLANES)
        wfi = w_ffn_in[l].astype(BF16)
        wfo = w_ffn_out[l].astype(BF16)

        if even:
            n_heads, n_kv = 8, 2
            w_full, w_out = we_in[i], we_out[i].astype(BF16)
            norm_args = (head_mean, we_q_gain[i][lane_dim][None, :], we_k_gain[i][lane_dim][None, :])
            sink = None
            mode = "dense"
        else:
            n_heads, n_kv = 16, 4
            w_full, w_out = wo_in[i], wo_out[i].astype(BF16)
            norm_args = None
            sink = wo_sink[i]
            mode = "window"
        q_w, kv_w = n_heads * HEAD_DIM, n_kv * HEAD_DIM
        cols = np.concatenate([_q_cols(n_heads), _k_cols(n_kv, q_w), np.arange(q_w + kv_w, w_full.shape[1])])
        w_in = w_full[:, cols].astype(BF16)
        widths = (q_w, kv_w, kv_w, w_full.shape[1] - q_w - 2 * kv_w)

        lat = _pre(x, mod_x, gains, w_in, norm_args, (cos, sin), widths=widths, tm=PRE_TILE)
        con = _pre(ctx, mod_c, gains, w_in, norm_args, None, widths=widths, tm=tm_ctx)
        a_t = _attention(lat[0], (lat[1], lat[2]), (con[1], con[2]), sink, mode=mode, tq=QUERY_TILE)
        pool_w = (we_pool[i].astype(BF16), we_pool_scale[i][None, :]) if even else None
        x = _post(x, a_t, mod_x, gains, w_out, wfi, wfo, (lat[3],) + pool_w if even else None, tm=tm)
        if with_ctx:
            ac_t = _attention(con[0], None, (con[1], con[2]), sink, mode="ctx", tq=c_len)
            ctx = _post(ctx, ac_t, mod_c, gains, w_out, wfi, wfo,
                        (con[3],) + pool_w if even else None, tm=tm_ctx)
    return x
```

```python
import functools
import math

import numpy as np
import jax
import jax.numpy as jnp
from jax import lax
from jax.experimental import pallas as pl
from jax.experimental.pallas import tpu as pltpu

D_MODEL = 1024
HEAD_DIM = 64
GQA_GROUP = 4
GRID_W = 64
ROPE_THETA = 10000.0
EPS = 1e-6
WINDOW = 128
POOL_WINDOWS = (2, 4, 8, 16)
POOL_HALO = 8
FFN_HIDDEN = 2816
FFN_CHUNK = 256
LANES = 128
SUBLANES = 8
ROPE_HALF = HEAD_DIM // 2
LOG2E = math.log2(math.e)
Q_SCALE = LOG2E / math.sqrt(HEAD_DIM)
VMEM_LIMIT = 56 * 1024 * 1024
MOD_COLS_TILE = 1536
TOKEN_TILE = 512
POST_SUB_TILE = 256
PRE_TILE = 1024
PRE_SUB_TILE = 256
QUERY_TILE = 1024
QUERY_SUB = LANES
ITEM_CHUNKS = 2
KEY_BLOCK = 256
VALUE_GROUP = 1024
VT_ONES_ROWS = 16
VT_CHUNK = LANES + VT_ONES_ROWS

F32 = jnp.float32
BF16 = jnp.bfloat16


def _lane_slot_dim():
    quarter = HEAD_DIM // 4
    lane = np.arange(LANES)
    part = lane // ROPE_HALF
    i = lane % ROPE_HALF
    dim = np.where(i < quarter, i, ROPE_HALF + (i - quarter)) + np.where(part >= 2, quarter, 0)
    return part % 2, dim


def _chunk_heads(c):
    head_a = (c // GQA_GROUP) * 2 * GQA_GROUP + c % GQA_GROUP
    return head_a, head_a + GQA_GROUP


def _q_cols(n_heads):
    slot, dim = _lane_slot_dim()
    cols = []
    for c in range(n_heads // 2):
        head_a, head_b = _chunk_heads(c)
        cols.append(np.where(slot == 0, head_a, head_b) * HEAD_DIM + dim)
    return np.concatenate(cols)


def _k_cols(n_kv, base):
    slot, dim = _lane_slot_dim()
    return np.concatenate([base + (2 * m + slot) * HEAD_DIM + dim for m in range(n_kv // 2)])


def _head_mean_matrix():
    slot, _ = _lane_slot_dim()
    mean = (slot[:, None] == slot[None, :]).astype(np.float32) / HEAD_DIM
    return np.concatenate([mean, mean], axis=0)


def _rope_tables(seq):
    quarter = HEAD_DIM // 4
    freqs = ROPE_THETA ** (-jnp.arange(quarter, dtype=F32) / quarter)
    t = jnp.arange(seq, dtype=jnp.int32)
    rows = (t // GRID_W).astype(F32)[:, None] * freqs[None, :]
    cols = (t % GRID_W).astype(F32)[:, None] * freqs[None, :]
    ang = jnp.concatenate([rows, cols], axis=-1)
    cos = jnp.tile(jnp.cos(ang), (1, 4))
    sin = jnp.sin(ang)
    sin = jnp.concatenate([-sin, -sin, sin, sin], axis=-1)
    return cos, sin


def _rms_normalise(x):
    return x * lax.rsqrt(jnp.mean(x * x, axis=-1, keepdims=True) + EPS)


def _silu(x):
    return x * (1.0 / (1.0 + jnp.exp(-x)))


def _dot(a, b):
    return jnp.dot(a, b, preferred_element_type=F32)


def _dot_tn(a_t, b):
    return lax.dot_general(a_t, b, (((0,), (0,)), ((), ())), preferred_element_type=F32)


def _const_spec(shape):
    return pl.BlockSpec(shape, lambda *_: (0,) * len(shape), pipeline_mode=pl.Buffered(1))


def _params():
    return pltpu.CompilerParams(dimension_semantics=("parallel", "parallel"), vmem_limit_bytes=VMEM_LIMIT)


def _mod_kernel(c_ref, w_ref, b_ref, o_ref):
    s = _silu(c_ref[...]).astype(BF16)
    o_ref[0] = _dot(s, w_ref[0].astype(BF16)) + b_ref[0]


def _modulation(cc, w_mod, b_mod):
    depth, d, n = w_mod.shape
    rows = cc.shape[0]
    tn = MOD_COLS_TILE
    return pl.pallas_call(
        _mod_kernel,
        grid=(depth, n // tn),
        in_specs=[
            pl.BlockSpec((rows, d), lambda l, j: (0, 0)),
            pl.BlockSpec((1, d, tn), lambda l, j: (l, 0, j)),
            pl.BlockSpec((1, 1, tn), lambda l, j: (l, 0, j)),
        ],
        out_specs=pl.BlockSpec((1, rows, tn), lambda l, j: (l, 0, j)),
        out_shape=jax.ShapeDtypeStruct((depth, rows, n), F32),
        compiler_params=_params(),
        name="modulation",
    )(cc, w_mod, b_mod.reshape(depth, 1, n))


def _pre_kernel(*refs, n_q, n_k, n_v, n_u, qk_norm, rope):
    x_ref, mod_ref, gains_ref, w_ref = refs[:4]
    pos = 4
    if qk_norm:
        pm_ref, qg_ref, kg_ref = refs[pos:pos + 3]
        pos += 3
    if rope:
        cos_ref, sin_ref = refs[pos:pos + 2]
        pos += 2
    qt_ref, k_ref, vt_ref = refs[pos:pos + 3]
    u_ref = refs[pos + 3] if n_u else None

    def modulated(rows):
        h = _rms_normalise(x_ref[0, rows, :]) * gains_ref[0:1, :]
        return (h * (1.0 + mod_ref[0, 1:2, :]) + mod_ref[0, 0:1, :]).astype(BF16)

    def head_chunk(c, gain_ref, rows):
        if qk_norm:
            c2 = c * c
            hi = c2.astype(BF16)
            lo = (c2 - hi.astype(F32)).astype(BF16)
            ms = _dot(jnp.concatenate([hi, lo], axis=1), pm_ref[...])
            c = c * lax.rsqrt(ms + EPS) * gain_ref[...]
        if rope:
            c = c * cos_ref[rows, :] + pltpu.roll(c, LANES // 2, 1) * sin_ref[rows, :]
        return c

    def project(rows, h):
        p = _dot(h, w_ref[...])
        for j in range(n_q // LANES):
            sl = slice(j * LANES, (j + 1) * LANES)
            c = head_chunk(p[:, sl], qg_ref if qk_norm else None, rows) * Q_SCALE
            qt_ref[0, sl, rows] = c.T.astype(BF16)
        for j in range(n_k // LANES):
            sl = slice(j * LANES, (j + 1) * LANES)
            c = head_chunk(p[:, n_q + j * LANES:n_q + (j + 1) * LANES], kg_ref if qk_norm else None, rows)
            k_ref[0, rows, sl] = c.astype(BF16)
        ones = jnp.ones((VT_ONES_ROWS, p.shape[0]), BF16)
        for j in range(n_v // LANES):
            r0 = j * VT_CHUNK
            vt_ref[0, r0:r0 + LANES, rows] = (
                p[:, n_q + n_k + j * LANES:n_q + n_k + (j + 1) * LANES].T.astype(BF16))
            vt_ref[0, r0 + LANES:r0 + VT_CHUNK, rows] = ones
        if n_u:
            u_ref[0, rows, :] = p[:, n_q + n_k + n_v:]

    tm = x_ref.shape[1]
    step = min(tm, PRE_SUB_TILE)
    parts = [slice(r, r + step) for r in range(0, tm, step)]
    h_next = modulated(parts[0])
    for i, rows in enumerate(parts):
        h_cur = h_next
        if i + 1 < len(parts):
            h_next = modulated(parts[i + 1])
        project(rows, h_cur)


def _pre(x, mod, gains, w, norm_args, rope_args, *, widths, tm):
    n_q, n_k, n_v, n_u = widths
    b, s, d = x.shape
    per_batch_mod = mod.shape[0] > 1
    qk_norm = norm_args is not None
    rope = rope_args is not None
    in_specs = [
        pl.BlockSpec((1, tm, d), lambda i, t: (i, t, 0)),
        pl.BlockSpec((1, SUBLANES, d), (lambda i, t: (i, 0, 0)) if per_batch_mod else (lambda i, t: (0, 0, 0))),
        _const_spec(gains.shape),
        _const_spec(w.shape),
    ]
    args = [x, mod, gains, w]
    if qk_norm:
        in_specs += [_const_spec(a.shape) for a in norm_args]
        args += list(norm_args)
    if rope:
        in_specs += [pl.BlockSpec((tm, LANES), lambda i, t: (t, 0))] * 2
        args += list(rope_args)
    vt_rows = n_v // LANES * VT_CHUNK
    out_shape = [jax.ShapeDtypeStruct((b, n_q, s), BF16), jax.ShapeDtypeStruct((b, s, n_k), BF16),
                 jax.ShapeDtypeStruct((b, vt_rows, s), BF16)]
    out_specs = [pl.BlockSpec((1, n_q, tm), lambda i, t: (i, 0, t)),
                 pl.BlockSpec((1, tm, n_k), lambda i, t: (i, t, 0)),
                 pl.BlockSpec((1, vt_rows, tm), lambda i, t: (i, 0, t))]
    if n_u:
        out_shape.append(jax.ShapeDtypeStruct((b, s, n_u), F32))
        out_specs.append(pl.BlockSpec((1, tm, n_u), lambda i, t: (i, t, 0)))
    return pl.pallas_call(
        functools.partial(_pre_kernel, n_q=n_q, n_k=n_k, n_v=n_v, n_u=n_u, qk_norm=qk_norm, rope=rope),
        grid=(b, s // tm),
        in_specs=in_specs,
        out_specs=out_specs,
        out_shape=out_shape,
        compiler_params=_params(),
        name="pre_even" if qk_norm else "pre_odd",
    )(*args)


def _attn_kernel(*refs, mode, n_pairs, n_sub, tq, seq, use_sink, cpi):
    qt_ref = refs[0]
    pos = 1
    if mode != "ctx":
        k_ref, vt_ref = refs[pos:pos + 2]
        pos += 2
    kc_ref, vct_ref = refs[pos:pos + 2]
    pos += 2
    if use_sink:
        sink_ref = refs[pos]
        pos += 1
    o_ref, s_ref, p_ref = refs[pos:pos + 3]

    sub = QUERY_SUB
    row = lax.broadcasted_iota(jnp.int32, (LANES, 1), 0)
    slot_a = (row // ROPE_HALF) % 2 == 0
    band = sub + 2 * WINDOW
    if mode == "window":
        rel = (lax.broadcasted_iota(jnp.int32, (band, sub), 0)
               - lax.broadcasted_iota(jnp.int32, (band, sub), 1))

    def block_geometry(sb):
        q0 = pl.program_id(1) * tq + sb * sub
        start = pl.multiple_of(jnp.clip(q0 - WINDOW, 0, seq - band), LANES)
        bias = jnp.where(jnp.abs(rel + (start - q0)) <= WINDOW, 0.0, -jnp.inf).astype(F32)
        return start, jnp.concatenate([bias] * (2 * cpi), axis=1)

    c_len = kc_ref.shape[1]
    first = {"dense": seq, "window": band, "ctx": 0}[mode]
    blocks = [(True, r, min(KEY_BLOCK, first - r), r) for r in range(0, first, KEY_BLOCK)]
    blocks += [(False, r, min(KEY_BLOCK, c_len - r), first + r) for r in range(0, c_len, KEY_BLOCK)]
    value_groups = {}
    for bi, (latent, r0, rows, b0) in enumerate(blocks):
        if r0 % VALUE_GROUP:
            _, g0, g_rows, gb0 = value_groups.pop(bi - 1)
            value_groups[bi] = (latent, g0, g_rows + rows, gb0)
        else:
            value_groups[bi] = (latent, r0, rows, b0)

    def item_chunks(item):
        return range(cpi * item, cpi * (item + 1))

    def item_pair(item):
        return cpi * item // GQA_GROUP

    def item_heads(item):
        return sum((_chunk_heads(c) for c in item_chunks(item)), ())

    def query_operand(sb, item):
        cols = []
        for c in item_chunks(item):
            qt = qt_ref[0, c * LANES:(c + 1) * LANES, sb * sub:(sb + 1) * sub]
            zero = jnp.zeros_like(qt)
            cols += [jnp.where(slot_a, qt, zero), jnp.where(slot_a, zero, qt)]
        return jnp.concatenate(cols, axis=1)

    def score_block(slot, item, geometry, rhs, blk, m):
        latent, r0, rows, b0 = blk
        psl = slice(item_pair(item) * LANES, (item_pair(item) + 1) * LANES)
        if not latent:
            s = _dot(kc_ref[0, r0:r0 + rows, psl], rhs)
        elif mode == "dense":
            s = _dot(k_ref[0, r0:r0 + rows, psl], rhs)
        else:
            start, bias = geometry
            s = _dot(k_ref[0, pl.ds(pl.multiple_of(start + r0, LANES), rows), psl], rhs) + bias[r0:r0 + rows]
        s_ref[slot, b0:b0 + rows, :] = s
        top = s.max(axis=0, keepdims=True)
        return top if m is None else jnp.maximum(m, top)

    def exp_block(slot, blk, m):
        _, _, rows, b0 = blk
        p_ref[slot, b0:b0 + rows, :] = jnp.exp2(s_ref[slot, b0:b0 + rows, :] - m).astype(BF16)

    def value_block(slot, item, geometry, blk, acc):
        latent, r0, rows, b0 = blk
        pair = item_pair(item)
        vsl = slice(pair * VT_CHUNK, (pair + 1) * VT_CHUNK)
        if not latent:
            v_t = vct_ref[0, vsl, r0:r0 + rows]
        elif mode == "dense":
            v_t = vt_ref[0, vsl, r0:r0 + rows]
        else:
            v_t = vt_ref[0, vsl, pl.ds(pl.multiple_of(geometry[0] + r0, LANES), rows)]
        pv = _dot(v_t, p_ref[slot, b0:b0 + rows, :])
        return pv if acc is None else acc + pv

    def with_sink(item, m):
        if not use_sink:
            return m, None
        sink = jnp.concatenate([jnp.full((1, sub), sink_ref[h], F32) for h in item_heads(item)], axis=1) * LOG2E
        m = jnp.maximum(m, sink)
        return m, jnp.exp2(sink - m)

    def store_output(sb, item, o4, sink_term):
        denom = o4[LANES:LANES + 1, :]
        if use_sink:
            denom = denom + sink_term
        o4 = o4[:LANES, :] * (1.0 / denom)
        for j, h in enumerate(item_heads(item)):
            rows = slice(0, HEAD_DIM) if j % 2 == 0 else slice(HEAD_DIM, 2 * HEAD_DIM)
            o_ref[0, h * HEAD_DIM:(h + 1) * HEAD_DIM, sb * sub:(sb + 1) * sub] = (
                o4[rows, j * sub:(j + 1) * sub].astype(BF16))

    items = [(sb, it) for sb in range(n_sub) for it in range(GQA_GROUP * n_pairs // cpi)]
    geometries = [block_geometry(sb) if mode == "window" else None for sb in range(n_sub)]
    n_items = len(items)
    m_cur = sink_cur = sink_prev = None
    for i in range(-1, n_items + 1):
        nxt, prv = i + 1, i - 1
        if nxt < n_items:
            rhs = query_operand(*items[nxt])
        m_next = o_acc = None
        for bi, blk in enumerate(blocks):
            if nxt < n_items:
                m_next = score_block(nxt % 2, items[nxt][1], geometries[items[nxt][0]], rhs, blk, m_next)
            if 0 <= i < n_items:
                exp_block(i % 2, blk, m_cur)
            if 0 <= prv and bi in value_groups:
                o_acc = value_block(prv % 2, items[prv][1], geometries[items[prv][0]], value_groups[bi], o_acc)
        if 0 <= prv:
            store_output(*items[prv], o_acc, sink_prev)
        sink_prev = sink_cur
        if nxt < n_items:
            m_cur, sink_cur = with_sink(items[nxt][1], m_next)


def _attention(qt, kv, kv_ctx, sink, *, mode, tq):
    b, wq, sq = qt.shape
    n_pairs = wq // (2 * GQA_GROUP * HEAD_DIM)
    kc, vct = kv_ctx
    c_len = kc.shape[1]
    use_sink = sink is not None
    in_specs = [pl.BlockSpec((1, wq, tq), lambda i, t: (i, 0, t))]
    args = [qt]
    seq = sq
    if mode != "ctx":
        k, vt = kv
        seq = k.shape[1]
        in_specs += [pl.BlockSpec((1, seq, k.shape[2]), lambda i, t: (i, 0, 0)),
                     pl.BlockSpec((1, vt.shape[1], seq), lambda i, t: (i, 0, 0))]
        args += [k, vt]
    in_specs += [pl.BlockSpec((1, c_len, kc.shape[2]), lambda i, t: (i, 0, 0)),
                 pl.BlockSpec((1, vct.shape[1], c_len), lambda i, t: (i, 0, 0))]
    args += [kc, vct]
    if use_sink:
        in_specs.append(pl.BlockSpec(memory_space=pltpu.SMEM))
        args.append(sink)
    n_keys = c_len + {"dense": seq, "window": QUERY_SUB + 2 * WINDOW, "ctx": 0}[mode]
    cpi = ITEM_CHUNKS
    return pl.pallas_call(
        functools.partial(_attn_kernel, mode=mode, n_pairs=n_pairs, n_sub=tq // QUERY_SUB, tq=tq, seq=seq,
                          use_sink=use_sink, cpi=cpi),
        grid=(b, sq // tq),
        in_specs=in_specs,
        out_specs=pl.BlockSpec((1, wq, tq), lambda i, t: (i, 0, t)),
        out_shape=jax.ShapeDtypeStruct((b, wq, sq), BF16),
        scratch_shapes=[pltpu.VMEM((2, n_keys, 2 * cpi * QUERY_SUB), F32),
                        pltpu.VMEM((2, n_keys, 2 * cpi * QUERY_SUB), BF16)],
        compiler_params=_params(),
        name="attn_" + mode,
    )(*args)


def _post_kernel(*refs, pool, tm, seq):
    x_ref, at_ref, mod_ref, gains_ref, wo_ref, wfi_ref, wfo_ref = refs[:7]
    pos = 7
    if pool:
        u_ref, uprev_ref, unext_ref, wpool_ref, pscale_ref = refs[pos:pos + 5]
        pos += 5
    o_ref = refs[pos]
    hid_ref = refs[pos + 1]
    if pool:
        ext_ref = refs[pos + 2]

    n_a = at_ref.shape[1]
    if pool:
        t = pl.program_id(1)
        last = pl.num_programs(1) - 1
        halo = jnp.zeros((POOL_HALO, ext_ref.shape[1]), F32)
        ext_ref[0:POOL_HALO, :] = jnp.where(t > 0, uprev_ref[0], halo)
        ext_ref[POOL_HALO:POOL_HALO + tm, :] = u_ref[0]
        ext_ref[POOL_HALO + tm:, :] = jnp.where(t < last, unext_ref[0], halo)

    def pool_mix(r0, rows):
        n_ext = rows + 2 * POOL_HALO
        tok = t * tm + r0 + lax.broadcasted_iota(jnp.int32, (rows, 1), 0)

        def ahead(v, k):
            return pltpu.roll(v, n_ext - k, 0)

        def behind(v, k):
            return pltpu.roll(v, k, 0)

        mixed = []
        for g, w in enumerate(POOL_WINDOWS):
            sl = slice(g * LANES, (g + 1) * LANES)
            e = ext_ref[r0:r0 + n_ext, sl]
            run, length = e, 1
            while length < w // 2:
                run, length = run + ahead(run, length), 2 * length
            total = run + behind(run, w // 2)
            total = total[POOL_HALO:POOL_HALO + rows]
            cnt = jnp.minimum(tok + (w - w // 2), seq) - jnp.maximum(tok - w // 2, 0)
            diff = total / cnt.astype(F32) - e[POOL_HALO:POOL_HALO + rows]
            mixed.append((_dot(diff.astype(BF16), wpool_ref[g]) * pscale_ref[:, sl]).astype(BF16))
        return jnp.concatenate(mixed, axis=-1)

    def attn_out(r0, rows):
        return _dot_tn(at_ref[0, :, r0:r0 + rows], wo_ref[0:n_a, :])

    def pool_out(r0, rows, y):
        return y + _dot(pool_mix(r0, rows), wo_ref[n_a:, :]) if pool else y

    def residual_and_norm(r0, rows, y):
        x1 = x_ref[0, r0:r0 + rows, :] + mod_ref[0, 2:3, :] * (_rms_normalise(y) * gains_ref[1:2, :])
        h = _rms_normalise(x1) * gains_ref[2:3, :]
        return x1, (h * (1.0 + mod_ref[0, 4:5, :]) + mod_ref[0, 3:4, :]).astype(BF16)

    def ffn_hidden(r0, rows, h):
        for i in range(FFN_HIDDEN // FFN_CHUNK):
            gate = _dot(h, wfi_ref[:, i * FFN_CHUNK:(i + 1) * FFN_CHUNK])
            up = _dot(h, wfi_ref[:, FFN_HIDDEN + i * FFN_CHUNK:FFN_HIDDEN + (i + 1) * FFN_CHUNK])
            hid_ref[r0:r0 + rows, i * FFN_CHUNK:(i + 1) * FFN_CHUNK] = (_silu(gate) * up).astype(BF16)

    def ffn_out(r0, rows, x1):
        z = _dot(hid_ref[r0:r0 + rows, :], wfo_ref[...])
        o_ref[0, r0:r0 + rows, :] = x1 + mod_ref[0, 5:6, :] * (_rms_normalise(z) * gains_ref[3:4, :])

    rows = min(tm // 2, POST_SUB_TILE)
    starts = list(range(0, tm, rows))
    n = len(starts)
    y = {p: pool_out(starts[p], rows, attn_out(starts[p], rows)) for p in range(min(2, n))}
    normed = {0: residual_and_norm(starts[0], rows, y.pop(0))}
    for p in range(n):
        x1, h = normed.pop(p)
        ffn_hidden(starts[p], rows, h)
        if p + 2 < n:
            y[p + 2] = pool_out(starts[p + 2], rows, attn_out(starts[p + 2], rows))
        if p + 1 < n:
            normed[p + 1] = residual_and_norm(starts[p + 1], rows, y.pop(p + 1))
        ffn_out(starts[p], rows, x1)


def _post(x, a_t, mod, gains, wo, wfi, wfo, pool_args, *, tm):
    b, s, d = x.shape
    per_batch_mod = mod.shape[0] > 1
    pool = pool_args is not None
    in_specs = [
        pl.BlockSpec((1, tm, d), lambda i, t: (i, t, 0)),
        pl.BlockSpec((1, a_t.shape[1], tm), lambda i, t: (i, 0, t)),
        pl.BlockSpec((1, SUBLANES, d), (lambda i, t: (i, 0, 0)) if per_batch_mod else (lambda i, t: (0, 0, 0))),
        _const_spec(gains.shape), _const_spec(wo.shape), _const_spec(wfi.shape), _const_spec(wfo.shape),
    ]
    args = [x, a_t, mod, gains, wo, wfi, wfo]
    scratch = [pltpu.VMEM((tm, FFN_HIDDEN), BF16)]
    if pool:
        u, wpool, pscale = pool_args
        nu = u.shape[2]
        per_tile = tm // POOL_HALO
        n_halo_blocks = s // POOL_HALO
        in_specs += [
            pl.BlockSpec((1, tm, nu), lambda i, t: (i, t, 0)),
            pl.BlockSpec((1, POOL_HALO, nu), lambda i, t: (i, jnp.maximum(t * per_tile - 1, 0), 0)),
            pl.BlockSpec((1, POOL_HALO, nu),
                         lambda i, t: (i, jnp.minimum((t + 1) * per_tile, n_halo_blocks - 1), 0)),
            _const_spec(wpool.shape), _const_spec(pscale.shape),
        ]
        args += [u, u, u, wpool, pscale]
        scratch.append(pltpu.VMEM((tm + 2 * POOL_HALO, nu), F32))
    return pl.pallas_call(
        functools.partial(_post_kernel, pool=pool, tm=tm, seq=s),
        grid=(b, s // tm),
        in_specs=in_specs,
        out_specs=pl.BlockSpec((1, tm, d), lambda i, t: (i, t, 0)),
        out_shape=jax.ShapeDtypeStruct((b, s, d), F32),
        scratch_shapes=scratch,
        compiler_params=_params(),
        name="post_even" if pool else "post_odd",
    )(*args)


def _pad_rows(a, rows):
    return jnp.pad(a, [(0, 0)] * (a.ndim - 2) + [(0, rows - a.shape[-2]), (0, 0)])


def kernel(x, c, ctx, c_ctx, w_mod, b_mod, g_pre_mix, g_post_mix, g_pre_ffn, g_post_ffn, we_in, we_out,
           we_q_gain, we_k_gain, we_pool, we_pool_scale, wo_in, wo_out, wo_sink, w_ffn_in, w_ffn_out):
    batch, seq, d = x.shape
    c_len = ctx.shape[1]
    depth = w_mod.shape[0]
    tm = TOKEN_TILE
    tm_ctx = c_len
    assert d == D_MODEL and w_ffn_out.shape[1] == FFN_HIDDEN and w_mod.shape[2] % MOD_COLS_TILE == 0
    assert seq % PRE_TILE == 0 and seq % TOKEN_TILE == 0 and seq % QUERY_TILE == 0 and seq % GRID_W == 0
    assert seq >= QUERY_SUB + 2 * WINDOW and c_len % PRE_SUB_TILE == 0 and c_len % QUERY_SUB == 0
    assert max(POOL_WINDOWS) // 2 <= POOL_HALO

    cc = jnp.concatenate([c, c_ctx[None, :]], axis=0)
    cc = _pad_rows(cc, -(-(batch + 1) // SUBLANES) * SUBLANES)
    mods = _modulation(cc, w_mod, b_mod)

    cos, sin = _rope_tables(seq)
    _, lane_dim = _lane_slot_dim()
    head_mean = jnp.asarray(_head_mean_matrix(), BF16)

    for l in range(depth):
        even = l % 2 == 0
        i = l // 2
        with_ctx = l < depth - 1
        mod_x = _pad_rows(mods[l, :batch].reshape(batch, 6, d), SUBLANES)
        mod_c = _pad_rows(mods[l, batch:batch + 1].reshape(1, 6, d), SUBLANES)
        gains = _pad_rows(jnp.stack([g_pre_mix[l], g_post_mix[l], g_pre_ffn[l], g_post_ffn[l]]), SUBLANES)
        wfi = w_ffn_in[l].astype(BF16)
        wfo = w_ffn_out[l].astype(BF16)

        if even:
            n_heads, n_kv = 8, 2
            w_full, w_out = we_in[i], we_out[i].astype(BF16)
            norm_args = (head_mean, we_q_gain[i][lane_dim][None, :], we_k_gain[i][lane_dim][None, :])
            sink = None
            mode = "dense"
        else:
            n_heads, n_kv = 16, 4
            w_full, w_out = wo_in[i], wo_out[i].astype(BF16)
            norm_args = None
            sink = wo_sink[i]
            mode = "window"
        q_w, kv_w = n_heads * HEAD_DIM, n_kv * HEAD_DIM
        cols = np.concatenate([_q_cols(n_heads), _k_cols(n_kv, q_w), np.arange(q_w + kv_w, w_full.shape[1])])
        w_in = w_full[:, cols].astype(BF16)
        widths = (q_w, kv_w, kv_w, w_full.shape[1] - q_w - 2 * kv_w)

        lat = _pre(x, mod_x, gains, w_in, norm_args, (cos, sin), widths=widths, tm=PRE_TILE)
        con = _pre(ctx, mod_c, gains, w_in, norm_args, None, widths=widths, tm=tm_ctx)
        a_t = _attention(lat[0], (lat[1], lat[2]), (con[1], con[2]), sink, mode=mode, tq=QUERY_TILE)
        pool_w = (we_pool[i].astype(BF16), we_pool_scale[i][None, :]) if even else None
        x = _post(x, a_t, mod_x, gains, w_out, wfi, wfo, (lat[3],) + pool_w if even else None, tm=tm)
        if with_ctx:
            ac_t = _attention(con[0], None, (con[1], con[2]), sink, mode="ctx", tq=c_len)
            ctx = _post(ctx, ac_t, mod_c, gains, w_out, wfi, wfo,
                        (con[3],) + pool_w if even else None, tm=tm_ctx)
    return x
```

```python
import functools
import math

import numpy as np
import jax
import jax.numpy as jnp
from jax import lax
from jax.experimental import pallas as pl
from jax.experimental.pallas import tpu as pltpu

D_MODEL = 1024
HEAD_DIM = 64
GQA_GROUP = 4
GRID_W = 64
ROPE_THETA = 10000.0
EPS = 1e-6
WINDOW = 128
POOL_WINDOWS = (2, 4, 8, 16)
POOL_HALO = 8
FFN_HIDDEN = 2816
FFN_CHUNK = 256
LANES = 128
SUBLANES = 8
ROPE_HALF = HEAD_DIM // 2
LOG2E = math.log2(math.e)
Q_SCALE = LOG2E / math.sqrt(HEAD_DIM)
VMEM_LIMIT = 56 * 1024 * 1024
MOD_COLS_TILE = 1536
TOKEN_TILE = 512
POST_SUB_TILE = 256
PRE_TILE = 1024
PRE_SUB_TILE = 256
QUERY_TILE = 1024
QUERY_SUB = LANES
ITEM_CHUNKS = 2
KEY_BLOCK = 256
VALUE_GROUP = 2048
VT_ONES_ROWS = 16
VT_CHUNK = LANES + VT_ONES_ROWS

F32 = jnp.float32
BF16 = jnp.bfloat16


def _lane_slot_dim():
    quarter = HEAD_DIM // 4
    lane = np.arange(LANES)
    part = lane // ROPE_HALF
    i = lane % ROPE_HALF
    dim = np.where(i < quarter, i, ROPE_HALF + (i - quarter)) + np.where(part >= 2, quarter, 0)
    return part % 2, dim


def _chunk_heads(c):
    head_a = (c // GQA_GROUP) * 2 * GQA_GROUP + c % GQA_GROUP
    return head_a, head_a + GQA_GROUP


def _q_cols(n_heads):
    slot, dim = _lane_slot_dim()
    cols = []
    for c in range(n_heads // 2):
        head_a, head_b = _chunk_heads(c)
        cols.append(np.where(slot == 0, head_a, head_b) * HEAD_DIM + dim)
    return np.concatenate(cols)


def _k_cols(n_kv, base):
    slot, dim = _lane_slot_dim()
    return np.concatenate([base + (2 * m + slot) * HEAD_DIM + dim for m in range(n_kv // 2)])


def _head_mean_matrix():
    slot, _ = _lane_slot_dim()
    mean = (slot[:, None] == slot[None, :]).astype(np.float32) / HEAD_DIM
    return np.concatenate([mean, mean], axis=0)


def _rope_tables(seq):
    quarter = HEAD_DIM // 4
    freqs = ROPE_THETA ** (-jnp.arange(quarter, dtype=F32) / quarter)
    t = jnp.arange(seq, dtype=jnp.int32)
    rows = (t // GRID_W).astype(F32)[:, None] * freqs[None, :]
    cols = (t % GRID_W).astype(F32)[:, None] * freqs[None, :]
    ang = jnp.concatenate([rows, cols], axis=-1)
    cos = jnp.tile(jnp.cos(ang), (1, 4))
    sin = jnp.sin(ang)
    sin = jnp.concatenate([-sin, -sin, sin, sin], axis=-1)
    return cos, sin


def _rms_normalise(x):
    return x * lax.rsqrt(jnp.mean(x * x, axis=-1, keepdims=True) + EPS)


def _silu(x):
    return x * (1.0 / (1.0 + jnp.exp(-x)))


def _dot(a, b):
    return jnp.dot(a, b, preferred_element_type=F32)


def _dot_tn(a_t, b):
    return lax.dot_general(a_t, b, (((0,), (0,)), ((), ())), preferred_element_type=F32)


def _const_spec(shape):
    return pl.BlockSpec(shape, lambda *_: (0,) * len(shape), pipeline_mode=pl.Buffered(1))


def _params():
    return pltpu.CompilerParams(dimension_semantics=("parallel", "parallel"), vmem_limit_bytes=VMEM_LIMIT)


def _mod_kernel(c_ref, w_ref, b_ref, o_ref):
    s = _silu(c_ref[...]).astype(BF16)
    o_ref[0] = _dot(s, w_ref[0].astype(BF16)) + b_ref[0]


def _modulation(cc, w_mod, b_mod):
    depth, d, n = w_mod.shape
    rows = cc.shape[0]
    tn = MOD_COLS_TILE
    return pl.pallas_call(
        _mod_kernel,
        grid=(depth, n // tn),
        in_specs=[
            pl.BlockSpec((rows, d), lambda l, j: (0, 0)),
            pl.BlockSpec((1, d, tn), lambda l, j: (l, 0, j)),
            pl.BlockSpec((1, 1, tn), lambda l, j: (l, 0, j)),
        ],
        out_specs=pl.BlockSpec((1, rows, tn), lambda l, j: (l, 0, j)),
        out_shape=jax.ShapeDtypeStruct((depth, rows, n), F32),
        compiler_params=_params(),
        name="modulation",
    )(cc, w_mod, b_mod.reshape(depth, 1, n))


def _pre_kernel(*refs, n_q, n_k, n_v, n_u, qk_norm, rope):
    x_ref, mod_ref, gains_ref, w_ref = refs[:4]
    pos = 4
    if qk_norm:
        pm_ref, qg_ref, kg_ref = refs[pos:pos + 3]
        pos += 3
    if rope:
        cos_ref, sin_ref = refs[pos:pos + 2]
        pos += 2
    qt_ref, k_ref, vt_ref = refs[pos:pos + 3]
    u_ref = refs[pos + 3] if n_u else None

    def modulated(rows):
        h = _rms_normalise(x_ref[0, rows, :]) * gains_ref[0:1, :]
        return (h * (1.0 + mod_ref[0, 1:2, :]) + mod_ref[0, 0:1, :]).astype(BF16)

    def head_chunk(c, gain_ref, rows):
        if qk_norm:
            c2 = c * c
            hi = c2.astype(BF16)
            lo = (c2 - hi.astype(F32)).astype(BF16)
            ms = _dot(jnp.concatenate([hi, lo], axis=1), pm_ref[...])
            c = c * lax.rsqrt(ms + EPS) * gain_ref[...]
        if rope:
            c = c * cos_ref[rows, :] + pltpu.roll(c, LANES // 2, 1) * sin_ref[rows, :]
        return c

    def project(rows, h):
        p = _dot(h, w_ref[...])
        for j in range(n_q // LANES):
            sl = slice(j * LANES, (j + 1) * LANES)
            c = head_chunk(p[:, sl], qg_ref if qk_norm else None, rows) * Q_SCALE
            qt_ref[0, sl, rows] = c.T.astype(BF16)
        for j in range(n_k // LANES):
            sl = slice(j * LANES, (j + 1) * LANES)
            c = head_chunk(p[:, n_q + j * LANES:n_q + (j + 1) * LANES], kg_ref if qk_norm else None, rows)
            k_ref[0, rows, sl] = c.astype(BF16)
        ones = jnp.ones((VT_ONES_ROWS, p.shape[0]), BF16)
        for j in range(n_v // LANES):
            r0 = j * VT_CHUNK
            vt_ref[0, r0:r0 + LANES, rows] = (
                p[:, n_q + n_k + j * LANES:n_q + n_k + (j + 1) * LANES].T.astype(BF16))
            vt_ref[0, r0 + LANES:r0 + VT_CHUNK, rows] = ones
        if n_u:
            u_ref[0, rows, :] = p[:, n_q + n_k + n_v:]

    tm = x_ref.shape[1]
    step = min(tm, PRE_SUB_TILE)
    parts = [slice(r, r + step) for r in range(0, tm, step)]
    h_next = modulated(parts[0])
    for i, rows in enumerate(parts):
        h_cur = h_next
        if i + 1 < len(parts):
            h_next = modulated(parts[i + 1])
        project(rows, h_cur)


def _pre(x, mod, gains, w, norm_args, rope_args, *, widths, tm):
    n_q, n_k, n_v, n_u = widths
    b, s, d = x.shape
    per_batch_mod = mod.shape[0] > 1
    qk_norm = norm_args is not None
    rope = rope_args is not None
    in_specs = [
        pl.BlockSpec((1, tm, d), lambda i, t: (i, t, 0)),
        pl.BlockSpec((1, SUBLANES, d), (lambda i, t: (i, 0, 0)) if per_batch_mod else (lambda i, t: (0, 0, 0))),
        _const_spec(gains.shape),
        _const_spec(w.shape),
    ]
    args = [x, mod, gains, w]
    if qk_norm:
        in_specs += [_const_spec(a.shape) for a in norm_args]
        args += list(norm_args)
    if rope:
        in_specs += [pl.BlockSpec((tm, LANES), lambda i, t: (t, 0))] * 2
        args += list(rope_args)
    vt_rows = n_v // LANES * VT_CHUNK
    out_shape = [jax.ShapeDtypeStruct((b, n_q, s), BF16), jax.ShapeDtypeStruct((b, s, n_k), BF16),
                 jax.ShapeDtypeStruct((b, vt_rows, s), BF16)]
    out_specs = [pl.BlockSpec((1, n_q, tm), lambda i, t: (i, 0, t)),
                 pl.BlockSpec((1, tm, n_k), lambda i, t: (i, t, 0)),
                 pl.BlockSpec((1, vt_rows, tm), lambda i, t: (i, 0, t))]
    if n_u:
        out_shape.append(jax.ShapeDtypeStruct((b, s, n_u), F32))
        out_specs.append(pl.BlockSpec((1, tm, n_u), lambda i, t: (i, t, 0)))
    return pl.pallas_call(
        functools.partial(_pre_kernel, n_q=n_q, n_k=n_k, n_v=n_v, n_u=n_u, qk_norm=qk_norm, rope=rope),
        grid=(b, s // tm),
        in_specs=in_specs,
        out_specs=out_specs,
        out_shape=out_shape,
        compiler_params=_params(),
        name="pre_even" if qk_norm else "pre_odd",
    )(*args)


def _attn_kernel(*refs, mode, n_pairs, n_sub, tq, seq, use_sink, cpi):
    qt_ref = refs[0]
    pos = 1
    if mode != "ctx":
        k_ref, vt_ref = refs[pos:pos + 2]
        pos += 2
    kc_ref, vct_ref = refs[pos:pos + 2]
    pos += 2
    if use_sink:
        sink_ref = refs[pos]
        pos += 1
    o_ref, s_ref, p_ref = refs[pos:pos + 3]

    sub = QUERY_SUB
    row = lax.broadcasted_iota(jnp.int32, (LANES, 1), 0)
    slot_a = (row // ROPE_HALF) % 2 == 0
    band = sub + 2 * WINDOW
    if mode == "window":
        rel = (lax.broadcasted_iota(jnp.int32, (band, sub), 0)
               - lax.broadcasted_iota(jnp.int32, (band, sub), 1))

    def block_geometry(sb):
        q0 = pl.program_id(1) * tq + sb * sub
        start = pl.multiple_of(jnp.clip(q0 - WINDOW, 0, seq - band), LANES)
        bias = jnp.where(jnp.abs(rel + (start - q0)) <= WINDOW, 0.0, -jnp.inf).astype(F32)
        return start, jnp.concatenate([bias] * (2 * cpi), axis=1)

    c_len = kc_ref.shape[1]
    first = {"dense": seq, "window": band, "ctx": 0}[mode]
    blocks = [(True, r, min(KEY_BLOCK, first - r), r) for r in range(0, first, KEY_BLOCK)]
    blocks += [(False, r, min(KEY_BLOCK, c_len - r), first + r) for r in range(0, c_len, KEY_BLOCK)]
    value_groups = {}
    for bi, (latent, r0, rows, b0) in enumerate(blocks):
        if r0 % VALUE_GROUP:
            _, g0, g_rows, gb0 = value_groups.pop(bi - 1)
            value_groups[bi] = (latent, g0, g_rows + rows, gb0)
        else:
            value_groups[bi] = (latent, r0, rows, b0)

    def item_chunks(item):
        return range(cpi * item, cpi * (item + 1))

    def item_pair(item):
        return cpi * item // GQA_GROUP

    def item_heads(item):
        return sum((_chunk_heads(c) for c in item_chunks(item)), ())

    def query_operand(sb, item):
        cols = []
        for c in item_chunks(item):
            qt = qt_ref[0, c * LANES:(c + 1) * LANES, sb * sub:(sb + 1) * sub]
            zero = jnp.zeros_like(qt)
            cols += [jnp.where(slot_a, qt, zero), jnp.where(slot_a, zero, qt)]
        return jnp.concatenate(cols, axis=1)

    def score_block(slot, item, geometry, rhs, blk, m):
        latent, r0, rows, b0 = blk
        psl = slice(item_pair(item) * LANES, (item_pair(item) + 1) * LANES)
        if not latent:
            s = _dot(kc_ref[0, r0:r0 + rows, psl], rhs)
        elif mode == "dense":
            s = _dot(k_ref[0, r0:r0 + rows, psl], rhs)
        else:
            start, bias = geometry
            s = _dot(k_ref[0, pl.ds(pl.multiple_of(start + r0, LANES), rows), psl], rhs) + bias[r0:r0 + rows]
        s_ref[slot, b0:b0 + rows, :] = s
        top = s.max(axis=0, keepdims=True)
        return top if m is None else jnp.maximum(m, top)

    def exp_block(slot, blk, m):
        _, _, rows, b0 = blk
        p_ref[slot, b0:b0 + rows, :] = jnp.exp2(s_ref[slot, b0:b0 + rows, :] - m).astype(BF16)

    def value_block(slot, item, geometry, blk, acc):
        latent, r0, rows, b0 = blk
        pair = item_pair(item)
        vsl = slice(pair * VT_CHUNK, (pair + 1) * VT_CHUNK)
        if not latent:
            v_t = vct_ref[0, vsl, r0:r0 + rows]
        elif mode == "dense":
            v_t = vt_ref[0, vsl, r0:r0 + rows]
        else:
            v_t = vt_ref[0, vsl, pl.ds(pl.multiple_of(geometry[0] + r0, LANES), rows)]
        pv = _dot(v_t, p_ref[slot, b0:b0 + rows, :])
        return pv if acc is None else acc + pv

    def with_sink(item, m):
        if not use_sink:
            return m, None
        sink = jnp.concatenate([jnp.full((1, sub), sink_ref[h], F32) for h in item_heads(item)], axis=1) * LOG2E
        m = jnp.maximum(m, sink)
        return m, jnp.exp2(sink - m)

    def store_output(sb, item, o4, sink_term):
        denom = o4[LANES:LANES + 1, :]
        if use_sink:
            denom = denom + sink_term
        o4 = o4[:LANES, :] * (1.0 / denom)
        for j, h in enumerate(item_heads(item)):
            rows = slice(0, HEAD_DIM) if j % 2 == 0 else slice(HEAD_DIM, 2 * HEAD_DIM)
            o_ref[0, h * HEAD_DIM:(h + 1) * HEAD_DIM, sb * sub:(sb + 1) * sub] = (
                o4[rows, j * sub:(j + 1) * sub].astype(BF16))

    items = [(sb, it) for sb in range(n_sub) for it in range(GQA_GROUP * n_pairs // cpi)]
    geometries = [block_geometry(sb) if mode == "window" else None for sb in range(n_sub)]
    n_items = len(items)
    m_cur = sink_cur = sink_prev = None
    for i in range(-1, n_items + 1):
        nxt, prv = i + 1, i - 1
        if nxt < n_items:
            rhs = query_operand(*items[nxt])
        m_next = o_acc = None
        for bi, blk in enumerate(blocks):
            if nxt < n_items:
                m_next = score_block(nxt % 2, items[nxt][1], geometries[items[nxt][0]], rhs, blk, m_next)
            if 0 <= i < n_items:
                exp_block(i % 2, blk, m_cur)
            if 0 <= prv and bi in value_groups:
                o_acc = value_block(prv % 2, items[prv][1], geometries[items[prv][0]], value_groups[bi], o_acc)
        if 0 <= prv:
            store_output(*items[prv], o_acc, sink_prev)
        sink_prev = sink_cur
        if nxt < n_items:
            m_cur, sink_cur = with_sink(items[nxt][1], m_next)


def _attention(qt, kv, kv_ctx, sink, *, mode, tq):
    b, wq, sq = qt.shape
    n_pairs = wq // (2 * GQA_GROUP * HEAD_DIM)
    kc, vct = kv_ctx
    c_len = kc.shape[1]
    use_sink = sink is not None
    in_specs = [pl.BlockSpec((1, wq, tq), lambda i, t: (i, 0, t))]
    args = [qt]
    seq = sq
    if mode != "ctx":
        k, vt = kv
        seq = k.shape[1]
        in_specs += [pl.BlockSpec((1, seq, k.shape[2]), lambda i, t: (i, 0, 0)),
                     pl.BlockSpec((1, vt.shape[1], seq), lambda i, t: (i, 0, 0))]
        args += [k, vt]
    in_specs += [pl.BlockSpec((1, c_len, kc.shape[2]), lambda i, t: (i, 0, 0)),
                 pl.BlockSpec((1, vct.shape[1], c_len), lambda i, t: (i, 0, 0))]
    args += [kc, vct]
    if use_sink:
        in_specs.append(pl.BlockSpec(memory_space=pltpu.SMEM))
        args.append(sink)
    n_keys = c_len + {"dense": seq, "window": QUERY_SUB + 2 * WINDOW, "ctx": 0}[mode]
    cpi = ITEM_CHUNKS
    return pl.pallas_call(
        functools.partial(_attn_kernel, mode=mode, n_pairs=n_pairs, n_sub=tq // QUERY_SUB, tq=tq, seq=seq,
                          use_sink=use_sink, cpi=cpi),
        grid=(b, sq // tq),
        in_specs=in_specs,
        out_specs=pl.BlockSpec((1, wq, tq), lambda i, t: (i, 0, t)),
        out_shape=jax.ShapeDtypeStruct((b, wq, sq), BF16),
        scratch_shapes=[pltpu.VMEM((2, n_keys, 2 * cpi * QUERY_SUB), F32),
                        pltpu.VMEM((2, n_keys, 2 * cpi * QUERY_SUB), BF16)],
        compiler_params=_params(),
        name="attn_" + mode,
    )(*args)


def _post_kernel(*refs, pool, tm, seq):
    x_ref, at_ref, mod_ref, gains_ref, wo_ref, wfi_ref, wfo_ref = refs[:7]
    pos = 7
    if pool:
        u_ref, uprev_ref, unext_ref, wpool_ref, pscale_ref = refs[pos:pos + 5]
        pos += 5
    o_ref = refs[pos]
    hid_ref = refs[pos + 1]
    if pool:
        ext_ref = refs[pos + 2]

    n_a = at_ref.shape[1]
    if pool:
        t = pl.program_id(1)
        last = pl.num_programs(1) - 1
        halo = jnp.zeros((POOL_HALO, ext_ref.shape[1]), F32)
        ext_ref[0:POOL_HALO, :] = jnp.where(t > 0, uprev_ref[0], halo)
        ext_ref[POOL_HALO:POOL_HALO + tm, :] = u_ref[0]
        ext_ref[POOL_HALO + tm:, :] = jnp.where(t < last, unext_ref[0], halo)

    def pool_mix(r0, rows):
        n_ext = rows + 2 * POOL_HALO
        tok = t * tm + r0 + lax.broadcasted_iota(jnp.int32, (rows, 1), 0)

        def ahead(v, k):
            return pltpu.roll(v, n_ext - k, 0)

        def behind(v, k):
            return pltpu.roll(v, k, 0)

        mixed = []
        for g, w in enumerate(POOL_WINDOWS):
            sl = slice(g * LANES, (g + 1) * LANES)
            e = ext_ref[r0:r0 + n_ext, sl]
            run, length = e, 1
            while length < w // 2:
                run, length = run + ahead(run, length), 2 * length
            total = run + behind(run, w // 2)
            total = total[POOL_HALO:POOL_HALO + rows]
            cnt = jnp.minimum(tok + (w - w // 2), seq) - jnp.maximum(tok - w // 2, 0)
            diff = total / cnt.astype(F32) - e[POOL_HALO:POOL_HALO + rows]
            mixed.append((_dot(diff.astype(BF16), wpool_ref[g]) * pscale_ref[:, sl]).astype(BF16))
        return jnp.concatenate(mixed, axis=-1)

    def attn_out(r0, rows):
        return _dot_tn(at_ref[0, :, r0:r0 + rows], wo_ref[0:n_a, :])

    def pool_out(r0, rows, y):
        return y + _dot(pool_mix(r0, rows), wo_ref[n_a:, :]) if pool else y

    def residual_and_norm(r0, rows, y):
        x1 = x_ref[0, r0:r0 + rows, :] + mod_ref[0, 2:3, :] * (_rms_normalise(y) * gains_ref[1:2, :])
        h = _rms_normalise(x1) * gains_ref[2:3, :]
        return x1, (h * (1.0 + mod_ref[0, 4:5, :]) + mod_ref[0, 3:4, :]).astype(BF16)

    def ffn_hidden(r0, rows, h):
        for i in range(FFN_HIDDEN // FFN_CHUNK):
            gate = _dot(h, wfi_ref[:, i * FFN_CHUNK:(i + 1) * FFN_CHUNK])
            up = _dot(h, wfi_ref[:, FFN_HIDDEN + i * FFN_CHUNK:FFN_HIDDEN + (i + 1) * FFN_CHUNK])
            hid_ref[r0:r0 + rows, i * FFN_CHUNK:(i + 1) * FFN_CHUNK] = (_silu(gate) * up).astype(BF16)

    def ffn_out(r0, rows, x1):
        z = _dot(hid_ref[r0:r0 + rows, :], wfo_ref[...])
        o_ref[0, r0:r0 + rows, :] = x1 + mod_ref[0, 5:6, :] * (_rms_normalise(z) * gains_ref[3:4, :])

    rows = min(tm // 2, POST_SUB_TILE)
    starts = list(range(0, tm, rows))
    n = len(starts)
    y = {p: pool_out(starts[p], rows, attn_out(starts[p], rows)) for p in range(min(2, n))}
    normed = {0: residual_and_norm(starts[0], rows, y.pop(0))}
    for p in range(n):
        x1, h = normed.pop(p)
        ffn_hidden(starts[p], rows, h)
        if p + 2 < n:
            y[p + 2] = pool_out(starts[p + 2], rows, attn_out(starts[p + 2], rows))
        if p + 1 < n:
            normed[p + 1] = residual_and_norm(starts[p + 1], rows, y.pop(p + 1))
        ffn_out(starts[p], rows, x1)


def _post(x, a_t, mod, gains, wo, wfi, wfo, pool_args, *, tm):
    b, s, d = x.shape
    per_batch_mod = mod.shape[0] > 1
    pool = pool_args is not None
    in_specs = [
        pl.BlockSpec((1, tm, d), lambda i, t: (i, t, 0)),
        pl.BlockSpec((1, a_t.shape[1], tm), lambda i, t: (i, 0, t)),
        pl.BlockSpec((1, SUBLANES, d), (lambda i, t: (i, 0, 0)) if per_batch_mod else (lambda i, t: (0, 0, 0))),
        _const_spec(gains.shape), _const_spec(wo.shape), _const_spec(wfi.shape), _const_spec(wfo.shape),
    ]
    args = [x, a_t, mod, gains, wo, wfi, wfo]
    scratch = [pltpu.VMEM((tm, FFN_HIDDEN), BF16)]
    if pool:
        u, wpool, pscale = pool_args
        nu = u.shape[2]
        per_tile = tm // POOL_HALO
        n_halo_blocks = s // POOL_HALO
        in_specs += [
            pl.BlockSpec((1, tm, nu), lambda i, t: (i, t, 0)),
            pl.BlockSpec((1, POOL_HALO, nu), lambda i, t: (i, jnp.maximum(t * per_tile - 1, 0), 0)),
            pl.BlockSpec((1, POOL_HALO, nu),
                         lambda i, t: (i, jnp.minimum((t + 1) * per_tile, n_halo_blocks - 1), 0)),
            _const_spec(wpool.shape), _const_spec(pscale.shape),
        ]
        args += [u, u, u, wpool, pscale]
        scratch.append(pltpu.VMEM((tm + 2 * POOL_HALO, nu), F32))
    return pl.pallas_call(
        functools.partial(_post_kernel, pool=pool, tm=tm, seq=s),
        grid=(b, s // tm),
        in_specs=in_specs,
        out_specs=pl.BlockSpec((1, tm, d), lambda i, t: (i, t, 0)),
        out_shape=jax.ShapeDtypeStruct((b, s, d), F32),
        scratch_shapes=scratch,
        compiler_params=_params(),
        name="post_even" if pool else "post_odd",
    )(*args)


def _pad_rows(a, rows):
    return jnp.pad(a, [(0, 0)] * (a.ndim - 2) + [(0, rows - a.shape[-2]), (0, 0)])


def kernel(x, c, ctx, c_ctx, w_mod, b_mod, g_pre_mix, g_post_mix, g_pre_ffn, g_post_ffn, we_in, we_out,
           we_q_gain, we_k_gain, we_pool, we_pool_scale, wo_in, wo_out, wo_sink, w_ffn_in, w_ffn_out):
    batch, seq, d = x.shape
    c_len = ctx.shape[1]
    depth = w_mod.shape[0]
    tm = TOKEN_TILE
    tm_ctx = c_len
    assert d == D_MODEL and w_ffn_out.shape[1] == FFN_HIDDEN and w_mod.shape[2] % MOD_COLS_TILE == 0
    assert seq % PRE_TILE == 0 and seq % TOKEN_TILE == 0 and seq % QUERY_TILE == 0 and seq % GRID_W == 0
    assert seq >= QUERY_SUB + 2 * WINDOW and c_len % PRE_SUB_TILE == 0 and c_len % QUERY_SUB == 0
    assert max(POOL_WINDOWS) // 2 <= POOL_HALO

    cc = jnp.concatenate([c, c_ctx[None, :]], axis=0)
    cc = _pad_rows(cc, -(-(batch + 1) // SUBLANES) * SUBLANES)
    mods = _modulation(cc, w_mod, b_mod)

    cos, sin = _rope_tables(seq)
    _, lane_dim = _lane_slot_dim()
    head_mean = jnp.asarray(_head_mean_matrix(), BF16)

    for l in range(depth):
        even = l % 2 == 0
        i = l // 2
        with_ctx = l < depth - 1
        mod_x = _pad_rows(mods[l, :batch].reshape(batch, 6, d), SUBLANES)
        mod_c = _pad_rows(mods[l, batch:batch + 1].reshape(1, 6, d), SUBLANES)
        gains = _pad_rows(jnp.stack([g_pre_mix[l], g_post_mix[l], g_pre_ffn[l], g_post_ffn[l]]), SUBLANES)
        wfi = w_ffn_in[l].astype(BF16)
        wfo = w_ffn_out[l].astype(BF16)

        if even:
            n_heads, n_kv = 8, 2
            w_full, w_out = we_in[i], we_out[i].astype(BF16)
            norm_args = (head_mean, we_q_gain[i][lane_dim][None, :], we_k_gain[i][lane_dim][None, :])
            sink = None
            mode = "dense"
        else:
            n_heads, n_kv = 16, 4
            w_full, w_out = wo_in[i], wo_out[i].astype(BF16)
            norm_args = None
            sink = wo_sink[i]
            mode = "window"
        q_w, kv_w = n_heads * HEAD_DIM, n_kv * HEAD_DIM
        cols = np.concatenate([_q_cols(n_heads), _k_cols(n_kv, q_w), np.arange(q_w + kv_w, w_full.shape[1])])
        w_in = w_full[:, cols].astype(BF16)
        widths = (q_w, kv_w, kv_w, w_full.shape[1] - q_w - 2 * kv_w)

        lat = _pre(x, mod_x, gains, w_in, norm_args, (cos, sin), widths=widths, tm=PRE_TILE)
        con = _pre(ctx, mod_c, gains, w_in, norm_args, None, widths=widths, tm=tm_ctx)
        a_t = _attention(lat[0], (lat[1], lat[2]), (con[1], con[2]), sink, mode=mode, tq=QUERY_TILE)
        pool_w = (we_pool[i].astype(BF16), we_pool_scale[i][None, :]) if even else None
        x = _post(x, a_t, mod_x, gains, w_out, wfi, wfo, (lat[3],) + pool_w if even else None, tm=tm)
        if with_ctx:
            ac_t = _attention(con[0], None, (con[1], con[2]), sink, mode="ctx", tq=c_len)
            ctx = _post(ctx, ac_t, mod_c, gains, w_out, wfi, wfo,
                        (con[3],) + pool_w if even else None, tm=tm_ctx)
    return x
```

```python
import functools
import math

import numpy as np
import jax
import jax.numpy as jnp
from jax import lax
from jax.experimental import pallas as pl
from jax.experimental.pallas import tpu as pltpu

D_MODEL = 1024
HEAD_DIM = 64
GQA_GROUP = 4
GRID_W = 64
ROPE_THETA = 10000.0
EPS = 1e-6
WINDOW = 128
POOL_WINDOWS = (2, 4, 8, 16)
POOL_HALO = 8
FFN_HIDDEN = 2816
FFN_CHUNK = 256
LANES = 128
SUBLANES = 8
ROPE_HALF = HEAD_DIM // 2
LOG2E = math.log2(math.e)
Q_SCALE = LOG2E / math.sqrt(HEAD_DIM)
VMEM_LIMIT = 56 * 1024 * 1024
MOD_COLS_TILE = 1536
TOKEN_TILE = 512
POST_SUB_TILE = 256
PRE_TILE = 1024
PRE_SUB_TILE = 256
QUERY_TILE = 1024
QUERY_SUB = LANES
ITEM_CHUNKS = 2
KEY_BLOCK = 256
VALUE_GROUP = 512
VT_ONES_ROWS = 16
VT_CHUNK = LANES + VT_ONES_ROWS

F32 = jnp.float32
BF16 = jnp.bfloat16


def _lane_slot_dim():
    quarter = HEAD_DIM // 4
    lane = np.arange(LANES)
    part = lane // ROPE_HALF
    i = lane % ROPE_HALF
    dim = np.where(i < quarter, i, ROPE_HALF + (i - quarter)) + np.where(part >= 2, quarter, 0)
    return part % 2, dim


def _chunk_heads(c):
    head_a = (c // GQA_GROUP) * 2 * GQA_GROUP + c % GQA_GROUP
    return head_a, head_a + GQA_GROUP


def _q_cols(n_heads):
    slot, dim = _lane_slot_dim()
    cols = []
    for c in range(n_heads // 2):
        head_a, head_b = _chunk_heads(c)
        cols.append(np.where(slot == 0, head_a, head_b) * HEAD_DIM + dim)
    return np.concatenate(cols)


def _k_cols(n_kv, base):
    slot, dim = _lane_slot_dim()
    return np.concatenate([base + (2 * m + slot) * HEAD_DIM + dim for m in range(n_kv // 2)])


def _head_mean_matrix():
    slot, _ = _lane_slot_dim()
    mean = (slot[:, None] == slot[None, :]).astype(np.float32) / HEAD_DIM
    return np.concatenate([mean, mean], axis=0)


def _rope_tables(seq):
    quarter = HEAD_DIM // 4
    freqs = ROPE_THETA ** (-jnp.arange(quarter, dtype=F32) / quarter)
    t = jnp.arange(seq, dtype=jnp.int32)
    rows = (t // GRID_W).astype(F32)[:, None] * freqs[None, :]
    cols = (t % GRID_W).astype(F32)[:, None] * freqs[None, :]
    ang = jnp.concatenate([rows, cols], axis=-1)
    cos = jnp.tile(jnp.cos(ang), (1, 4))
    sin = jnp.sin(ang)
    sin = jnp.concatenate([-sin, -sin, sin, sin], axis=-1)
    return cos, sin


def _rms_normalise(x):
    return x * lax.rsqrt(jnp.mean(x * x, axis=-1, keepdims=True) + EPS)


def _silu(x):
    return x * (1.0 / (1.0 + jnp.exp(-x)))


def _dot(a, b):
    return jnp.dot(a, b, preferred_element_type=F32)


def _dot_tn(a_t, b):
    return lax.dot_general(a_t, b, (((0,), (0,)), ((), ())), preferred_element_type=F32)


def _const_spec(shape):
    return pl.BlockSpec(shape, lambda *_: (0,) * len(shape), pipeline_mode=pl.Buffered(1))


def _params():
    return pltpu.CompilerParams(dimension_semantics=("parallel", "parallel"), vmem_limit_bytes=VMEM_LIMIT)


def _mod_kernel(c_ref, w_ref, b_ref, o_ref):
    s = _silu(c_ref[...]).astype(BF16)
    o_ref[0] = _dot(s, w_ref[0].astype(BF16)) + b_ref[0]


def _modulation(cc, w_mod, b_mod):
    depth, d, n = w_mod.shape
    rows = cc.shape[0]
    tn = MOD_COLS_TILE
    return pl.pallas_call(
        _mod_kernel,
        grid=(depth, n // tn),
        in_specs=[
            pl.BlockSpec((rows, d), lambda l, j: (0, 0)),
            pl.BlockSpec((1, d, tn), lambda l, j: (l, 0, j)),
            pl.BlockSpec((1, 1, tn), lambda l, j: (l, 0, j)),
        ],
        out_specs=pl.BlockSpec((1, rows, tn), lambda l, j: (l, 0, j)),
        out_shape=jax.ShapeDtypeStruct((depth, rows, n), F32),
        compiler_params=_params(),
        name="modulation",
    )(cc, w_mod, b_mod.reshape(depth, 1, n))


def _pre_kernel(*refs, n_q, n_k, n_v, n_u, qk_norm, rope):
    x_ref, mod_ref, gains_ref, w_ref = refs[:4]
    pos = 4
    if qk_norm:
        pm_ref, qg_ref, kg_ref = refs[pos:pos + 3]
        pos += 3
    if rope:
        cos_ref, sin_ref = refs[pos:pos + 2]
        pos += 2
    qt_ref, k_ref, vt_ref = refs[pos:pos + 3]
    u_ref = refs[pos + 3] if n_u else None

    def modulated(rows):
        h = _rms_normalise(x_ref[0, rows, :]) * gains_ref[0:1, :]
        return (h * (1.0 + mod_ref[0, 1:2, :]) + mod_ref[0, 0:1, :]).astype(BF16)

    def head_chunk(c, gain_ref, rows):
        if qk_norm:
            c2 = c * c
            hi = c2.astype(BF16)
            lo = (c2 - hi.astype(F32)).astype(BF16)
            ms = _dot(jnp.concatenate([hi, lo], axis=1), pm_ref[...])
            c = c * lax.rsqrt(ms + EPS) * gain_ref[...]
        if rope:
            c = c * cos_ref[rows, :] + pltpu.roll(c, LANES // 2, 1) * sin_ref[rows, :]
        return c

    def project(rows, h):
        p = _dot(h, w_ref[...])
        for j in range(n_q // LANES):
            sl = slice(j * LANES, (j + 1) * LANES)
            c = head_chunk(p[:, sl], qg_ref if qk_norm else None, rows) * Q_SCALE
            qt_ref[0, sl, rows] = c.T.astype(BF16)
        for j in range(n_k // LANES):
            sl = slice(j * LANES, (j + 1) * LANES)
            c = head_chunk(p[:, n_q + j * LANES:n_q + (j + 1) * LANES], kg_ref if qk_norm else None, rows)
            k_ref[0, rows, sl] = c.astype(BF16)
        ones = jnp.ones((VT_ONES_ROWS, p.shape[0]), BF16)
        for j in range(n_v // LANES):
            r0 = j * VT_CHUNK
            vt_ref[0, r0:r0 + LANES, rows] = (
                p[:, n_q + n_k + j * LANES:n_q + n_k + (j + 1) * LANES].T.astype(BF16))
            vt_ref[0, r0 + LANES:r0 + VT_CHUNK, rows] = ones
        if n_u:
            u_ref[0, rows, :] = p[:, n_q + n_k + n_v:]

    tm = x_ref.shape[1]
    step = min(tm, PRE_SUB_TILE)
    parts = [slice(r, r + step) for r in range(0, tm, step)]
    h_next = modulated(parts[0])
    for i, rows in enumerate(parts):
        h_cur = h_next
        if i + 1 < len(parts):
            h_next = modulated(parts[i + 1])
        project(rows, h_cur)


def _pre(x, mod, gains, w, norm_args, rope_args, *, widths, tm):
    n_q, n_k, n_v, n_u = widths
    b, s, d = x.shape
    per_batch_mod = mod.shape[0] > 1
    qk_norm = norm_args is not None
    rope = rope_args is not None
    in_specs = [
        pl.BlockSpec((1, tm, d), lambda i, t: (i, t, 0)),
        pl.BlockSpec((1, SUBLANES, d), (lambda i, t: (i, 0, 0)) if per_batch_mod else (lambda i, t: (0, 0, 0))),
        _const_spec(gains.shape),
        _const_spec(w.shape),
    ]
    args = [x, mod, gains, w]
    if qk_norm:
        in_specs += [_const_spec(a.shape) for a in norm_args]
        args += list(norm_args)
    if rope:
        in_specs += [pl.BlockSpec((tm, LANES), lambda i, t: (t, 0))] * 2
        args += list(rope_args)
    vt_rows = n_v // LANES * VT_CHUNK
    out_shape = [jax.ShapeDtypeStruct((b, n_q, s), BF16), jax.ShapeDtypeStruct((b, s, n_k), BF16),
                 jax.ShapeDtypeStruct((b, vt_rows, s), BF16)]
    out_specs = [pl.BlockSpec((1, n_q, tm), lambda i, t: (i, 0, t)),
                 pl.BlockSpec((1, tm, n_k), lambda i, t: (i, t, 0)),
                 pl.BlockSpec((1, vt_rows, tm), lambda i, t: (i, 0, t))]
    if n_u:
        out_shape.append(jax.ShapeDtypeStruct((b, s, n_u), F32))
        out_specs.append(pl.BlockSpec((1, tm, n_u), lambda i, t: (i, t, 0)))
    return pl.pallas_call(
        functools.partial(_pre_kernel, n_q=n_q, n_k=n_k, n_v=n_v, n_u=n_u, qk_norm=qk_norm, rope=rope),
        grid=(b, s // tm),
        in_specs=in_specs,
        out_specs=out_specs,
        out_shape=out_shape,
        compiler_params=_params(),
        name="pre_even" if qk_norm else "pre_odd",
    )(*args)


def _attn_kernel(*refs, mode, n_pairs, n_sub, tq, seq, use_sink, cpi):
    qt_ref = refs[0]
    pos = 1
    if mode != "ctx":
        k_ref, vt_ref = refs[pos:pos + 2]
        pos += 2
    kc_ref, vct_ref = refs[pos:pos + 2]
    pos += 2
    if use_sink:
        sink_ref = refs[pos]
        pos += 1
    o_ref, s_ref, p_ref = refs[pos:pos + 3]

    sub = QUERY_SUB
    row = lax.broadcasted_iota(jnp.int32, (LANES, 1), 0)
    slot_a = (row // ROPE_HALF) % 2 == 0
    band = sub + 2 * WINDOW
    if mode == "window":
        rel = (lax.broadcasted_iota(jnp.int32, (band, sub), 0)
               - lax.broadcasted_iota(jnp.int32, (band, sub), 1))

    def block_geometry(sb):
        q0 = pl.program_id(1) * tq + sb * sub
        start = pl.multiple_of(jnp.clip(q0 - WINDOW, 0, seq - band), LANES)
        bias = jnp.where(jnp.abs(rel + (start - q0)) <= WINDOW, 0.0, -jnp.inf).astype(F32)
        return start, jnp.concatenate([bias] * (2 * cpi), axis=1)

    c_len = kc_ref.shape[1]
    first = {"dense": seq, "window": band, "ctx": 0}[mode]
    blocks = [(True, r, min(KEY_BLOCK, first - r), r) for r in range(0, first, KEY_BLOCK)]
    blocks += [(False, r, min(KEY_BLOCK, c_len - r), first + r) for r in range(0, c_len, KEY_BLOCK)]
    value_groups = {}
    for bi, (latent, r0, rows, b0) in enumerate(blocks):
        if r0 % VALUE_GROUP:
            _, g0, g_rows, gb0 = value_groups.pop(bi - 1)
            value_groups[bi] = (latent, g0, g_rows + rows, gb0)
        else:
            value_groups[bi] = (latent, r0, rows, b0)

    def item_chunks(item):
        return range(cpi * item, cpi * (item + 1))

    def item_pair(item):
        return cpi * item // GQA_GROUP

    def item_heads(item):
        return sum((_chunk_heads(c) for c in item_chunks(item)), ())

    def query_operand(sb, item):
        cols = []
        for c in item_chunks(item):
            qt = qt_ref[0, c * LANES:(c + 1) * LANES, sb * sub:(sb + 1) * sub]
            zero = jnp.zeros_like(qt)
            cols += [jnp.where(slot_a, qt, zero), jnp.where(slot_a, zero, qt)]
        return jnp.concatenate(cols, axis=1)

    def score_block(slot, item, geometry, rhs, blk, m):
        latent, r0, rows, b0 = blk
        psl = slice(item_pair(item) * LANES, (item_pair(item) + 1) * LANES)
        if not latent:
            s = _dot(kc_ref[0, r0:r0 + rows, psl], rhs)
        elif mode == "dense":
            s = _dot(k_ref[0, r0:r0 + rows, psl], rhs)
        else:
            start, bias = geometry
            s = _dot(k_ref[0, pl.ds(pl.multiple_of(start + r0, LANES), rows), psl], rhs) + bias[r0:r0 + rows]
        s_ref[slot, b0:b0 + rows, :] = s
        top = s.max(axis=0, keepdims=True)
        return top if m is None else jnp.maximum(m, top)

    def exp_block(slot, blk, m):
        _, _, rows, b0 = blk
        p_ref[slot, b0:b0 + rows, :] = jnp.exp2(s_ref[slot, b0:b0 + rows, :] - m).astype(BF16)

    def value_block(slot, item, geometry, blk, acc):
        latent, r0, rows, b0 = blk
        pair = item_pair(item)
        vsl = slice(pair * VT_CHUNK, (pair + 1) * VT_CHUNK)
        if not latent:
            v_t = vct_ref[0, vsl, r0:r0 + rows]
        elif mode == "dense":
            v_t = vt_ref[0, vsl, r0:r0 + rows]
        else:
            v_t = vt_ref[0, vsl, pl.ds(pl.multiple_of(geometry[0] + r0, LANES), rows)]
        pv = _dot(v_t, p_ref[slot, b0:b0 + rows, :])
        return pv if acc is None else acc + pv

    def with_sink(item, m):
        if not use_sink:
            return m, None
        sink = jnp.concatenate([jnp.full((1, sub), sink_ref[h], F32) for h in item_heads(item)], axis=1) * LOG2E
        m = jnp.maximum(m, sink)
        return m, jnp.exp2(sink - m)

    def store_output(sb, item, o4, sink_term):
        denom = o4[LANES:LANES + 1, :]
        if use_sink:
            denom = denom + sink_term
        o4 = o4[:LANES, :] * (1.0 / denom)
        for j, h in enumerate(item_heads(item)):
            rows = slice(0, HEAD_DIM) if j % 2 == 0 else slice(HEAD_DIM, 2 * HEAD_DIM)
            o_ref[0, h * HEAD_DIM:(h + 1) * HEAD_DIM, sb * sub:(sb + 1) * sub] = (
                o4[rows, j * sub:(j + 1) * sub].astype(BF16))

    items = [(sb, it) for sb in range(n_sub) for it in range(GQA_GROUP * n_pairs // cpi)]
    geometries = [block_geometry(sb) if mode == "window" else None for sb in range(n_sub)]
    n_items = len(items)
    m_cur = sink_cur = sink_prev = None
    for i in range(-1, n_items + 1):
        nxt, prv = i + 1, i - 1
        if nxt < n_items:
            rhs = query_operand(*items[nxt])
        m_next = o_acc = None
        for bi, blk in enumerate(blocks):
            if nxt < n_items:
                m_next = score_block(nxt % 2, items[nxt][1], geometries[items[nxt][0]], rhs, blk, m_next)
            if 0 <= i < n_items:
                exp_block(i % 2, blk, m_cur)
            if 0 <= prv and bi in value_groups:
                o_acc = value_block(prv % 2, items[prv][1], geometries[items[prv][0]], value_groups[bi], o_acc)
        if 0 <= prv:
            store_output(*items[prv], o_acc, sink_prev)
        sink_prev = sink_cur
        if nxt < n_items:
            m_cur, sink_cur = with_sink(items[nxt][1], m_next)


def _attention(qt, kv, kv_ctx, sink, *, mode, tq):
    b, wq, sq = qt.shape
    n_pairs = wq // (2 * GQA_GROUP * HEAD_DIM)
    kc, vct = kv_ctx
    c_len = kc.shape[1]
    use_sink = sink is not None
    in_specs = [pl.BlockSpec((1, wq, tq), lambda i, t: (i, 0, t))]
    args = [qt]
    seq = sq
    if mode != "ctx":
        k, vt = kv
        seq = k.shape[1]
        in_specs += [pl.BlockSpec((1, seq, k.shape[2]), lambda i, t: (i, 0, 0)),
                     pl.BlockSpec((1, vt.shape[1], seq), lambda i, t: (i, 0, 0))]
        args += [k, vt]
    in_specs += [pl.BlockSpec((1, c_len, kc.shape[2]), lambda i, t: (i, 0, 0)),
                 pl.BlockSpec((1, vct.shape[1], c_len), lambda i, t: (i, 0, 0))]
    args += [kc, vct]
    if use_sink:
        in_specs.append(pl.BlockSpec(memory_space=pltpu.SMEM))
        args.append(sink)
    n_keys = c_len + {"dense": seq, "window": QUERY_SUB + 2 * WINDOW, "ctx": 0}[mode]
    cpi = ITEM_CHUNKS
    return pl.pallas_call(
        functools.partial(_attn_kernel, mode=mode, n_pairs=n_pairs, n_sub=tq // QUERY_SUB, tq=tq, seq=seq,
                          use_sink=use_sink, cpi=cpi),
        grid=(b, sq // tq),
        in_specs=in_specs,
        out_specs=pl.BlockSpec((1, wq, tq), lambda i, t: (i, 0, t)),
        out_shape=jax.ShapeDtypeStruct((b, wq, sq), BF16),
        scratch_shapes=[pltpu.VMEM((2, n_keys, 2 * cpi * QUERY_SUB), F32),
                        pltpu.VMEM((2, n_keys, 2 * cpi * QUERY_SUB), BF16)],
        compiler_params=_params(),
        name="attn_" + mode,
    )(*args)


def _post_kernel(*refs, pool, tm, seq):
    x_ref, at_ref, mod_ref, gains_ref, wo_ref, wfi_ref, wfo_ref = refs[:7]
    pos = 7
    if pool:
        u_ref, uprev_ref, unext_ref, wpool_ref, pscale_ref = refs[pos:pos + 5]
        pos += 5
    o_ref = refs[pos]
    hid_ref = refs[pos + 1]
    if pool:
        ext_ref = refs[pos + 2]

    n_a = at_ref.shape[1]
    if pool:
        t = pl.program_id(1)
        last = pl.num_programs(1) - 1
        halo = jnp.zeros((POOL_HALO, ext_ref.shape[1]), F32)
        ext_ref[0:POOL_HALO, :] = jnp.where(t > 0, uprev_ref[0], halo)
        ext_ref[POOL_HALO:POOL_HALO + tm, :] = u_ref[0]
        ext_ref[POOL_HALO + tm:, :] = jnp.where(t < last, unext_ref[0], halo)

    def pool_mix(r0, rows):
        n_ext = rows + 2 * POOL_HALO
        tok = t * tm + r0 + lax.broadcasted_iota(jnp.int32, (rows, 1), 0)

        def ahead(v, k):
            return pltpu.roll(v, n_ext - k, 0)

        def behind(v, k):
            return pltpu.roll(v, k, 0)

        mixed = []
        for g, w in enumerate(POOL_WINDOWS):
            sl = slice(g * LANES, (g + 1) * LANES)
            e = ext_ref[r0:r0 + n_ext, sl]
            run, length = e, 1
            while length < w // 2:
                run, length = run + ahead(run, length), 2 * length
            total = run + behind(run, w // 2)
            total = total[POOL_HALO:POOL_HALO + rows]
            cnt = jnp.minimum(tok + (w - w // 2), seq) - jnp.maximum(tok - w // 2, 0)
            diff = total / cnt.astype(F32) - e[POOL_HALO:POOL_HALO + rows]
            mixed.append((_dot(diff.astype(BF16), wpool_ref[g]) * pscale_ref[:, sl]).astype(BF16))
        return jnp.concatenate(mixed, axis=-1)

    def attn_out(r0, rows):
        return _dot_tn(at_ref[0, :, r0:r0 + rows], wo_ref[0:n_a, :])

    def pool_out(r0, rows, y):
        return y + _dot(pool_mix(r0, rows), wo_ref[n_a:, :]) if pool else y

    def residual_and_norm(r0, rows, y):
        x1 = x_ref[0, r0:r0 + rows, :] + mod_ref[0, 2:3, :] * (_rms_normalise(y) * gains_ref[1:2, :])
        h = _rms_normalise(x1) * gains_ref[2:3, :]
        return x1, (h * (1.0 + mod_ref[0, 4:5, :]) + mod_ref[0, 3:4, :]).astype(BF16)

    def ffn_hidden(r0, rows, h):
        for i in range(FFN_HIDDEN // FFN_CHUNK):
            gate = _dot(h, wfi_ref[:, i * FFN_CHUNK:(i + 1) * FFN_CHUNK])
            up = _dot(h, wfi_ref[:, FFN_HIDDEN + i * FFN_CHUNK:FFN_HIDDEN + (i + 1) * FFN_CHUNK])
            hid_ref[r0:r0 + rows, i * FFN_CHUNK:(i + 1) * FFN_CHUNK] = (_silu(gate) * up).astype(BF16)

    def ffn_out(r0, rows, x1):
        z = _dot(hid_ref[r0:r0 + rows, :], wfo_ref[...])
        o_ref[0, r0:r0 + rows, :] = x1 + mod_ref[0, 5:6, :] * (_rms_normalise(z) * gains_ref[3:4, :])

    rows = min(tm // 2, POST_SUB_TILE)
    starts = list(range(0, tm, rows))
    n = len(starts)
    y = {p: pool_out(starts[p], rows, attn_out(starts[p], rows)) for p in range(min(2, n))}
    normed = {0: residual_and_norm(starts[0], rows, y.pop(0))}
    for p in range(n):
        x1, h = normed.pop(p)
        ffn_hidden(starts[p], rows, h)
        if p + 2 < n:
            y[p + 2] = pool_out(starts[p + 2], rows, attn_out(starts[p + 2], rows))
        if p + 1 < n:
            normed[p + 1] = residual_and_norm(starts[p + 1], rows, y.pop(p + 1))
        ffn_out(starts[p], rows, x1)


def _post(x, a_t, mod, gains, wo, wfi, wfo, pool_args, *, tm):
    b, s, d = x.shape
    per_batch_mod = mod.shape[0] > 1
    pool = pool_args is not None
    in_specs = [
        pl.BlockSpec((1, tm, d), lambda i, t: (i, t, 0)),
        pl.BlockSpec((1, a_t.shape[1], tm), lambda i, t: (i, 0, t)),
        pl.BlockSpec((1, SUBLANES, d), (lambda i, t: (i, 0, 0)) if per_batch_mod else (lambda i, t: (0, 0, 0))),
        _const_spec(gains.shape), _const_spec(wo.shape), _const_spec(wfi.shape), _const_spec(wfo.shape),
    ]
    args = [x, a_t, mod, gains, wo, wfi, wfo]
    scratch = [pltpu.VMEM((tm, FFN_HIDDEN), BF16)]
    if pool:
        u, wpool, pscale = pool_args
        nu = u.shape[2]
        per_tile = tm // POOL_HALO
        n_halo_blocks = s // POOL_HALO
        in_specs += [
            pl.BlockSpec((1, tm, nu), lambda i, t: (i, t, 0)),
            pl.BlockSpec((1, POOL_HALO, nu), lambda i, t: (i, jnp.maximum(t * per_tile - 1, 0), 0)),
            pl.BlockSpec((1, POOL_HALO, nu),
                         lambda i, t: (i, jnp.minimum((t + 1) * per_tile, n_halo_blocks - 1), 0)),
            _const_spec(wpool.shape), _const_spec(pscale.shape),
        ]
        args += [u, u, u, wpool, pscale]
        scratch.append(pltpu.VMEM((tm + 2 * POOL_HALO, nu), F32))
    return pl.pallas_call(
        functools.partial(_post_kernel, pool=pool, tm=tm, seq=s),
        grid=(b, s // tm),
        in_specs=in_specs,
        out_specs=pl.BlockSpec((1, tm, d), lambda i, t: (i, t, 0)),
        out_shape=jax.ShapeDtypeStruct((b, s, d), F32),
        scratch_shapes=scratch,
        compiler_params=_params(),
        name="post_even" if pool else "post_odd",
    )(*args)


def _pad_rows(a, rows):
    return jnp.pad(a, [(0, 0)] * (a.ndim - 2) + [(0, rows - a.shape[-2]), (0, 0)])


def kernel(x, c, ctx, c_ctx, w_mod, b_mod, g_pre_mix, g_post_mix, g_pre_ffn, g_post_ffn, we_in, we_out,
           we_q_gain, we_k_gain, we_pool, we_pool_scale, wo_in, wo_out, wo_sink, w_ffn_in, w_ffn_out):
    batch, seq, d = x.shape
    c_len = ctx.shape[1]
    depth = w_mod.shape[0]
    tm = TOKEN_TILE
    tm_ctx = c_len
    assert d == D_MODEL and w_ffn_out.shape[1] == FFN_HIDDEN and w_mod.shape[2] % MOD_COLS_TILE == 0
    assert seq % PRE_TILE == 0 and seq % TOKEN_TILE == 0 and seq % QUERY_TILE == 0 and seq % GRID_W == 0
    assert seq >= QUERY_SUB + 2 * WINDOW and c_len % PRE_SUB_TILE == 0 and c_len % QUERY_SUB == 0
    assert max(POOL_WINDOWS) // 2 <= POOL_HALO

    cc = jnp.concatenate([c, c_ctx[None, :]], axis=0)
    cc = _pad_rows(cc, -(-(batch + 1) // SUBLANES) * SUBLANES)
    mods = _modulation(cc, w_mod, b_mod)

    cos, sin = _rope_tables(seq)
    _, lane_dim = _lane_slot_dim()
    head_mean = jnp.asarray(_head_mean_matrix(), BF16)

    for l in range(depth):
        even = l % 2 == 0
        i = l // 2
        with_ctx = l < depth - 1
        mod_x = _pad_rows(mods[l, :batch].reshape(batch, 6, d), SUBLANES)
        mod_c = _pad_rows(mods[l, batch:batch + 1].reshape(1, 6, d), SUBLANES)
        gains = _pad_rows(jnp.stack([g_pre_mix[l], g_post_mix[l], g_pre_ffn[l], g_post_ffn[l]]), SUBLANES)
        wfi = w_ffn_in[l].astype(BF16)
        wfo = w_ffn_out[l].astype(BF16)

        if even:
            n_heads, n_kv = 8, 2
            w_full, w_out = we_in[i], we_out[i].astype(BF16)
            norm_args = (head_mean, we_q_gain[i][lane_dim][None, :], we_k_gain[i][lane_dim][None, :])
            sink = None
            mode = "dense"
        else:
            n_heads, n_kv = 16, 4
            w_full, w_out = wo_in[i], wo_out[i].astype(BF16)
            norm_args = None
            sink = wo_sink[i]
            mode = "window"
        q_w, kv_w = n_heads * HEAD_DIM, n_kv * HEAD_DIM
        cols = np.concatenate([_q_cols(n_heads), _k_cols(n_kv, q_w), np.arange(q_w + kv_w, w_full.shape[1])])
        w_in = w_full[:, cols].astype(BF16)
        widths = (q_w, kv_w, kv_w, w_full.shape[1] - q_w - 2 * kv_w)

        lat = _pre(x, mod_x, gains, w_in, norm_args, (cos, sin), widths=widths, tm=PRE_TILE)
        con = _pre(ctx, mod_c, gains, w_in, norm_args, None, widths=widths, tm=tm_ctx)
        a_t = _attention(lat[0], (lat[1], lat[2]), (con[1], con[2]), sink, mode=mode, tq=QUERY_TILE)
        pool_w = (we_pool[i].astype(BF16), we_pool_scale[i][None, :]) if even else None
        x = _post(x, a_t, mod_x, gains, w_out, wfi, wfo, (lat[3],) + pool_w if even else None, tm=tm)
        if with_ctx:
            ac_t = _attention(con[0], None, (con[1], con[2]), sink, mode="ctx", tq=c_len)
            ctx = _post(ctx, ac_t, mod_c, gains, w_out, wfi, wfo,
                        (con[3],) + pool_w if even else None, tm=tm_ctx)
    return x
```

```python
import functools
import math

import numpy as np
import jax
import jax.numpy as jnp
from jax import lax
from jax.experimental import pallas as pl
from jax.experimental.pallas import tpu as pltpu

D_MODEL = 1024
HEAD_DIM = 64
GQA_GROUP = 4
GRID_W = 64
ROPE_THETA = 10000.0
EPS = 1e-6
WINDOW = 128
POOL_WINDOWS = (2, 4, 8, 16)
POOL_HALO = 8
FFN_HIDDEN = 2816
FFN_CHUNK = 256
LANES = 128
SUBLANES = 8
ROPE_HALF = HEAD_DIM // 2
LOG2E = math.log2(math.e)
Q_SCALE = LOG2E / math.sqrt(HEAD_DIM)
VMEM_LIMIT = 56 * 1024 * 1024
MOD_COLS_TILE = 1536
TOKEN_TILE = 512
POST_SUB_TILE = 256
PRE_TILE = 1024
PRE_SUB_TILE = 256
QUERY_TILE = 1024
QUERY_SUB = LANES
ITEM_CHUNKS = 2
KEY_BLOCK = 512
VALUE_GROUP = 1024
VT_ONES_ROWS = 16
VT_CHUNK = LANES + VT_ONES_ROWS

F32 = jnp.float32
BF16 = jnp.bfloat16


def _lane_slot_dim():
    quarter = HEAD_DIM // 4
    lane = np.arange(LANES)
    part = lane // ROPE_HALF
    i = lane % ROPE_HALF
    dim = np.where(i < quarter, i, ROPE_HALF + (i - quarter)) + np.where(part >= 2, quarter, 0)
    return part % 2, dim


def _chunk_heads(c):
    head_a = (c // GQA_GROUP) * 2 * GQA_GROUP + c % GQA_GROUP
    return head_a, head_a + GQA_GROUP


def _q_cols(n_heads):
    slot, dim = _lane_slot_dim()
    cols = []
    for c in range(n_heads // 2):
        head_a, head_b = _chunk_heads(c)
        cols.append(np.where(slot == 0, head_a, head_b) * HEAD_DIM + dim)
    return np.concatenate(cols)


def _k_cols(n_kv, base):
    slot, dim = _lane_slot_dim()
    return np.concatenate([base + (2 * m + slot) * HEAD_DIM + dim for m in range(n_kv // 2)])


def _head_mean_matrix():
    slot, _ = _lane_slot_dim()
    mean = (slot[:, None] == slot[None, :]).astype(np.float32) / HEAD_DIM
    return np.concatenate([mean, mean], axis=0)


def _rope_tables(seq):
    quarter = HEAD_DIM // 4
    freqs = ROPE_THETA ** (-jnp.arange(quarter, dtype=F32) / quarter)
    t = jnp.arange(seq, dtype=jnp.int32)
    rows = (t // GRID_W).astype(F32)[:, None] * freqs[None, :]
    cols = (t % GRID_W).astype(F32)[:, None] * freqs[None, :]
    ang = jnp.concatenate([rows, cols], axis=-1)
    cos = jnp.tile(jnp.cos(ang), (1, 4))
    sin = jnp.sin(ang)
    sin = jnp.concatenate([-sin, -sin, sin, sin], axis=-1)
    return cos, sin


def _rms_normalise(x):
    return x * lax.rsqrt(jnp.mean(x * x, axis=-1, keepdims=True) + EPS)


def _silu(x):
    return x * (1.0 / (1.0 + jnp.exp(-x)))


def _dot(a, b):
    return jnp.dot(a, b, preferred_element_type=F32)


def _dot_tn(a_t, b):
    return lax.dot_general(a_t, b, (((0,), (0,)), ((), ())), preferred_element_type=F32)


def _const_spec(shape):
    return pl.BlockSpec(shape, lambda *_: (0,) * len(shape), pipeline_mode=pl.Buffered(1))


def _params():
    return pltpu.CompilerParams(dimension_semantics=("parallel", "parallel"), vmem_limit_bytes=VMEM_LIMIT)


def _mod_kernel(c_ref, w_ref, b_ref, o_ref):
    s = _silu(c_ref[...]).astype(BF16)
    o_ref[0] = _dot(s, w_ref[0].astype(BF16)) + b_ref[0]


def _modulation(cc, w_mod, b_mod):
    depth, d, n = w_mod.shape
    rows = cc.shape[0]
    tn = MOD_COLS_TILE
    return pl.pallas_call(
        _mod_kernel,
        grid=(depth, n // tn),
        in_specs=[
            pl.BlockSpec((rows, d), lambda l, j: (0, 0)),
            pl.BlockSpec((1, d, tn), lambda l, j: (l, 0, j)),
            pl.BlockSpec((1, 1, tn), lambda l, j: (l, 0, j)),
        ],
        out_specs=pl.BlockSpec((1, rows, tn), lambda l, j: (l, 0, j)),
        out_shape=jax.ShapeDtypeStruct((depth, rows, n), F32),
        compiler_params=_params(),
        name="modulation",
    )(cc, w_mod, b_mod.reshape(depth, 1, n))


def _pre_kernel(*refs, n_q, n_k, n_v, n_u, qk_norm, rope):
    x_ref, mod_ref, gains_ref, w_ref = refs[:4]
    pos = 4
    if qk_norm:
        pm_ref, qg_ref, kg_ref = refs[pos:pos + 3]
        pos += 3
    if rope:
        cos_ref, sin_ref = refs[pos:pos + 2]
        pos += 2
    qt_ref, k_ref, vt_ref = refs[pos:pos + 3]
    u_ref = refs[pos + 3] if n_u else None

    def modulated(rows):
        h = _rms_normalise(x_ref[0, rows, :]) * gains_ref[0:1, :]
        return (h * (1.0 + mod_ref[0, 1:2, :]) + mod_ref[0, 0:1, :]).astype(BF16)

    def head_chunk(c, gain_ref, rows):
        if qk_norm:
            c2 = c * c
            hi = c2.astype(BF16)
            lo = (c2 - hi.astype(F32)).astype(BF16)
            ms = _dot(jnp.concatenate([hi, lo], axis=1), pm_ref[...])
            c = c * lax.rsqrt(ms + EPS) * gain_ref[...]
        if rope:
            c = c * cos_ref[rows, :] + pltpu.roll(c, LANES // 2, 1) * sin_ref[rows, :]
        return c

    def project(rows, h):
        p = _dot(h, w_ref[...])
        for j in range(n_q // LANES):
            sl = slice(j * LANES, (j + 1) * LANES)
            c = head_chunk(p[:, sl], qg_ref if qk_norm else None, rows) * Q_SCALE
            qt_ref[0, sl, rows] = c.T.astype(BF16)
        for j in range(n_k // LANES):
            sl = slice(j * LANES, (j + 1) * LANES)
            c = head_chunk(p[:, n_q + j * LANES:n_q + (j + 1) * LANES], kg_ref if qk_norm else None, rows)
            k_ref[0, rows, sl] = c.astype(BF16)
        ones = jnp.ones((VT_ONES_ROWS, p.shape[0]), BF16)
        for j in range(n_v // LANES):
            r0 = j * VT_CHUNK
            vt_ref[0, r0:r0 + LANES, rows] = (
                p[:, n_q + n_k + j * LANES:n_q + n_k + (j + 1) * LANES].T.astype(BF16))
            vt_ref[0, r0 + LANES:r0 + VT_CHUNK, rows] = ones
        if n_u:
            u_ref[0, rows, :] = p[:, n_q + n_k + n_v:]

    tm = x_ref.shape[1]
    step = min(tm, PRE_SUB_TILE)
    parts = [slice(r, r + step) for r in range(0, tm, step)]
    h_next = modulated(parts[0])
    for i, rows in enumerate(parts):
        h_cur = h_next
        if i + 1 < len(parts):
            h_next = modulated(parts[i + 1])
        project(rows, h_cur)


def _pre(x, mod, gains, w, norm_args, rope_args, *, widths, tm):
    n_q, n_k, n_v, n_u = widths
    b, s, d = x.shape
    per_batch_mod = mod.shape[0] > 1
    qk_norm = norm_args is not None
    rope = rope_args is not None
    in_specs = [
        pl.BlockSpec((1, tm, d), lambda i, t: (i, t, 0)),
        pl.BlockSpec((1, SUBLANES, d), (lambda i, t: (i, 0, 0)) if per_batch_mod else (lambda i, t: (0, 0, 0))),
        _const_spec(gains.shape),
        _const_spec(w.shape),
    ]
    args = [x, mod, gains, w]
    if qk_norm:
        in_specs += [_const_spec(a.shape) for a in norm_args]
        args += list(norm_args)
    if rope:
        in_specs += [pl.BlockSpec((tm, LANES), lambda i, t: (t, 0))] * 2
        args += list(rope_args)
    vt_rows = n_v // LANES * VT_CHUNK
    out_shape = [jax.ShapeDtypeStruct((b, n_q, s), BF16), jax.ShapeDtypeStruct((b, s, n_k), BF16),
                 jax.ShapeDtypeStruct((b, vt_rows, s), BF16)]
    out_specs = [pl.BlockSpec((1, n_q, tm), lambda i, t: (i, 0, t)),
                 pl.BlockSpec((1, tm, n_k), lambda i, t: (i, t, 0)),
                 pl.BlockSpec((1, vt_rows, tm), lambda i, t: (i, 0, t))]
    if n_u:
        out_shape.append(jax.ShapeDtypeStruct((b, s, n_u), F32))
        out_specs.append(pl.BlockSpec((1, tm, n_u), lambda i, t: (i, t, 0)))
    return pl.pallas_call(
        functools.partial(_pre_kernel, n_q=n_q, n_k=n_k, n_v=n_v, n_u=n_u, qk_norm=qk_norm, rope=rope),
        grid=(b, s // tm),
        in_specs=in_specs,
        out_specs=out_specs,
        out_shape=out_shape,
        compiler_params=_params(),
        name="pre_even" if qk_norm else "pre_odd",
    )(*args)


def _attn_kernel(*refs, mode, n_pairs, n_sub, tq, seq, use_sink, cpi):
    qt_ref = refs[0]
    pos = 1
    if mode != "ctx":
        k_ref, vt_ref = refs[pos:pos + 2]
        pos += 2
    kc_ref, vct_ref = refs[pos:pos + 2]
    pos += 2
    if use_sink:
        sink_ref = refs[pos]
        pos += 1
    o_ref, s_ref, p_ref = refs[pos:pos + 3]

    sub = QUERY_SUB
    row = lax.broadcasted_iota(jnp.int32, (LANES, 1), 0)
    slot_a = (row // ROPE_HALF) % 2 == 0
    band = sub + 2 * WINDOW
    if mode == "window":
        rel = (lax.broadcasted_iota(jnp.int32, (band, sub), 0)
               - lax.broadcasted_iota(jnp.int32, (band, sub), 1))

    def block_geometry(sb):
        q0 = pl.program_id(1) * tq + sb * sub
        start = pl.multiple_of(jnp.clip(q0 - WINDOW, 0, seq - band), LANES)
        bias = jnp.where(jnp.abs(rel + (start - q0)) <= WINDOW, 0.0, -jnp.inf).astype(F32)
        return start, jnp.concatenate([bias] * (2 * cpi), axis=1)

    c_len = kc_ref.shape[1]
    first = {"dense": seq, "window": band, "ctx": 0}[mode]
    blocks = [(True, r, min(KEY_BLOCK, first - r), r) for r in range(0, first, KEY_BLOCK)]
    blocks += [(False, r, min(KEY_BLOCK, c_len - r), first + r) for r in range(0, c_len, KEY_BLOCK)]
    value_groups = {}
    for bi, (latent, r0, rows, b0) in enumerate(blocks):
        if r0 % VALUE_GROUP:
            _, g0, g_rows, gb0 = value_groups.pop(bi - 1)
            value_groups[bi] = (latent, g0, g_rows + rows, gb0)
        else:
            value_groups[bi] = (latent, r0, rows, b0)

    def item_chunks(item):
        return range(cpi * item, cpi * (item + 1))

    def item_pair(item):
        return cpi * item // GQA_GROUP

    def item_heads(item):
        return sum((_chunk_heads(c) for c in item_chunks(item)), ())

    def query_operand(sb, item):
        cols = []
        for c in item_chunks(item):
            qt = qt_ref[0, c * LANES:(c + 1) * LANES, sb * sub:(sb + 1) * sub]
            zero = jnp.zeros_like(qt)
            cols += [jnp.where(slot_a, qt, zero), jnp.where(slot_a, zero, qt)]
        return jnp.concatenate(cols, axis=1)

    def score_block(slot, item, geometry, rhs, blk, m):
        latent, r0, rows, b0 = blk
        psl = slice(item_pair(item) * LANES, (item_pair(item) + 1) * LANES)
        if not latent:
            s = _dot(kc_ref[0, r0:r0 + rows, psl], rhs)
        elif mode == "dense":
            s = _dot(k_ref[0, r0:r0 + rows, psl], rhs)
        else:
            start, bias = geometry
            s = _dot(k_ref[0, pl.ds(pl.multiple_of(start + r0, LANES), rows), psl], rhs) + bias[r0:r0 + rows]
        s_ref[slot, b0:b0 + rows, :] = s
        top = s.max(axis=0, keepdims=True)
        return top if m is None else jnp.maximum(m, top)

    def exp_block(slot, blk, m):
        _, _, rows, b0 = blk
        p_ref[slot, b0:b0 + rows, :] = jnp.exp2(s_ref[slot, b0:b0 + rows, :] - m).astype(BF16)

    def value_block(slot, item, geometry, blk, acc):
        latent, r0, rows, b0 = blk
        pair = item_pair(item)
        vsl = slice(pair * VT_CHUNK, (pair + 1) * VT_CHUNK)
        if not latent:
            v_t = vct_ref[0, vsl, r0:r0 + rows]
        elif mode == "dense":
            v_t = vt_ref[0, vsl, r0:r0 + rows]
        else:
            v_t = vt_ref[0, vsl, pl.ds(pl.multiple_of(geometry[0] + r0, LANES), rows)]
        pv = _dot(v_t, p_ref[slot, b0:b0 + rows, :])
        return pv if acc is None else acc + pv

    def with_sink(item, m):
        if not use_sink:
            return m, None
        sink = jnp.concatenate([jnp.full((1, sub), sink_ref[h], F32) for h in item_heads(item)], axis=1) * LOG2E
        m = jnp.maximum(m, sink)
        return m, jnp.exp2(sink - m)

    def store_output(sb, item, o4, sink_term):
        denom = o4[LANES:LANES + 1, :]
        if use_sink:
            denom = denom + sink_term
        o4 = o4[:LANES, :] * (1.0 / denom)
        for j, h in enumerate(item_heads(item)):
            rows = slice(0, HEAD_DIM) if j % 2 == 0 else slice(HEAD_DIM, 2 * HEAD_DIM)
            o_ref[0, h * HEAD_DIM:(h + 1) * HEAD_DIM, sb * sub:(sb + 1) * sub] = (
                o4[rows, j * sub:(j + 1) * sub].astype(BF16))

    items = [(sb, it) for sb in range(n_sub) for it in range(GQA_GROUP * n_pairs // cpi)]
    geometries = [block_geometry(sb) if mode == "window" else None for sb in range(n_sub)]
    n_items = len(items)
    m_cur = sink_cur = sink_prev = None
    for i in range(-1, n_items + 1):
        nxt, prv = i + 1, i - 1
        if nxt < n_items:
            rhs = query_operand(*items[nxt])
        m_next = o_acc = None
        for bi, blk in enumerate(blocks):
            if nxt < n_items:
                m_next = score_block(nxt % 2, items[nxt][1], geometries[items[nxt][0]], rhs, blk, m_next)
            if 0 <= i < n_items:
                exp_block(i % 2, blk, m_cur)
            if 0 <= prv and bi in value_groups:
                o_acc = value_block(prv % 2, items[prv][1], geometries[items[prv][0]], value_groups[bi], o_acc)
        if 0 <= prv:
            store_output(*items[prv], o_acc, sink_prev)
        sink_prev = sink_cur
        if nxt < n_items:
            m_cur, sink_cur = with_sink(items[nxt][1], m_next)


def _attention(qt, kv, kv_ctx, sink, *, mode, tq):
    b, wq, sq = qt.shape
    n_pairs = wq // (2 * GQA_GROUP * HEAD_DIM)
    kc, vct = kv_ctx
    c_len = kc.shape[1]
    use_sink = sink is not None
    in_specs = [pl.BlockSpec((1, wq, tq), lambda i, t: (i, 0, t))]
    args = [qt]
    seq = sq
    if mode != "ctx":
        k, vt = kv
        seq = k.shape[1]
        in_specs += [pl.BlockSpec((1, seq, k.shape[2]), lambda i, t: (i, 0, 0)),
                     pl.BlockSpec((1, vt.shape[1], seq), lambda i, t: (i, 0, 0))]
        args += [k, vt]
    in_specs += [pl.BlockSpec((1, c_len, kc.shape[2]), lambda i, t: (i, 0, 0)),
                 pl.BlockSpec((1, vct.shape[1], c_len), lambda i, t: (i, 0, 0))]
    args += [kc, vct]
    if use_sink:
        in_specs.append(pl.BlockSpec(memory_space=pltpu.SMEM))
        args.append(sink)
    n_keys = c_len + {"dense": seq, "window": QUERY_SUB + 2 * WINDOW, "ctx": 0}[mode]
    cpi = ITEM_CHUNKS
    return pl.pallas_call(
        functools.partial(_attn_kernel, mode=mode, n_pairs=n_pairs, n_sub=tq // QUERY_SUB, tq=tq, seq=seq,
                          use_sink=use_sink, cpi=cpi),
        grid=(b, sq // tq),
        in_specs=in_specs,
        out_specs=pl.BlockSpec((1, wq, tq), lambda i, t: (i, 0, t)),
        out_shape=jax.ShapeDtypeStruct((b, wq, sq), BF16),
        scratch_shapes=[pltpu.VMEM((2, n_keys, 2 * cpi * QUERY_SUB), F32),
                        pltpu.VMEM((2, n_keys, 2 * cpi * QUERY_SUB), BF16)],
        compiler_params=_params(),
        name="attn_" + mode,
    )(*args)


def _post_kernel(*refs, pool, tm, seq):
    x_ref, at_ref, mod_ref, gains_ref, wo_ref, wfi_ref, wfo_ref = refs[:7]
    pos = 7
    if pool:
        u_ref, uprev_ref, unext_ref, wpool_ref, pscale_ref = refs[pos:pos + 5]
        pos += 5
    o_ref = refs[pos]
    hid_ref = refs[pos + 1]
    if pool:
        ext_ref = refs[pos + 2]

    n_a = at_ref.shape[1]
    if pool:
        t = pl.program_id(1)
        last = pl.num_programs(1) - 1
        halo = jnp.zeros((POOL_HALO, ext_ref.shape[1]), F32)
        ext_ref[0:POOL_HALO, :] = jnp.where(t > 0, uprev_ref[0], halo)
        ext_ref[POOL_HALO:POOL_HALO + tm, :] = u_ref[0]
        ext_ref[POOL_HALO + tm:, :] = jnp.where(t < last, unext_ref[0], halo)

    def pool_mix(r0, rows):
        n_ext = rows + 2 * POOL_HALO
        tok = t * tm + r0 + lax.broadcasted_iota(jnp.int32, (rows, 1), 0)

        def ahead(v, k):
            return pltpu.roll(v, n_ext - k, 0)

        def behind(v, k):
            return pltpu.roll(v, k, 0)

        mixed = []
        for g, w in enumerate(POOL_WINDOWS):
            sl = slice(g * LANES, (g + 1) * LANES)
            e = ext_ref[r0:r0 + n_ext, sl]
            run, length = e, 1
            while length < w // 2:
                run, length = run + ahead(run, length), 2 * length
            total = run + behind(run, w // 2)
            total = total[POOL_HALO:POOL_HALO + rows]
            cnt = jnp.minimum(tok + (w - w // 2), seq) - jnp.maximum(tok - w // 2, 0)
            diff = total / cnt.astype(F32) - e[POOL_HALO:POOL_HALO + rows]
            mixed.append((_dot(diff.astype(BF16), wpool_ref[g]) * pscale_ref[:, sl]).astype(BF16))
        return jnp.concatenate(mixed, axis=-1)

    def attn_out(r0, rows):
        return _dot_tn(at_ref[0, :, r0:r0 + rows], wo_ref[0:n_a, :])

    def pool_out(r0, rows, y):
        return y + _dot(pool_mix(r0, rows), wo_ref[n_a:, :]) if pool else y

    def residual_and_norm(r0, rows, y):
        x1 = x_ref[0, r0:r0 + rows, :] + mod_ref[0, 2:3, :] * (_rms_normalise(y) * gains_ref[1:2, :])
        h = _rms_normalise(x1) * gains_ref[2:3, :]
        return x1, (h * (1.0 + mod_ref[0, 4:5, :]) + mod_ref[0, 3:4, :]).astype(BF16)

    def ffn_hidden(r0, rows, h):
        for i in range(FFN_HIDDEN // FFN_CHUNK):
            gate = _dot(h, wfi_ref[:, i * FFN_CHUNK:(i + 1) * FFN_CHUNK])
            up = _dot(h, wfi_ref[:, FFN_HIDDEN + i * FFN_CHUNK:FFN_HIDDEN + (i + 1) * FFN_CHUNK])
            hid_ref[r0:r0 + rows, i * FFN_CHUNK:(i + 1) * FFN_CHUNK] = (_silu(gate) * up).astype(BF16)

    def ffn_out(r0, rows, x1):
        z = _dot(hid_ref[r0:r0 + rows, :], wfo_ref[...])
        o_ref[0, r0:r0 + rows, :] = x1 + mod_ref[0, 5:6, :] * (_rms_normalise(z) * gains_ref[3:4, :])

    rows = min(tm // 2, POST_SUB_TILE)
    starts = list(range(0, tm, rows))
    n = len(starts)
    y = {p: pool_out(starts[p], rows, attn_out(starts[p], rows)) for p in range(min(2, n))}
    normed = {0: residual_and_norm(starts[0], rows, y.pop(0))}
    for p in range(n):
        x1, h = normed.pop(p)
        ffn_hidden(starts[p], rows, h)
        if p + 2 < n:
            y[p + 2] = pool_out(starts[p + 2], rows, attn_out(starts[p + 2], rows))
        if p + 1 < n:
            normed[p + 1] = residual_and_norm(starts[p + 1], rows, y.pop(p + 1))
        ffn_out(starts[p], rows, x1)


def _post(x, a_t, mod, gains, wo, wfi, wfo, pool_args, *, tm):
    b, s, d = x.shape
    per_batch_mod = mod.shape[0] > 1
    pool = pool_args is not None
    in_specs = [
        pl.BlockSpec((1, tm, d), lambda i, t: (i, t, 0)),
        pl.BlockSpec((1, a_t.shape[1], tm), lambda i, t: (i, 0, t)),
        pl.BlockSpec((1, SUBLANES, d), (lambda i, t: (i, 0, 0)) if per_batch_mod else (lambda i, t: (0, 0, 0))),
        _const_spec(gains.shape), _const_spec(wo.shape), _const_spec(wfi.shape), _const_spec(wfo.shape),
    ]
    args = [x, a_t, mod, gains, wo, wfi, wfo]
    scratch = [pltpu.VMEM((tm, FFN_HIDDEN), BF16)]
    if pool:
        u, wpool, pscale = pool_args
        nu = u.shape[2]
        per_tile = tm // POOL_HALO
        n_halo_blocks = s // POOL_HALO
        in_specs += [
            pl.BlockSpec((1, tm, nu), lambda i, t: (i, t, 0)),
            pl.BlockSpec((1, POOL_HALO, nu), lambda i, t: (i, jnp.maximum(t * per_tile - 1, 0), 0)),
            pl.BlockSpec((1, POOL_HALO, nu),
                         lambda i, t: (i, jnp.minimum((t + 1) * per_tile, n_halo_blocks - 1), 0)),
            _const_spec(wpool.shape), _const_spec(pscale.shape),
        ]
        args += [u, u, u, wpool, pscale]
        scratch.append(pltpu.VMEM((tm + 2 * POOL_HALO, nu), F32))
    return pl.pallas_call(
        functools.partial(_post_kernel, pool=pool, tm=tm, seq=s),
        grid=(b, s // tm),
        in_specs=in_specs,
        out_specs=pl.BlockSpec((1, tm, d), lambda i, t: (i, t, 0)),
        out_shape=jax.ShapeDtypeStruct((b, s, d), F32),
        scratch_shapes=scratch,
        compiler_params=_params(),
        name="post_even" if pool else "post_odd",
    )(*args)


def _pad_rows(a, rows):
    return jnp.pad(a, [(0, 0)] * (a.ndim - 2) + [(0, rows - a.shape[-2]), (0, 0)])


def kernel(x, c, ctx, c_ctx, w_mod, b_mod, g_pre_mix, g_post_mix, g_pre_ffn, g_post_ffn, we_in, we_out,
           we_q_gain, we_k_gain, we_pool, we_pool_scale, wo_in, wo_out, wo_sink, w_ffn_in, w_ffn_out):
    batch, seq, d = x.shape
    c_len = ctx.shape[1]
    depth = w_mod.shape[0]
    tm = TOKEN_TILE
    tm_ctx = c_len
    assert d == D_MODEL and w_ffn_out.shape[1] == FFN_HIDDEN and w_mod.shape[2] % MOD_COLS_TILE == 0
    assert seq % PRE_TILE == 0 and seq % TOKEN_TILE == 0 and seq % QUERY_TILE == 0 and seq % GRID_W == 0
    assert seq >= QUERY_SUB + 2 * WINDOW and c_len % PRE_SUB_TILE == 0 and c_len % QUERY_SUB == 0
    assert max(POOL_WINDOWS) // 2 <= POOL_HALO

    cc = jnp.concatenate([c, c_ctx[None, :]], axis=0)
    cc = _pad_rows(cc, -(-(batch + 1) // SUBLANES) * SUBLANES)
    mods = _modulation(cc, w_mod, b_mod)

    cos, sin = _rope_tables(seq)
    _, lane_dim = _lane_slot_dim()
    head_mean = jnp.asarray(_head_mean_matrix(), BF16)

    for l in range(depth):
        even = l % 2 == 0
        i = l // 2
        with_ctx = l < depth - 1
        mod_x = _pad_rows(mods[l, :batch].reshape(batch, 6, d), SUBLANES)
        mod_c = _pad_rows(mods[l, batch:batch + 1].reshape(1, 6, d), SUBLANES)
        gains = _pad_rows(jnp.stack([g_pre_mix[l], g_post_mix[l], g_pre_ffn[l], g_post_ffn[l]]), SUBLANES)
        wfi = w_ffn_in[l].astype(BF16)
        wfo = w_ffn_out[l].astype(BF16)

        if even:
            n_heads, n_kv = 8, 2
            w_full, w_out = we_in[i], we_out[i].astype(BF16)
            norm_args = (head_mean, we_q_gain[i][lane_dim][None, :], we_k_gain[i][lane_dim][None, :])
            sink = None
            mode = "dense"
        else:
            n_heads, n_kv = 16, 4
            w_full, w_out = wo_in[i], wo_out[i].astype(BF16)
            norm_args = None
            sink = wo_sink[i]
            mode = "window"
        q_w, kv_w = n_heads * HEAD_DIM, n_kv * HEAD_DIM
        cols = np.concatenate([_q_cols(n_heads), _k_cols(n_kv, q_w), np.arange(q_w + kv_w, w_full.shape[1])])
        w_in = w_full[:, cols].astype(BF16)
        widths = (q_w, kv_w, kv_w, w_full.shape[1] - q_w - 2 * kv_w)

        lat = _pre(x, mod_x, gains, w_in, norm_args, (cos, sin), widths=widths, tm=PRE_TILE)
        con = _pre(ctx, mod_c, gains, w_in, norm_args, None, widths=widths, tm=tm_ctx)
        a_t = _attention(lat[0], (lat[1], lat[2]), (con[1], con[2]), sink, mode=mode, tq=QUERY_TILE)
        pool_w = (we_pool[i].astype(BF16), we_pool_scale[i][None, :]) if even else None
        x = _post(x, a_t, mod_x, gains, w_out, wfi, wfo, (lat[3],) + pool_w if even else None, tm=tm)
        if with_ctx:
            ac_t = _attention(con[0], None, (con[1], con[2]), sink, mode="ctx", tq=c_len)
            ctx = _post(ctx, ac_t, mod_c, gains, w_out, wfi, wfo,
                        (con[3],) + pool_w if even else None, tm=tm_ctx)
    return x
```

```python
import functools
import math

import numpy as np
import jax
import jax.numpy as jnp
from jax import lax
from jax.experimental import pallas as pl
from jax.experimental.pallas import tpu as pltpu

D_MODEL = 1024
HEAD_DIM = 64
GQA_GROUP = 4
GRID_W = 64
ROPE_THETA = 10000.0
EPS = 1e-6
WINDOW = 128
POOL_WINDOWS = (2, 4, 8, 16)
POOL_HALO = 8
FFN_HIDDEN = 2816
FFN_CHUNK = 256
LANES = 128
SUBLANES = 8
ROPE_HALF = HEAD_DIM // 2
LOG2E = math.log2(math.e)
Q_SCALE = LOG2E / math.sqrt(HEAD_DIM)
VMEM_LIMIT = 56 * 1024 * 1024
MOD_COLS_TILE = 1536
TOKEN_TILE = 512
POST_SUB_TILE = 256
PRE_TILE = 1024
PRE_SUB_TILE = 256
QUERY_TILE = 1024
QUERY_SUB = LANES
ITEM_CHUNKS = 2
KEY_BLOCK = 1024
VALUE_GROUP = 1024
VT_ONES_ROWS = 16
VT_CHUNK = LANES + VT_ONES_ROWS

F32 = jnp.float32
BF16 = jnp.bfloat16


def _lane_slot_dim():
    quarter = HEAD_DIM // 4
    lane = np.arange(LANES)
    part = lane // ROPE_HALF
    i = lane % ROPE_HALF
    dim = np.where(i < quarter, i, ROPE_HALF + (i - quarter)) + np.where(part >= 2, quarter, 0)
    return part % 2, dim


def _chunk_heads(c):
    head_a = (c // GQA_GROUP) * 2 * GQA_GROUP + c % GQA_GROUP
    return head_a, head_a + GQA_GROUP


def _q_cols(n_heads):
    slot, dim = _lane_slot_dim()
    cols = []
    for c in range(n_heads // 2):
        head_a, head_b = _chunk_heads(c)
        cols.append(np.where(slot == 0, head_a, head_b) * HEAD_DIM + dim)
    return np.concatenate(cols)


def _k_cols(n_kv, base):
    slot, dim = _lane_slot_dim()
    return np.concatenate([base + (2 * m + slot) * HEAD_DIM + dim for m in range(n_kv // 2)])


def _head_mean_matrix():
    slot, _ = _lane_slot_dim()
    mean = (slot[:, None] == slot[None, :]).astype(np.float32) / HEAD_DIM
    return np.concatenate([mean, mean], axis=0)


def _rope_tables(seq):
    quarter = HEAD_DIM // 4
    freqs = ROPE_THETA ** (-jnp.arange(quarter, dtype=F32) / quarter)
    t = jnp.arange(seq, dtype=jnp.int32)
    rows = (t // GRID_W).astype(F32)[:, None] * freqs[None, :]
    cols = (t % GRID_W).astype(F32)[:, None] * freqs[None, :]
    ang = jnp.concatenate([rows, cols], axis=-1)
    cos = jnp.tile(jnp.cos(ang), (1, 4))
    sin = jnp.sin(ang)
    sin = jnp.concatenate([-sin, -sin, sin, sin], axis=-1)
    return cos, sin


def _rms_normalise(x):
    return x * lax.rsqrt(jnp.mean(x * x, axis=-1, keepdims=True) + EPS)


def _silu(x):
    return x * (1.0 / (1.0 + jnp.exp(-x)))


def _dot(a, b):
    return jnp.dot(a, b, preferred_element_type=F32)


def _dot_tn(a_t, b):
    return lax.dot_general(a_t, b, (((0,), (0,)), ((), ())), preferred_element_type=F32)


def _const_spec(shape):
    return pl.BlockSpec(shape, lambda *_: (0,) * len(shape), pipeline_mode=pl.Buffered(1))


def _params():
    return pltpu.CompilerParams(dimension_semantics=("parallel", "parallel"), vmem_limit_bytes=VMEM_LIMIT)


def _mod_kernel(c_ref, w_ref, b_ref, o_ref):
    s = _silu(c_ref[...]).astype(BF16)
    o_ref[0] = _dot(s, w_ref[0].astype(BF16)) + b_ref[0]


def _modulation(cc, w_mod, b_mod):
    depth, d, n = w_mod.shape
    rows = cc.shape[0]
    tn = MOD_COLS_TILE
    return pl.pallas_call(
        _mod_kernel,
        grid=(depth, n // tn),
        in_specs=[
            pl.BlockSpec((rows, d), lambda l, j: (0, 0)),
            pl.BlockSpec((1, d, tn), lambda l, j: (l, 0, j)),
            pl.BlockSpec((1, 1, tn), lambda l, j: (l, 0, j)),
        ],
        out_specs=pl.BlockSpec((1, rows, tn), lambda l, j: (l, 0, j)),
        out_shape=jax.ShapeDtypeStruct((depth, rows, n), F32),
        compiler_params=_params(),
        name="modulation",
    )(cc, w_mod, b_mod.reshape(depth, 1, n))


def _pre_kernel(*refs, n_q, n_k, n_v, n_u, qk_norm, rope):
    x_ref, mod_ref, gains_ref, w_ref = refs[:4]
    pos = 4
    if qk_norm:
        pm_ref, qg_ref, kg_ref = refs[pos:pos + 3]
        pos += 3
    if rope:
        cos_ref, sin_ref = refs[pos:pos + 2]
        pos += 2
    qt_ref, k_ref, vt_ref = refs[pos:pos + 3]
    u_ref = refs[pos + 3] if n_u else None

    def modulated(rows):
        h = _rms_normalise(x_ref[0, rows, :]) * gains_ref[0:1, :]
        return (h * (1.0 + mod_ref[0, 1:2, :]) + mod_ref[0, 0:1, :]).astype(BF16)

    def head_chunk(c, gain_ref, rows):
        if qk_norm:
            c2 = c * c
            hi = c2.astype(BF16)
            lo = (c2 - hi.astype(F32)).astype(BF16)
            ms = _dot(jnp.concatenate([hi, lo], axis=1), pm_ref[...])
            c = c * lax.rsqrt(ms + EPS) * gain_ref[...]
        if rope:
            c = c * cos_ref[rows, :] + pltpu.roll(c, LANES // 2, 1) * sin_ref[rows, :]
        return c

    def project(rows, h):
        p = _dot(h, w_ref[...])
        for j in range(n_q // LANES):
            sl = slice(j * LANES, (j + 1) * LANES)
            c = head_chunk(p[:, sl], qg_ref if qk_norm else None, rows) * Q_SCALE
            qt_ref[0, sl, rows] = c.T.astype(BF16)
        for j in range(n_k // LANES):
            sl = slice(j * LANES, (j + 1) * LANES)
            c = head_chunk(p[:, n_q + j * LANES:n_q + (j + 1) * LANES], kg_ref if qk_norm else None, rows)
            k_ref[0, rows, sl] = c.astype(BF16)
        ones = jnp.ones((VT_ONES_ROWS, p.shape[0]), BF16)
        for j in range(n_v // LANES):
            r0 = j * VT_CHUNK
            vt_ref[0, r0:r0 + LANES, rows] = (
                p[:, n_q + n_k + j * LANES:n_q + n_k + (j + 1) * LANES].T.astype(BF16))
            vt_ref[0, r0 + LANES:r0 + VT_CHUNK, rows] = ones
        if n_u:
            u_ref[0, rows, :] = p[:, n_q + n_k + n_v:]

    tm = x_ref.shape[1]
    step = min(tm, PRE_SUB_TILE)
    parts = [slice(r, r + step) for r in range(0, tm, step)]
    h_next = modulated(parts[0])
    for i, rows in enumerate(parts):
        h_cur = h_next
        if i + 1 < len(parts):
            h_next = modulated(parts[i + 1])
        project(rows, h_cur)


def _pre(x, mod, gains, w, norm_args, rope_args, *, widths, tm):
    n_q, n_k, n_v, n_u = widths
    b, s, d = x.shape
    per_batch_mod = mod.shape[0] > 1
    qk_norm = norm_args is not None
    rope = rope_args is not None
    in_specs = [
        pl.BlockSpec((1, tm, d), lambda i, t: (i, t, 0)),
        pl.BlockSpec((1, SUBLANES, d), (lambda i, t: (i, 0, 0)) if per_batch_mod else (lambda i, t: (0, 0, 0))),
        _const_spec(gains.shape),
        _const_spec(w.shape),
    ]
    args = [x, mod, gains, w]
    if qk_norm:
        in_specs += [_const_spec(a.shape) for a in norm_args]
        args += list(norm_args)
    if rope:
        in_specs += [pl.BlockSpec((tm, LANES), lambda i, t: (t, 0))] * 2
        args += list(rope_args)
    vt_rows = n_v // LANES * VT_CHUNK
    out_shape = [jax.ShapeDtypeStruct((b, n_q, s), BF16), jax.ShapeDtypeStruct((b, s, n_k), BF16),
                 jax.ShapeDtypeStruct((b, vt_rows, s), BF16)]
    out_specs = [pl.BlockSpec((1, n_q, tm), lambda i, t: (i, 0, t)),
                 pl.BlockSpec((1, tm, n_k), lambda i, t: (i, t, 0)),
                 pl.BlockSpec((1, vt_rows, tm), lambda i, t: (i, 0, t))]
    if n_u:
        out_shape.append(jax.ShapeDtypeStruct((b, s, n_u), F32))
        out_specs.append(pl.BlockSpec((1, tm, n_u), lambda i, t: (i, t, 0)))
    return pl.pallas_call(
        functools.partial(_pre_kernel, n_q=n_q, n_k=n_k, n_v=n_v, n_u=n_u, qk_norm=qk_norm, rope=rope),
        grid=(b, s // tm),
        in_specs=in_specs,
        out_specs=out_specs,
        out_shape=out_shape,
        compiler_params=_params(),
        name="pre_even" if qk_norm else "pre_odd",
    )(*args)


def _attn_kernel(*refs, mode, n_pairs, n_sub, tq, seq, use_sink, cpi):
    qt_ref = refs[0]
    pos = 1
    if mode != "ctx":
        k_ref, vt_ref = refs[pos:pos + 2]
        pos += 2
    kc_ref, vct_ref = refs[pos:pos + 2]
    pos += 2
    if use_sink:
        sink_ref = refs[pos]
        pos += 1
    o_ref, s_ref, p_ref = refs[pos:pos + 3]

    sub = QUERY_SUB
    row = lax.broadcasted_iota(jnp.int32, (LANES, 1), 0)
    slot_a = (row // ROPE_HALF) % 2 == 0
    band = sub + 2 * WINDOW
    if mode == "window":
        rel = (lax.broadcasted_iota(jnp.int32, (band, sub), 0)
               - lax.broadcasted_iota(jnp.int32, (band, sub), 1))

    def block_geometry(sb):
        q0 = pl.program_id(1) * tq + sb * sub
        start = pl.multiple_of(jnp.clip(q0 - WINDOW, 0, seq - band), LANES)
        bias = jnp.where(jnp.abs(rel + (start - q0)) <= WINDOW, 0.0, -jnp.inf).astype(F32)
        return start, jnp.concatenate([bias] * (2 * cpi), axis=1)

    c_len = kc_ref.shape[1]
    first = {"dense": seq, "window": band, "ctx": 0}[mode]
    blocks = [(True, r, min(KEY_BLOCK, first - r), r) for r in range(0, first, KEY_BLOCK)]
    blocks += [(False, r, min(KEY_BLOCK, c_len - r), first + r) for r in range(0, c_len, KEY_BLOCK)]
    value_groups = {}
    for bi, (latent, r0, rows, b0) in enumerate(blocks):
        if r0 % VALUE_GROUP:
            _, g0, g_rows, gb0 = value_groups.pop(bi - 1)
            value_groups[bi] = (latent, g0, g_rows + rows, gb0)
        else:
            value_groups[bi] = (latent, r0, rows, b0)

    def item_chunks(item):
        return range(cpi * item, cpi * (item + 1))

    def item_pair(item):
        return cpi * item // GQA_GROUP

    def item_heads(item):
        return sum((_chunk_heads(c) for c in item_chunks(item)), ())

    def query_operand(sb, item):
        cols = []
        for c in item_chunks(item):
            qt = qt_ref[0, c * LANES:(c + 1) * LANES, sb * sub:(sb + 1) * sub]
            zero = jnp.zeros_like(qt)
            cols += [jnp.where(slot_a, qt, zero), jnp.where(slot_a, zero, qt)]
        return jnp.concatenate(cols, axis=1)

    def score_block(slot, item, geometry, rhs, blk, m):
        latent, r0, rows, b0 = blk
        psl = slice(item_pair(item) * LANES, (item_pair(item) + 1) * LANES)
        if not latent:
            s = _dot(kc_ref[0, r0:r0 + rows, psl], rhs)
        elif mode == "dense":
            s = _dot(k_ref[0, r0:r0 + rows, psl], rhs)
        else:
            start, bias = geometry
            s = _dot(k_ref[0, pl.ds(pl.multiple_of(start + r0, LANES), rows), psl], rhs) + bias[r0:r0 + rows]
        s_ref[slot, b0:b0 + rows, :] = s
        top = s.max(axis=0, keepdims=True)
        return top if m is None else jnp.maximum(m, top)

    def exp_block(slot, blk, m):
        _, _, rows, b0 = blk
        p_ref[slot, b0:b0 + rows, :] = jnp.exp2(s_ref[slot, b0:b0 + rows, :] - m).astype(BF16)

    def value_block(slot, item, geometry, blk, acc):
        latent, r0, rows, b0 = blk
        pair = item_pair(item)
        vsl = slice(pair * VT_CHUNK, (pair + 1) * VT_CHUNK)
        if not latent:
            v_t = vct_ref[0, vsl, r0:r0 + rows]
        elif mode == "dense":
            v_t = vt_ref[0, vsl, r0:r0 + rows]
        else:
            v_t = vt_ref[0, vsl, pl.ds(pl.multiple_of(geometry[0] + r0, LANES), rows)]
        pv = _dot(v_t, p_ref[slot, b0:b0 + rows, :])
        return pv if acc is None else acc + pv

    def with_sink(item, m):
        if not use_sink:
            return m, None
        sink = jnp.concatenate([jnp.full((1, sub), sink_ref[h], F32) for h in item_heads(item)], axis=1) * LOG2E
        m = jnp.maximum(m, sink)
        return m, jnp.exp2(sink - m)

    def store_output(sb, item, o4, sink_term):
        denom = o4[LANES:LANES + 1, :]
        if use_sink:
            denom = denom + sink_term
        o4 = o4[:LANES, :] * (1.0 / denom)
        for j, h in enumerate(item_heads(item)):
            rows = slice(0, HEAD_DIM) if j % 2 == 0 else slice(HEAD_DIM, 2 * HEAD_DIM)
            o_ref[0, h * HEAD_DIM:(h + 1) * HEAD_DIM, sb * sub:(sb + 1) * sub] = (
                o4[rows, j * sub:(j + 1) * sub].astype(BF16))

    items = [(sb, it) for sb in range(n_sub) for it in range(GQA_GROUP * n_pairs // cpi)]
    geometries = [block_geometry(sb) if mode == "window" else None for sb in range(n_sub)]
    n_items = len(items)
    m_cur = sink_cur = sink_prev = None
    for i in range(-1, n_items + 1):
        nxt, prv = i + 1, i - 1
        if nxt < n_items:
            rhs = query_operand(*items[nxt])
        m_next = o_acc = None
        for bi, blk in enumerate(blocks):
            if nxt < n_items:
                m_next = score_block(nxt % 2, items[nxt][1], geometries[items[nxt][0]], rhs, blk, m_next)
            if 0 <= i < n_items:
                exp_block(i % 2, blk, m_cur)
            if 0 <= prv and bi in value_groups:
                o_acc = value_block(prv % 2, items[prv][1], geometries[items[prv][0]], value_groups[bi], o_acc)
        if 0 <= prv:
            store_output(*items[prv], o_acc, sink_prev)
        sink_prev = sink_cur
        if nxt < n_items:
            m_cur, sink_cur = with_sink(items[nxt][1], m_next)


def _attention(qt, kv, kv_ctx, sink, *, mode, tq):
    b, wq, sq = qt.shape
    n_pairs = wq // (2 * GQA_GROUP * HEAD_DIM)
    kc, vct = kv_ctx
    c_len = kc.shape[1]
    use_sink = sink is not None
    in_specs = [pl.BlockSpec((1, wq, tq), lambda i, t: (i, 0, t))]
    args = [qt]
    seq = sq
    if mode != "ctx":
        k, vt = kv
        seq = k.shape[1]
        in_specs += [pl.BlockSpec((1, seq, k.shape[2]), lambda i, t: (i, 0, 0)),
                     pl.BlockSpec((1, vt.shape[1], seq), lambda i, t: (i, 0, 0))]
        args += [k, vt]
    in_specs += [pl.BlockSpec((1, c_len, kc.shape[2]), lambda i, t: (i, 0, 0)),
                 pl.BlockSpec((1, vct.shape[1], c_len), lambda i, t: (i, 0, 0))]
    args += [kc, vct]
    if use_sink:
        in_specs.append(pl.BlockSpec(memory_space=pltpu.SMEM))
        args.append(sink)
    n_keys = c_len + {"dense": seq, "window": QUERY_SUB + 2 * WINDOW, "ctx": 0}[mode]
    cpi = ITEM_CHUNKS
    return pl.pallas_call(
        functools.partial(_attn_kernel, mode=mode, n_pairs=n_pairs, n_sub=tq // QUERY_SUB, tq=tq, seq=seq,
                          use_sink=use_sink, cpi=cpi),
        grid=(b, sq // tq),
        in_specs=in_specs,
        out_specs=pl.BlockSpec((1, wq, tq), lambda i, t: (i, 0, t)),
        out_shape=jax.ShapeDtypeStruct((b, wq, sq), BF16),
        scratch_shapes=[pltpu.VMEM((2, n_keys, 2 * cpi * QUERY_SUB), F32),
                        pltpu.VMEM((2, n_keys, 2 * cpi * QUERY_SUB), BF16)],
        compiler_params=_params(),
        name="attn_" + mode,
    )(*args)


def _post_kernel(*refs, pool, tm, seq):
    x_ref, at_ref, mod_ref, gains_ref, wo_ref, wfi_ref, wfo_ref = refs[:7]
    pos = 7
    if pool:
        u_ref, uprev_ref, unext_ref, wpool_ref, pscale_ref = refs[pos:pos + 5]
        pos += 5
    o_ref = refs[pos]
    hid_ref = refs[pos + 1]
    if pool:
        ext_ref = refs[pos + 2]

    n_a = at_ref.shape[1]
    if pool:
        t = pl.program_id(1)
        last = pl.num_programs(1) - 1
        halo = jnp.zeros((POOL_HALO, ext_ref.shape[1]), F32)
        ext_ref[0:POOL_HALO, :] = jnp.where(t > 0, uprev_ref[0], halo)
        ext_ref[POOL_HALO:POOL_HALO + tm, :] = u_ref[0]
        ext_ref[POOL_HALO + tm:, :] = jnp.where(t < last, unext_ref[0], halo)

    def pool_mix(r0, rows):
        n_ext = rows + 2 * POOL_HALO
        tok = t * tm + r0 + lax.broadcasted_iota(jnp.int32, (rows, 1), 0)

        def ahead(v, k):
            return pltpu.roll(v, n_ext - k, 0)

        def behind(v, k):
            return pltpu.roll(v, k, 0)

        mixed = []
        for g, w in enumerate(POOL_WINDOWS):
            sl = slice(g * LANES, (g + 1) * LANES)
            e = ext_ref[r0:r0 + n_ext, sl]
            run, length = e, 1
            while length < w // 2:
                run, length = run + ahead(run, length), 2 * length
            total = run + behind(run, w // 2)
            total = total[POOL_HALO:POOL_HALO + rows]
            cnt = jnp.minimum(tok + (w - w // 2), seq) - jnp.maximum(tok - w // 2, 0)
            diff = total / cnt.astype(F32) - e[POOL_HALO:POOL_HALO + rows]
            mixed.append((_dot(diff.astype(BF16), wpool_ref[g]) * pscale_ref[:, sl]).astype(BF16))
        return jnp.concatenate(mixed, axis=-1)

    def attn_out(r0, rows):
        return _dot_tn(at_ref[0, :, r0:r0 + rows], wo_ref[0:n_a, :])

    def pool_out(r0, rows, y):
        return y + _dot(pool_mix(r0, rows), wo_ref[n_a:, :]) if pool else y

    def residual_and_norm(r0, rows, y):
        x1 = x_ref[0, r0:r0 + rows, :] + mod_ref[0, 2:3, :] * (_rms_normalise(y) * gains_ref[1:2, :])
        h = _rms_normalise(x1) * gains_ref[2:3, :]
        return x1, (h * (1.0 + mod_ref[0, 4:5, :]) + mod_ref[0, 3:4, :]).astype(BF16)

    def ffn_hidden(r0, rows, h):
        for i in range(FFN_HIDDEN // FFN_CHUNK):
            gate = _dot(h, wfi_ref[:, i * FFN_CHUNK:(i + 1) * FFN_CHUNK])
            up = _dot(h, wfi_ref[:, FFN_HIDDEN + i * FFN_CHUNK:FFN_HIDDEN + (i + 1) * FFN_CHUNK])
            hid_ref[r0:r0 + rows, i * FFN_CHUNK:(i + 1) * FFN_CHUNK] = (_silu(gate) * up).astype(BF16)

    def ffn_out(r0, rows, x1):
        z = _dot(hid_ref[r0:r0 + rows, :], wfo_ref[...])
        o_ref[0, r0:r0 + rows, :] = x1 + mod_ref[0, 5:6, :] * (_rms_normalise(z) * gains_ref[3:4, :])

    rows = min(tm // 2, POST_SUB_TILE)
    starts = list(range(0, tm, rows))
    n = len(starts)
    y = {p: pool_out(starts[p], rows, attn_out(starts[p], rows)) for p in range(min(2, n))}
    normed = {0: residual_and_norm(starts[0], rows, y.pop(0))}
    for p in range(n):
        x1, h = normed.pop(p)
        ffn_hidden(starts[p], rows, h)
        if p + 2 < n:
            y[p + 2] = pool_out(starts[p + 2], rows, attn_out(starts[p + 2], rows))
        if p + 1 < n:
            normed[p + 1] = residual_and_norm(starts[p + 1], rows, y.pop(p + 1))
        ffn_out(starts[p], rows, x1)


def _post(x, a_t, mod, gains, wo, wfi, wfo, pool_args, *, tm):
    b, s, d = x.shape
    per_batch_mod = mod.shape[0] > 1
    pool = pool_args is not None
    in_specs = [
        pl.BlockSpec((1, tm, d), lambda i, t: (i, t, 0)),
        pl.BlockSpec((1, a_t.shape[1], tm), lambda i, t: (i, 0, t)),
        pl.BlockSpec((1, SUBLANES, d), (lambda i, t: (i, 0, 0)) if per_batch_mod else (lambda i, t: (0, 0, 0))),
        _const_spec(gains.shape), _const_spec(wo.shape), _const_spec(wfi.shape), _const_spec(wfo.shape),
    ]
    args = [x, a_t, mod, gains, wo, wfi, wfo]
    scratch = [pltpu.VMEM((tm, FFN_HIDDEN), BF16)]
    if pool:
        u, wpool, pscale = pool_args
        nu = u.shape[2]
        per_tile = tm // POOL_HALO
        n_halo_blocks = s // POOL_HALO
        in_specs += [
            pl.BlockSpec((1, tm, nu), lambda i, t: (i, t, 0)),
            pl.BlockSpec((1, POOL_HALO, nu), lambda i, t: (i, jnp.maximum(t * per_tile - 1, 0), 0)),
            pl.BlockSpec((1, POOL_HALO, nu),
                         lambda i, t: (i, jnp.minimum((t + 1) * per_tile, n_halo_blocks - 1), 0)),
            _const_spec(wpool.shape), _const_spec(pscale.shape),
        ]
        args += [u, u, u, wpool, pscale]
        scratch.append(pltpu.VMEM((tm + 2 * POOL_HALO, nu), F32))
    return pl.pallas_call(
        functools.partial(_post_kernel, pool=pool, tm=tm, seq=s),
        grid=(b, s // tm),
        in_specs=in_specs,
        out_specs=pl.BlockSpec((1, tm, d), lambda i, t: (i, t, 0)),
        out_shape=jax.ShapeDtypeStruct((b, s, d), F32),
        scratch_shapes=scratch,
        compiler_params=_params(),
        name="post_even" if pool else "post_odd",
    )(*args)


def _pad_rows(a, rows):
    return jnp.pad(a, [(0, 0)] * (a.ndim - 2) + [(0, rows - a.shape[-2]), (0, 0)])


def kernel(x, c, ctx, c_ctx, w_mod, b_mod, g_pre_mix, g_post_mix, g_pre_ffn, g_post_ffn, we_in, we_out,
           we_q_gain, we_k_gain, we_pool, we_pool_scale, wo_in, wo_out, wo_sink, w_ffn_in, w_ffn_out):
    batch, seq, d = x.shape
    c_len = ctx.shape[1]
    depth = w_mod.shape[0]
    tm = TOKEN_TILE
    tm_ctx = c_len
    assert d == D_MODEL and w_ffn_out.shape[1] == FFN_HIDDEN and w_mod.shape[2] % MOD_COLS_TILE == 0
    assert seq % PRE_TILE == 0 and seq % TOKEN_TILE == 0 and seq % QUERY_TILE == 0 and seq % GRID_W == 0
    assert seq >= QUERY_SUB + 2 * WINDOW and c_len % PRE_SUB_TILE == 0 and c_len % QUERY_SUB == 0
    assert max(POOL_WINDOWS) // 2 <= POOL_HALO

    cc = jnp.concatenate([c, c_ctx[None, :]], axis=0)
    cc = _pad_rows(cc, -(-(batch + 1) // SUBLANES) * SUBLANES)
    mods = _modulation(cc, w_mod, b_mod)

    cos, sin = _rope_tables(seq)
    _, lane_dim = _lane_slot_dim()
    head_mean = jnp.asarray(_head_mean_matrix(), BF16)

    for l in range(depth):
        even = l % 2 == 0
        i = l // 2
        with_ctx = l < depth - 1
        mod_x = _pad_rows(mods[l, :batch].reshape(batch, 6, d), SUBLANES)
        mod_c = _pad_rows(mods[l, batch:batch + 1].reshape(1, 6, d), SUBLANES)
        gains = _pad_rows(jnp.stack([g_pre_mix[l], g_post_mix[l], g_pre_ffn[l], g_post_ffn[l]]), SUBLANES)
        wfi = w_ffn_in[l].astype(BF16)
        wfo = w_ffn_out[l].astype(BF16)

        if even:
            n_heads, n_kv = 8, 2
            w_full, w_out = we_in[i], we_out[i].astype(BF16)
            norm_args = (head_mean, we_q_gain[i][lane_dim][None, :], we_k_gain[i][lane_dim][None, :])
            sink = None
            mode = "dense"
        else:
            n_heads, n_kv = 16, 4
            w_full, w_out = wo_in[i], wo_out[i].astype(BF16)
            norm_args = None
            sink = wo_sink[i]
            mode = "window"
        q_w, kv_w = n_heads * HEAD_DIM, n_kv * HEAD_DIM
        cols = np.concatenate([_q_cols(n_heads), _k_cols(n_kv, q_w), np.arange(q_w + kv_w, w_full.shape[1])])
        w_in = w_full[:, cols].astype(BF16)
        widths = (q_w, kv_w, kv_w, w_full.shape[1] - q_w - 2 * kv_w)

        lat = _pre(x, mod_x, gains, w_in, norm_args, (cos, sin), widths=widths, tm=PRE_TILE)
        con = _pre(ctx, mod_c, gains, w_in, norm_args, None, widths=widths, tm=tm_ctx)
        a_t = _attention(lat[0], (lat[1], lat[2]), (con[1], con[2]), sink, mode=mode, tq=QUERY_TILE)
        pool_w = (we_pool[i].astype(BF16), we_pool_scale[i][None, :]) if even else None
        x = _post(x, a_t, mod_x, gains, w_out, wfi, wfo, (lat[3],) + pool_w if even else None, tm=tm)
        if with_ctx:
            ac_t = _attention(con[0], None, (con[1], con[2]), sink, mode="ctx", tq=c_len)
            ctx = _post(ctx, ac_t, mod_c, gains, w_out, wfi, wfo,
                        (con[3],) + pool_w if even else None, tm=tm_ctx)
    return x
```

```python
import functools
import math

import numpy as np
import jax
import jax.numpy as jnp
from jax import lax
from jax.experimental import pallas as pl
from jax.experimental.pallas import tpu as pltpu

D_MODEL = 1024
HEAD_DIM = 64
GQA_GROUP = 4
GRID_W = 64
ROPE_THETA = 10000.0
EPS = 1e-6
WINDOW = 128
POOL_WINDOWS = (2, 4, 8, 16)
POOL_HALO = 8
FFN_HIDDEN = 2816
FFN_CHUNK = 256
LANES = 128
SUBLANES = 8
ROPE_HALF = HEAD_DIM // 2
LOG2E = math.log2(math.e)
Q_SCALE = LOG2E / math.sqrt(HEAD_DIM)
VMEM_LIMIT = 56 * 1024 * 1024
MOD_COLS_TILE = 1536
TOKEN_TILE = 512
POST_SUB_TILE = 256
PRE_TILE = 1024
PRE_SUB_TILE = 256
QUERY_TILE = 1024
QUERY_SUB = LANES
ITEM_CHUNKS = 4
KEY_BLOCK = 1024
VALUE_GROUP = 1024
VT_ONES_ROWS = 16
VT_CHUNK = LANES + VT_ONES_ROWS

F32 = jnp.float32
BF16 = jnp.bfloat16


def _lane_slot_dim():
    quarter = HEAD_DIM // 4
    lane = np.arange(LANES)
    part = lane // ROPE_HALF
    i = lane % ROPE_HALF
    dim = np.where(i < quarter, i, ROPE_HALF + (i - quarter)) + np.where(part >= 2, quarter, 0)
    return part % 2, dim


def _chunk_heads(c):
    head_a = (c // GQA_GROUP) * 2 * GQA_GROUP + c % GQA_GROUP
    return head_a, head_a + GQA_GROUP


def _q_cols(n_heads):
    slot, dim = _lane_slot_dim()
    cols = []
    for c in range(n_heads // 2):
        head_a, head_b = _chunk_heads(c)
        cols.append(np.where(slot == 0, head_a, head_b) * HEAD_DIM + dim)
    return np.concatenate(cols)


def _k_cols(n_kv, base):
    slot, dim = _lane_slot_dim()
    return np.concatenate([base + (2 * m + slot) * HEAD_DIM + dim for m in range(n_kv // 2)])


def _head_mean_matrix():
    slot, _ = _lane_slot_dim()
    mean = (slot[:, None] == slot[None, :]).astype(np.float32) / HEAD_DIM
    return np.concatenate([mean, mean], axis=0)


def _rope_tables(seq):
    quarter = HEAD_DIM // 4
    freqs = ROPE_THETA ** (-jnp.arange(quarter, dtype=F32) / quarter)
    t = jnp.arange(seq, dtype=jnp.int32)
    rows = (t // GRID_W).astype(F32)[:, None] * freqs[None, :]
    cols = (t % GRID_W).astype(F32)[:, None] * freqs[None, :]
    ang = jnp.concatenate([rows, cols], axis=-1)
    cos = jnp.tile(jnp.cos(ang), (1, 4))
    sin = jnp.sin(ang)
    sin = jnp.concatenate([-sin, -sin, sin, sin], axis=-1)
    return cos, sin


def _rms_normalise(x):
    return x * lax.rsqrt(jnp.mean(x * x, axis=-1, keepdims=True) + EPS)


def _silu(x):
    return x * (1.0 / (1.0 + jnp.exp(-x)))


def _dot(a, b):
    return jnp.dot(a, b, preferred_element_type=F32)


def _dot_tn(a_t, b):
    return lax.dot_general(a_t, b, (((0,), (0,)), ((), ())), preferred_element_type=F32)


def _const_spec(shape):
    return pl.BlockSpec(shape, lambda *_: (0,) * len(shape), pipeline_mode=pl.Buffered(1))


def _params():
    return pltpu.CompilerParams(dimension_semantics=("parallel", "parallel"), vmem_limit_bytes=VMEM_LIMIT)


def _mod_kernel(c_ref, w_ref, b_ref, o_ref):
    s = _silu(c_ref[...]).astype(BF16)
    o_ref[0] = _dot(s, w_ref[0].astype(BF16)) + b_ref[0]


def _modulation(cc, w_mod, b_mod):
    depth, d, n = w_mod.shape
    rows = cc.shape[0]
    tn = MOD_COLS_TILE
    return pl.pallas_call(
        _mod_kernel,
        grid=(depth, n // tn),
        in_specs=[
            pl.BlockSpec((rows, d), lambda l, j: (0, 0)),
            pl.BlockSpec((1, d, tn), lambda l, j: (l, 0, j)),
            pl.BlockSpec((1, 1, tn), lambda l, j: (l, 0, j)),
        ],
        out_specs=pl.BlockSpec((1, rows, tn), lambda l, j: (l, 0, j)),
        out_shape=jax.ShapeDtypeStruct((depth, rows, n), F32),
        compiler_params=_params(),
        name="modulation",
    )(cc, w_mod, b_mod.reshape(depth, 1, n))


def _pre_kernel(*refs, n_q, n_k, n_v, n_u, qk_norm, rope):
    x_ref, mod_ref, gains_ref, w_ref = refs[:4]
    pos = 4
    if qk_norm:
        pm_ref, qg_ref, kg_ref = refs[pos:pos + 3]
        pos += 3
    if rope:
        cos_ref, sin_ref = refs[pos:pos + 2]
        pos += 2
    qt_ref, k_ref, vt_ref = refs[pos:pos + 3]
    u_ref = refs[pos + 3] if n_u else None

    def modulated(rows):
        h = _rms_normalise(x_ref[0, rows, :]) * gains_ref[0:1, :]
        return (h * (1.0 + mod_ref[0, 1:2, :]) + mod_ref[0, 0:1, :]).astype(BF16)

    def head_chunk(c, gain_ref, rows):
        if qk_norm:
            c2 = c * c
            hi = c2.astype(BF16)
            lo = (c2 - hi.astype(F32)).astype(BF16)
            ms = _dot(jnp.concatenate([hi, lo], axis=1), pm_ref[...])
            c = c * lax.rsqrt(ms + EPS) * gain_ref[...]
        if rope:
            c = c * cos_ref[rows, :] + pltpu.roll(c, LANES // 2, 1) * sin_ref[rows, :]
        return c

    def project(rows, h):
        p = _dot(h, w_ref[...])
        for j in range(n_q // LANES):
            sl = slice(j * LANES, (j + 1) * LANES)
            c = head_chunk(p[:, sl], qg_ref if qk_norm else None, rows) * Q_SCALE
            qt_ref[0, sl, rows] = c.T.astype(BF16)
        for j in range(n_k // LANES):
            sl = slice(j * LANES, (j + 1) * LANES)
            c = head_chunk(p[:, n_q + j * LANES:n_q + (j + 1) * LANES], kg_ref if qk_norm else None, rows)
            k_ref[0, rows, sl] = c.astype(BF16)
        ones = jnp.ones((VT_ONES_ROWS, p.shape[0]), BF16)
        for j in range(n_v // LANES):
            r0 = j * VT_CHUNK
            vt_ref[0, r0:r0 + LANES, rows] = (
                p[:, n_q + n_k + j * LANES:n_q + n_k + (j + 1) * LANES].T.astype(BF16))
            vt_ref[0, r0 + LANES:r0 + VT_CHUNK, rows] = ones
        if n_u:
            u_ref[0, rows, :] = p[:, n_q + n_k + n_v:]

    tm = x_ref.shape[1]
    step = min(tm, PRE_SUB_TILE)
    parts = [slice(r, r + step) for r in range(0, tm, step)]
    h_next = modulated(parts[0])
    for i, rows in enumerate(parts):
        h_cur = h_next
        if i + 1 < len(parts):
            h_next = modulated(parts[i + 1])
        project(rows, h_cur)


def _pre(x, mod, gains, w, norm_args, rope_args, *, widths, tm):
    n_q, n_k, n_v, n_u = widths
    b, s, d = x.shape
    per_batch_mod = mod.shape[0] > 1
    qk_norm = norm_args is not None
    rope = rope_args is not None
    in_specs = [
        pl.BlockSpec((1, tm, d), lambda i, t: (i, t, 0)),
        pl.BlockSpec((1, SUBLANES, d), (lambda i, t: (i, 0, 0)) if per_batch_mod else (lambda i, t: (0, 0, 0))),
        _const_spec(gains.shape),
        _const_spec(w.shape),
    ]
    args = [x, mod, gains, w]
    if qk_norm:
        in_specs += [_const_spec(a.shape) for a in norm_args]
        args += list(norm_args)
    if rope:
        in_specs += [pl.BlockSpec((tm, LANES), lambda i, t: (t, 0))] * 2
        args += list(rope_args)
    vt_rows = n_v // LANES * VT_CHUNK
    out_shape = [jax.ShapeDtypeStruct((b, n_q, s), BF16), jax.ShapeDtypeStruct((b, s, n_k), BF16),
                 jax.ShapeDtypeStruct((b, vt_rows, s), BF16)]
    out_specs = [pl.BlockSpec((1, n_q, tm), lambda i, t: (i, 0, t)),
                 pl.BlockSpec((1, tm, n_k), lambda i, t: (i, t, 0)),
                 pl.BlockSpec((1, vt_rows, tm), lambda i, t: (i, 0, t))]
    if n_u:
        out_shape.append(jax.ShapeDtypeStruct((b, s, n_u), F32))
        out_specs.append(pl.BlockSpec((1, tm, n_u), lambda i, t: (i, t, 0)))
    return pl.pallas_call(
        functools.partial(_pre_kernel, n_q=n_q, n_k=n_k, n_v=n_v, n_u=n_u, qk_norm=qk_norm, rope=rope),
        grid=(b, s // tm),
        in_specs=in_specs,
        out_specs=out_specs,
        out_shape=out_shape,
        compiler_params=_params(),
        name="pre_even" if qk_norm else "pre_odd",
    )(*args)


def _attn_kernel(*refs, mode, n_pairs, n_sub, tq, seq, use_sink, cpi):
    qt_ref = refs[0]
    pos = 1
    if mode != "ctx":
        k_ref, vt_ref = refs[pos:pos + 2]
        pos += 2
    kc_ref, vct_ref = refs[pos:pos + 2]
    pos += 2
    if use_sink:
        sink_ref = refs[pos]
        pos += 1
    o_ref, s_ref, p_ref = refs[pos:pos + 3]

    sub = QUERY_SUB
    row = lax.broadcasted_iota(jnp.int32, (LANES, 1), 0)
    slot_a = (row // ROPE_HALF) % 2 == 0
    band = sub + 2 * WINDOW
    if mode == "window":
        rel = (lax.broadcasted_iota(jnp.int32, (band, sub), 0)
               - lax.broadcasted_iota(jnp.int32, (band, sub), 1))

    def block_geometry(sb):
        q0 = pl.program_id(1) * tq + sb * sub
        start = pl.multiple_of(jnp.clip(q0 - WINDOW, 0, seq - band), LANES)
        bias = jnp.where(jnp.abs(rel + (start - q0)) <= WINDOW, 0.0, -jnp.inf).astype(F32)
        return start, jnp.concatenate([bias] * (2 * cpi), axis=1)

    c_len = kc_ref.shape[1]
    first = {"dense": seq, "window": band, "ctx": 0}[mode]
    blocks = [(True, r, min(KEY_BLOCK, first - r), r) for r in range(0, first, KEY_BLOCK)]
    blocks += [(False, r, min(KEY_BLOCK, c_len - r), first + r) for r in range(0, c_len, KEY_BLOCK)]
    value_groups = {}
    for bi, (latent, r0, rows, b0) in enumerate(blocks):
        if r0 % VALUE_GROUP:
            _, g0, g_rows, gb0 = value_groups.pop(bi - 1)
            value_groups[bi] = (latent, g0, g_rows + rows, gb0)
        else:
            value_groups[bi] = (latent, r0, rows, b0)

    def item_chunks(item):
        return range(cpi * item, cpi * (item + 1))

    def item_pair(item):
        return cpi * item // GQA_GROUP

    def item_heads(item):
        return sum((_chunk_heads(c) for c in item_chunks(item)), ())

    def query_operand(sb, item):
        cols = []
        for c in item_chunks(item):
            qt = qt_ref[0, c * LANES:(c + 1) * LANES, sb * sub:(sb + 1) * sub]
            zero = jnp.zeros_like(qt)
            cols += [jnp.where(slot_a, qt, zero), jnp.where(slot_a, zero, qt)]
        return jnp.concatenate(cols, axis=1)

    def score_block(slot, item, geometry, rhs, blk, m):
        latent, r0, rows, b0 = blk
        psl = slice(item_pair(item) * LANES, (item_pair(item) + 1) * LANES)
        if not latent:
            s = _dot(kc_ref[0, r0:r0 + rows, psl], rhs)
        elif mode == "dense":
            s = _dot(k_ref[0, r0:r0 + rows, psl], rhs)
        else:
            start, bias = geometry
            s = _dot(k_ref[0, pl.ds(pl.multiple_of(start + r0, LANES), rows), psl], rhs) + bias[r0:r0 + rows]
        s_ref[slot, b0:b0 + rows, :] = s
        top = s.max(axis=0, keepdims=True)
        return top if m is None else jnp.maximum(m, top)

    def exp_block(slot, blk, m):
        _, _, rows, b0 = blk
        p_ref[slot, b0:b0 + rows, :] = jnp.exp2(s_ref[slot, b0:b0 + rows, :] - m).astype(BF16)

    def value_block(slot, item, geometry, blk, acc):
        latent, r0, rows, b0 = blk
        pair = item_pair(item)
        vsl = slice(pair * VT_CHUNK, (pair + 1) * VT_CHUNK)
        if not latent:
            v_t = vct_ref[0, vsl, r0:r0 + rows]
        elif mode == "dense":
            v_t = vt_ref[0, vsl, r0:r0 + rows]
        else:
            v_t = vt_ref[0, vsl, pl.ds(pl.multiple_of(geometry[0] + r0, LANES), rows)]
        pv = _dot(v_t, p_ref[slot, b0:b0 + rows, :])
        return pv if acc is None else acc + pv

    def with_sink(item, m):
        if not use_sink:
            return m, None
        sink = jnp.concatenate([jnp.full((1, sub), sink_ref[h], F32) for h in item_heads(item)], axis=1) * LOG2E
        m = jnp.maximum(m, sink)
        return m, jnp.exp2(sink - m)

    def store_output(sb, item, o4, sink_term):
        denom = o4[LANES:LANES + 1, :]
        if use_sink:
            denom = denom + sink_term
        o4 = o4[:LANES, :] * (1.0 / denom)
        for j, h in enumerate(item_heads(item)):
            rows = slice(0, HEAD_DIM) if j % 2 == 0 else slice(HEAD_DIM, 2 * HEAD_DIM)
            o_ref[0, h * HEAD_DIM:(h + 1) * HEAD_DIM, sb * sub:(sb + 1) * sub] = (
                o4[rows, j * sub:(j + 1) * sub].astype(BF16))

    items = [(sb, it) for sb in range(n_sub) for it in range(GQA_GROUP * n_pairs // cpi)]
    geometries = [block_geometry(sb) if mode == "window" else None for sb in range(n_sub)]
    n_items = len(items)
    m_cur = sink_cur = sink_prev = None
    for i in range(-1, n_items + 1):
        nxt, prv = i + 1, i - 1
        if nxt < n_items:
            rhs = query_operand(*items[nxt])
        m_next = o_acc = None
        for bi, blk in enumerate(blocks):
            if nxt < n_items:
                m_next = score_block(nxt % 2, items[nxt][1], geometries[items[nxt][0]], rhs, blk, m_next)
            if 0 <= i < n_items:
                exp_block(i % 2, blk, m_cur)
            if 0 <= prv and bi in value_groups:
                o_acc = value_block(prv % 2, items[prv][1], geometries[items[prv][0]], value_groups[bi], o_acc)
        if 0 <= prv:
            store_output(*items[prv], o_acc, sink_prev)
        sink_prev = sink_cur
        if nxt < n_items:
            m_cur, sink_cur = with_sink(items[nxt][1], m_next)


def _attention(qt, kv, kv_ctx, sink, *, mode, tq):
    b, wq, sq = qt.shape
    n_pairs = wq // (2 * GQA_GROUP * HEAD_DIM)
    kc, vct = kv_ctx
    c_len = kc.shape[1]
    use_sink = sink is not None
    in_specs = [pl.BlockSpec((1, wq, tq), lambda i, t: (i, 0, t))]
    args = [qt]
    seq = sq
    if mode != "ctx":
        k, vt = kv
        seq = k.shape[1]
        in_specs += [pl.BlockSpec((1, seq, k.shape[2]), lambda i, t: (i, 0, 0)),
                     pl.BlockSpec((1, vt.shape[1], seq), lambda i, t: (i, 0, 0))]
        args += [k, vt]
    in_specs += [pl.BlockSpec((1, c_len, kc.shape[2]), lambda i, t: (i, 0, 0)),
                 pl.BlockSpec((1, vct.shape[1], c_len), lambda i, t: (i, 0, 0))]
    args += [kc, vct]
    if use_sink:
        in_specs.append(pl.BlockSpec(memory_space=pltpu.SMEM))
        args.append(sink)
    n_keys = c_len + {"dense": seq, "window": QUERY_SUB + 2 * WINDOW, "ctx": 0}[mode]
    cpi = ITEM_CHUNKS
    return pl.pallas_call(
        functools.partial(_attn_kernel, mode=mode, n_pairs=n_pairs, n_sub=tq // QUERY_SUB, tq=tq, seq=seq,
                          use_sink=use_sink, cpi=cpi),
        grid=(b, sq // tq),
        in_specs=in_specs,
        out_specs=pl.BlockSpec((1, wq, tq), lambda i, t: (i, 0, t)),
        out_shape=jax.ShapeDtypeStruct((b, wq, sq), BF16),
        scratch_shapes=[pltpu.VMEM((2, n_keys, 2 * cpi * QUERY_SUB), F32),
                        pltpu.VMEM((2, n_keys, 2 * cpi * QUERY_SUB), BF16)],
        compiler_params=_params(),
        name="attn_" + mode,
    )(*args)


def _post_kernel(*refs, pool, tm, seq):
    x_ref, at_ref, mod_ref, gains_ref, wo_ref, wfi_ref, wfo_ref = refs[:7]
    pos = 7
    if pool:
        u_ref, uprev_ref, unext_ref, wpool_ref, pscale_ref = refs[pos:pos + 5]
        pos += 5
    o_ref = refs[pos]
    hid_ref = refs[pos + 1]
    if pool:
        ext_ref = refs[pos + 2]

    n_a = at_ref.shape[1]
    if pool:
        t = pl.program_id(1)
        last = pl.num_programs(1) - 1
        halo = jnp.zeros((POOL_HALO, ext_ref.shape[1]), F32)
        ext_ref[0:POOL_HALO, :] = jnp.where(t > 0, uprev_ref[0], halo)
        ext_ref[POOL_HALO:POOL_HALO + tm, :] = u_ref[0]
        ext_ref[POOL_HALO + tm:, :] = jnp.where(t < last, unext_ref[0], halo)

    def pool_mix(r0, rows):
        n_ext = rows + 2 * POOL_HALO
        tok = t * tm + r0 + lax.broadcasted_iota(jnp.int32, (rows, 1), 0)

        def ahead(v, k):
            return pltpu.roll(v, n_ext - k, 0)

        def behind(v, k):
            return pltpu.roll(v, k, 0)

        mixed = []
        for g, w in enumerate(POOL_WINDOWS):
            sl = slice(g * LANES, (g + 1) * LANES)
            e = ext_ref[r0:r0 + n_ext, sl]
            run, length = e, 1
            while length < w // 2:
                run, length = run + ahead(run, length), 2 * length
            total = run + behind(run, w // 2)
            total = total[POOL_HALO:POOL_HALO + rows]
            cnt = jnp.minimum(tok + (w - w // 2), seq) - jnp.maximum(tok - w // 2, 0)
            diff = total / cnt.astype(F32) - e[POOL_HALO:POOL_HALO + rows]
            mixed.append((_dot(diff.astype(BF16), wpool_ref[g]) * pscale_ref[:, sl]).astype(BF16))
        return jnp.concatenate(mixed, axis=-1)

    def attn_out(r0, rows):
        return _dot_tn(at_ref[0, :, r0:r0 + rows], wo_ref[0:n_a, :])

    def pool_out(r0, rows, y):
        return y + _dot(pool_mix(r0, rows), wo_ref[n_a:, :]) if pool else y

    def residual_and_norm(r0, rows, y):
        x1 = x_ref[0, r0:r0 + rows, :] + mod_ref[0, 2:3, :] * (_rms_normalise(y) * gains_ref[1:2, :])
        h = _rms_normalise(x1) * gains_ref[2:3, :]
        return x1, (h * (1.0 + mod_ref[0, 4:5, :]) + mod_ref[0, 3:4, :]).astype(BF16)

    def ffn_hidden(r0, rows, h):
        for i in range(FFN_HIDDEN // FFN_CHUNK):
            gate = _dot(h, wfi_ref[:, i * FFN_CHUNK:(i + 1) * FFN_CHUNK])
            up = _dot(h, wfi_ref[:, FFN_HIDDEN + i * FFN_CHUNK:FFN_HIDDEN + (i + 1) * FFN_CHUNK])
            hid_ref[r0:r0 + rows, i * FFN_CHUNK:(i + 1) * FFN_CHUNK] = (_silu(gate) * up).astype(BF16)

    def ffn_out(r0, rows, x1):
        z = _dot(hid_ref[r0:r0 + rows, :], wfo_ref[...])
        o_ref[0, r0:r0 + rows, :] = x1 + mod_ref[0, 5:6, :] * (_rms_normalise(z) * gains_ref[3:4, :])

    rows = min(tm // 2, POST_SUB_TILE)
    starts = list(range(0, tm, rows))
    n = len(starts)
    y = {p: pool_out(starts[p], rows, attn_out(starts[p], rows)) for p in range(min(2, n))}
    normed = {0: residual_and_norm(starts[0], rows, y.pop(0))}
    for p in range(n):
        x1, h = normed.pop(p)
        ffn_hidden(starts[p], rows, h)
        if p + 2 < n:
            y[p + 2] = pool_out(starts[p + 2], rows, attn_out(starts[p + 2], rows))
        if p + 1 < n:
            normed[p + 1] = residual_and_norm(starts[p + 1], rows, y.pop(p + 1))
        ffn_out(starts[p], rows, x1)


def _post(x, a_t, mod, gains, wo, wfi, wfo, pool_args, *, tm):
    b, s, d = x.shape
    per_batch_mod = mod.shape[0] > 1
    pool = pool_args is not None
    in_specs = [
        pl.BlockSpec((1, tm, d), lambda i, t: (i, t, 0)),
        pl.BlockSpec((1, a_t.shape[1], tm), lambda i, t: (i, 0, t)),
        pl.BlockSpec((1, SUBLANES, d), (lambda i, t: (i, 0, 0)) if per_batch_mod else (lambda i, t: (0, 0, 0))),
        _const_spec(gains.shape), _const_spec(wo.shape), _const_spec(wfi.shape), _const_spec(wfo.shape),
    ]
    args = [x, a_t, mod, gains, wo, wfi, wfo]
    scratch = [pltpu.VMEM((tm, FFN_HIDDEN), BF16)]
    if pool:
        u, wpool, pscale = pool_args
        nu = u.shape[2]
        per_tile = tm // POOL_HALO
        n_halo_blocks = s // POOL_HALO
        in_specs += [
            pl.BlockSpec((1, tm, nu), lambda i, t: (i, t, 0)),
            pl.BlockSpec((1, POOL_HALO, nu), lambda i, t: (i, jnp.maximum(t * per_tile - 1, 0), 0)),
            pl.BlockSpec((1, POOL_HALO, nu),
                         lambda i, t: (i, jnp.minimum((t + 1) * per_tile, n_halo_blocks - 1), 0)),
            _const_spec(wpool.shape), _const_spec(pscale.shape),
        ]
        args += [u, u, u, wpool, pscale]
        scratch.append(pltpu.VMEM((tm + 2 * POOL_HALO, nu), F32))
    return pl.pallas_call(
        functools.partial(_post_kernel, pool=pool, tm=tm, seq=s),
        grid=(b, s // tm),
        in_specs=in_specs,
        out_specs=pl.BlockSpec((1, tm, d), lambda i, t: (i, t, 0)),
        out_shape=jax.ShapeDtypeStruct((b, s, d), F32),
        scratch_shapes=scratch,
        compiler_params=_params(),
        name="post_even" if pool else "post_odd",
    )(*args)


def _pad_rows(a, rows):
    return jnp.pad(a, [(0, 0)] * (a.ndim - 2) + [(0, rows - a.shape[-2]), (0, 0)])


def kernel(x, c, ctx, c_ctx, w_mod, b_mod, g_pre_mix, g_post_mix, g_pre_ffn, g_post_ffn, we_in, we_out,
           we_q_gain, we_k_gain, we_pool, we_pool_scale, wo_in, wo_out, wo_sink, w_ffn_in, w_ffn_out):
    batch, seq, d = x.shape
    c_len = ctx.shape[1]
    depth = w_mod.shape[0]
    tm = TOKEN_TILE
    tm_ctx = c_len
    assert d == D_MODEL and w_ffn_out.shape[1] == FFN_HIDDEN and w_mod.shape[2] % MOD_COLS_TILE == 0
    assert seq % PRE_TILE == 0 and seq % TOKEN_TILE == 0 and seq % QUERY_TILE == 0 and seq % GRID_W == 0
    assert seq >= QUERY_SUB + 2 * WINDOW and c_len % PRE_SUB_TILE == 0 and c_len % QUERY_SUB == 0
    assert max(POOL_WINDOWS) // 2 <= POOL_HALO

    cc = jnp.concatenate([c, c_ctx[None, :]], axis=0)
    cc = _pad_rows(cc, -(-(batch + 1) // SUBLANES) * SUBLANES)
    mods = _modulation(cc, w_mod, b_mod)

    cos, sin = _rope_tables(seq)
    _, lane_dim = _lane_slot_dim()
    head_mean = jnp.asarray(_head_mean_matrix(), BF16)

    for l in range(depth):
        even = l % 2 == 0
        i = l // 2
        with_ctx = l < depth - 1
        mod_x = _pad_rows(mods[l, :batch].reshape(batch, 6, d), SUBLANES)
        mod_c = _pad_rows(mods[l, batch:batch + 1].reshape(1, 6, d), SUBLANES)
        gains = _pad_rows(jnp.stack([g_pre_mix[l], g_post_mix[l], g_pre_ffn[l], g_post_ffn[l]]), SUBLANES)
        wfi = w_ffn_in[l].astype(BF16)
        wfo = w_ffn_out[l].astype(BF16)

        if even:
            n_heads, n_kv = 8, 2
            w_full, w_out = we_in[i], we_out[i].astype(BF16)
            norm_args = (head_mean, we_q_gain[i][lane_dim][None, :], we_k_gain[i][lane_dim][None, :])
            sink = None
            mode = "dense"
        else:
            n_heads, n_kv = 16, 4
            w_full, w_out = wo_in[i], wo_out[i].astype(BF16)
            norm_args = None
            sink = wo_sink[i]
            mode = "window"
        q_w, kv_w = n_heads * HEAD_DIM, n_kv * HEAD_DIM
        cols = np.concatenate([_q_cols(n_heads), _k_cols(n_kv, q_w), np.arange(q_w + kv_w, w_full.shape[1])])
        w_in = w_full[:, cols].astype(BF16)
        widths = (q_w, kv_w, kv_w, w_full.shape[1] - q_w - 2 * kv_w)

        lat = _pre(x, mod_x, gains, w_in, norm_args, (cos, sin), widths=widths, tm=PRE_TILE)
        con = _pre(ctx, mod_c, gains, w_in, norm_args, None, widths=widths, tm=tm_ctx)
        a_t = _attention(lat[0], (lat[1], lat[2]), (con[1], con[2]), sink, mode=mode, tq=QUERY_TILE)
        pool_w = (we_pool[i].astype(BF16), we_pool_scale[i][None, :]) if even else None
        x = _post(x, a_t, mod_x, gains, w_out, wfi, wfo, (lat[3],) + pool_w if even else None, tm=tm)
        if with_ctx:
            ac_t = _attention(con[0], None, (con[1], con[2]), sink, mode="ctx", tq=c_len)
            ctx = _post(ctx, ac_t, mod_c, gains, w_out, wfi, wfo,
                        (con[3],) + pool_w if even else None, tm=tm_ctx)
    return x
```

```python
import functools
import math

import numpy as np
import jax
import jax.numpy as jnp
from jax import lax
from jax.experimental import pallas as pl
from jax.experimental.pallas import tpu as pltpu

D_MODEL = 1024
HEAD_DIM = 64
GQA_GROUP = 4
GRID_W = 64
ROPE_THETA = 10000.0
EPS = 1e-6
WINDOW = 128
POOL_WINDOWS = (2, 4, 8, 16)
POOL_HALO = 8
FFN_HIDDEN = 2816
FFN_CHUNK = 256
LANES = 128
SUBLANES = 8
ROPE_HALF = HEAD_DIM // 2
LOG2E = math.log2(math.e)
Q_SCALE = LOG2E / math.sqrt(HEAD_DIM)
VMEM_LIMIT = 56 * 1024 * 1024
MOD_COLS_TILE = 1536
TOKEN_TILE = 512
POST_SUB_TILE = 256
PRE_TILE = 1024
PRE_SUB_TILE = 256
QUERY_TILE = 1024
QUERY_SUB = LANES
ITEM_CHUNKS = {"dense": 2, "window": 4, "ctx": 2}
KEY_BLOCK = 1024
VALUE_GROUP = 1024
VT_ONES_ROWS = 16
VT_CHUNK = LANES + VT_ONES_ROWS

F32 = jnp.float32
BF16 = jnp.bfloat16


def _lane_slot_dim():
    quarter = HEAD_DIM // 4
    lane = np.arange(LANES)
    part = lane // ROPE_HALF
    i = lane % ROPE_HALF
    dim = np.where(i < quarter, i, ROPE_HALF + (i - quarter)) + np.where(part >= 2, quarter, 0)
    return part % 2, dim


def _chunk_heads(c):
    head_a = (c // GQA_GROUP) * 2 * GQA_GROUP + c % GQA_GROUP
    return head_a, head_a + GQA_GROUP


def _q_cols(n_heads):
    slot, dim = _lane_slot_dim()
    cols = []
    for c in range(n_heads // 2):
        head_a, head_b = _chunk_heads(c)
        cols.append(np.where(slot == 0, head_a, head_b) * HEAD_DIM + dim)
    return np.concatenate(cols)


def _k_cols(n_kv, base):
    slot, dim = _lane_slot_dim()
    return np.concatenate([base + (2 * m + slot) * HEAD_DIM + dim for m in range(n_kv // 2)])


def _head_mean_matrix():
    slot, _ = _lane_slot_dim()
    mean = (slot[:, None] == slot[None, :]).astype(np.float32) / HEAD_DIM
    return np.concatenate([mean, mean], axis=0)


def _rope_tables(seq):
    quarter = HEAD_DIM // 4
    freqs = ROPE_THETA ** (-jnp.arange(quarter, dtype=F32) / quarter)
    t = jnp.arange(seq, dtype=jnp.int32)
    rows = (t // GRID_W).astype(F32)[:, None] * freqs[None, :]
    cols = (t % GRID_W).astype(F32)[:, None] * freqs[None, :]
    ang = jnp.concatenate([rows, cols], axis=-1)
    cos = jnp.tile(jnp.cos(ang), (1, 4))
    sin = jnp.sin(ang)
    sin = jnp.concatenate([-sin, -sin, sin, sin], axis=-1)
    return cos, sin


def _rms_normalise(x):
    return x * lax.rsqrt(jnp.mean(x * x, axis=-1, keepdims=True) + EPS)


def _silu(x):
    return x * (1.0 / (1.0 + jnp.exp(-x)))


def _dot(a, b):
    return jnp.dot(a, b, preferred_element_type=F32)


def _dot_tn(a_t, b):
    return lax.dot_general(a_t, b, (((0,), (0,)), ((), ())), preferred_element_type=F32)


def _const_spec(shape):
    return pl.BlockSpec(shape, lambda *_: (0,) * len(shape), pipeline_mode=pl.Buffered(1))


def _params():
    return pltpu.CompilerParams(dimension_semantics=("parallel", "parallel"), vmem_limit_bytes=VMEM_LIMIT)


def _mod_kernel(c_ref, w_ref, b_ref, o_ref):
    s = _silu(c_ref[...]).astype(BF16)
    o_ref[0] = _dot(s, w_ref[0].astype(BF16)) + b_ref[0]


def _modulation(cc, w_mod, b_mod):
    depth, d, n = w_mod.shape
    rows = cc.shape[0]
    tn = MOD_COLS_TILE
    return pl.pallas_call(
        _mod_kernel,
        grid=(depth, n // tn),
        in_specs=[
            pl.BlockSpec((rows, d), lambda l, j: (0, 0)),
            pl.BlockSpec((1, d, tn), lambda l, j: (l, 0, j)),
            pl.BlockSpec((1, 1, tn), lambda l, j: (l, 0, j)),
        ],
        out_specs=pl.BlockSpec((1, rows, tn), lambda l, j: (l, 0, j)),
        out_shape=jax.ShapeDtypeStruct((depth, rows, n), F32),
        compiler_params=_params(),
        name="modulation",
    )(cc, w_mod, b_mod.reshape(depth, 1, n))


def _pre_kernel(*refs, n_q, n_k, n_v, n_u, qk_norm, rope):
    x_ref, mod_ref, gains_ref, w_ref = refs[:4]
    pos = 4
    if qk_norm:
        pm_ref, qg_ref, kg_ref = refs[pos:pos + 3]
        pos += 3
    if rope:
        cos_ref, sin_ref = refs[pos:pos + 2]
        pos += 2
    qt_ref, k_ref, vt_ref = refs[pos:pos + 3]
    u_ref = refs[pos + 3] if n_u else None

    def modulated(rows):
        h = _rms_normalise(x_ref[0, rows, :]) * gains_ref[0:1, :]
        return (h * (1.0 + mod_ref[0, 1:2, :]) + mod_ref[0, 0:1, :]).astype(BF16)

    def head_chunk(c, gain_ref, rows):
        if qk_norm:
            c2 = c * c
            hi = c2.astype(BF16)
            lo = (c2 - hi.astype(F32)).astype(BF16)
            ms = _dot(jnp.concatenate([hi, lo], axis=1), pm_ref[...])
            c = c * lax.rsqrt(ms + EPS) * gain_ref[...]
        if rope:
            c = c * cos_ref[rows, :] + pltpu.roll(c, LANES // 2, 1) * sin_ref[rows, :]
        return c

    def project(rows, h):
        p = _dot(h, w_ref[...])
        for j in range(n_q // LANES):
            sl = slice(j * LANES, (j + 1) * LANES)
            c = head_chunk(p[:, sl], qg_ref if qk_norm else None, rows) * Q_SCALE
            qt_ref[0, sl, rows] = c.T.astype(BF16)
        for j in range(n_k // LANES):
            sl = slice(j * LANES, (j + 1) * LANES)
            c = head_chunk(p[:, n_q + j * LANES:n_q + (j + 1) * LANES], kg_ref if qk_norm else None, rows)
            k_ref[0, rows, sl] = c.astype(BF16)
        ones = jnp.ones((VT_ONES_ROWS, p.shape[0]), BF16)
        for j in range(n_v // LANES):
            r0 = j * VT_CHUNK
            vt_ref[0, r0:r0 + LANES, rows] = (
                p[:, n_q + n_k + j * LANES:n_q + n_k + (j + 1) * LANES].T.astype(BF16))
            vt_ref[0, r0 + LANES:r0 + VT_CHUNK, rows] = ones
        if n_u:
            u_ref[0, rows, :] = p[:, n_q + n_k + n_v:]

    tm = x_ref.shape[1]
    step = min(tm, PRE_SUB_TILE)
    parts = [slice(r, r + step) for r in range(0, tm, step)]
    h_next = modulated(parts[0])
    for i, rows in enumerate(parts):
        h_cur = h_next
        if i + 1 < len(parts):
            h_next = modulated(parts[i + 1])
        project(rows, h_cur)


def _pre(x, mod, gains, w, norm_args, rope_args, *, widths, tm):
    n_q, n_k, n_v, n_u = widths
    b, s, d = x.shape
    per_batch_mod = mod.shape[0] > 1
    qk_norm = norm_args is not None
    rope = rope_args is not None
    in_specs = [
        pl.BlockSpec((1, tm, d), lambda i, t: (i, t, 0)),
        pl.BlockSpec((1, SUBLANES, d), (lambda i, t: (i, 0, 0)) if per_batch_mod else (lambda i, t: (0, 0, 0))),
        _const_spec(gains.shape),
        _const_spec(w.shape),
    ]
    args = [x, mod, gains, w]
    if qk_norm:
        in_specs += [_const_spec(a.shape) for a in norm_args]
        args += list(norm_args)
    if rope:
        in_specs += [pl.BlockSpec((tm, LANES), lambda i, t: (t, 0))] * 2
        args += list(rope_args)
    vt_rows = n_v // LANES * VT_CHUNK
    out_shape = [jax.ShapeDtypeStruct((b, n_q, s), BF16), jax.ShapeDtypeStruct((b, s, n_k), BF16),
                 jax.ShapeDtypeStruct((b, vt_rows, s), BF16)]
    out_specs = [pl.BlockSpec((1, n_q, tm), lambda i, t: (i, 0, t)),
                 pl.BlockSpec((1, tm, n_k), lambda i, t: (i, t, 0)),
                 pl.BlockSpec((1, vt_rows, tm), lambda i, t: (i, 0, t))]
    if n_u:
        out_shape.append(jax.ShapeDtypeStruct((b, s, n_u), F32))
        out_specs.append(pl.BlockSpec((1, tm, n_u), lambda i, t: (i, t, 0)))
    return pl.pallas_call(
        functools.partial(_pre_kernel, n_q=n_q, n_k=n_k, n_v=n_v, n_u=n_u, qk_norm=qk_norm, rope=rope),
        grid=(b, s // tm),
        in_specs=in_specs,
        out_specs=out_specs,
        out_shape=out_shape,
        compiler_params=_params(),
        name="pre_even" if qk_norm else "pre_odd",
    )(*args)


def _attn_kernel(*refs, mode, n_pairs, n_sub, tq, seq, use_sink, cpi):
    qt_ref = refs[0]
    pos = 1
    if mode != "ctx":
        k_ref, vt_ref = refs[pos:pos + 2]
        pos += 2
    kc_ref, vct_ref = refs[pos:pos + 2]
    pos += 2
    if use_sink:
        sink_ref = refs[pos]
        pos += 1
    o_ref, s_ref, p_ref = refs[pos:pos + 3]

    sub = QUERY_SUB
    row = lax.broadcasted_iota(jnp.int32, (LANES, 1), 0)
    slot_a = (row // ROPE_HALF) % 2 == 0
    band = sub + 2 * WINDOW
    if mode == "window":
        rel = (lax.broadcasted_iota(jnp.int32, (band, sub), 0)
               - lax.broadcasted_iota(jnp.int32, (band, sub), 1))

    def block_geometry(sb):
        q0 = pl.program_id(1) * tq + sb * sub
        start = pl.multiple_of(jnp.clip(q0 - WINDOW, 0, seq - band), LANES)
        bias = jnp.where(jnp.abs(rel + (start - q0)) <= WINDOW, 0.0, -jnp.inf).astype(F32)
        return start, jnp.concatenate([bias] * (2 * cpi), axis=1)

    c_len = kc_ref.shape[1]
    first = {"dense": seq, "window": band, "ctx": 0}[mode]
    blocks = [(True, r, min(KEY_BLOCK, first - r), r) for r in range(0, first, KEY_BLOCK)]
    blocks += [(False, r, min(KEY_BLOCK, c_len - r), first + r) for r in range(0, c_len, KEY_BLOCK)]
    value_groups = {}
    for bi, (latent, r0, rows, b0) in enumerate(blocks):
        if r0 % VALUE_GROUP:
            _, g0, g_rows, gb0 = value_groups.pop(bi - 1)
            value_groups[bi] = (latent, g0, g_rows + rows, gb0)
        else:
            value_groups[bi] = (latent, r0, rows, b0)

    def item_chunks(item):
        return range(cpi * item, cpi * (item + 1))

    def item_pair(item):
        return cpi * item // GQA_GROUP

    def item_heads(item):
        return sum((_chunk_heads(c) for c in item_chunks(item)), ())

    def query_operand(sb, item):
        cols = []
        for c in item_chunks(item):
            qt = qt_ref[0, c * LANES:(c + 1) * LANES, sb * sub:(sb + 1) * sub]
            zero = jnp.zeros_like(qt)
            cols += [jnp.where(slot_a, qt, zero), jnp.where(slot_a, zero, qt)]
        return jnp.concatenate(cols, axis=1)

    def score_block(slot, item, geometry, rhs, blk, m):
        latent, r0, rows, b0 = blk
        psl = slice(item_pair(item) * LANES, (item_pair(item) + 1) * LANES)
        if not latent:
            s = _dot(kc_ref[0, r0:r0 + rows, psl], rhs)
        elif mode == "dense":
            s = _dot(k_ref[0, r0:r0 + rows, psl], rhs)
        else:
            start, bias = geometry
            s = _dot(k_ref[0, pl.ds(pl.multiple_of(start + r0, LANES), rows), psl], rhs) + bias[r0:r0 + rows]
        s_ref[slot, b0:b0 + rows, :] = s
        top = s.max(axis=0, keepdims=True)
        return top if m is None else jnp.maximum(m, top)

    def exp_block(slot, blk, m):
        _, _, rows, b0 = blk
        p_ref[slot, b0:b0 + rows, :] = jnp.exp2(s_ref[slot, b0:b0 + rows, :] - m).astype(BF16)

    def value_block(slot, item, geometry, blk, acc):
        latent, r0, rows, b0 = blk
        pair = item_pair(item)
        vsl = slice(pair * VT_CHUNK, (pair + 1) * VT_CHUNK)
        if not latent:
            v_t = vct_ref[0, vsl, r0:r0 + rows]
        elif mode == "dense":
            v_t = vt_ref[0, vsl, r0:r0 + rows]
        else:
            v_t = vt_ref[0, vsl, pl.ds(pl.multiple_of(geometry[0] + r0, LANES), rows)]
        pv = _dot(v_t, p_ref[slot, b0:b0 + rows, :])
        return pv if acc is None else acc + pv

    def with_sink(item, m):
        if not use_sink:
            return m, None
        sink = jnp.concatenate([jnp.full((1, sub), sink_ref[h], F32) for h in item_heads(item)], axis=1) * LOG2E
        m = jnp.maximum(m, sink)
        return m, jnp.exp2(sink - m)

    def store_output(sb, item, o4, sink_term):
        denom = o4[LANES:LANES + 1, :]
        if use_sink:
            denom = denom + sink_term
        o4 = o4[:LANES, :] * (1.0 / denom)
        for j, h in enumerate(item_heads(item)):
            rows = slice(0, HEAD_DIM) if j % 2 == 0 else slice(HEAD_DIM, 2 * HEAD_DIM)
            o_ref[0, h * HEAD_DIM:(h + 1) * HEAD_DIM, sb * sub:(sb + 1) * sub] = (
                o4[rows, j * sub:(j + 1) * sub].astype(BF16))

    items = [(sb, it) for sb in range(n_sub) for it in range(GQA_GROUP * n_pairs // cpi)]
    geometries = [block_geometry(sb) if mode == "window" else None for sb in range(n_sub)]
    n_items = len(items)
    m_cur = sink_cur = sink_prev = None
    for i in range(-1, n_items + 1):
        nxt, prv = i + 1, i - 1
        if nxt < n_items:
            rhs = query_operand(*items[nxt])
        m_next = o_acc = None
        for bi, blk in enumerate(blocks):
            if nxt < n_items:
                m_next = score_block(nxt % 2, items[nxt][1], geometries[items[nxt][0]], rhs, blk, m_next)
            if 0 <= i < n_items:
                exp_block(i % 2, blk, m_cur)
            if 0 <= prv and bi in value_groups:
                o_acc = value_block(prv % 2, items[prv][1], geometries[items[prv][0]], value_groups[bi], o_acc)
        if 0 <= prv:
            store_output(*items[prv], o_acc, sink_prev)
        sink_prev = sink_cur
        if nxt < n_items:
            m_cur, sink_cur = with_sink(items[nxt][1], m_next)


def _attention(qt, kv, kv_ctx, sink, *, mode, tq):
    b, wq, sq = qt.shape
    n_pairs = wq // (2 * GQA_GROUP * HEAD_DIM)
    kc, vct = kv_ctx
    c_len = kc.shape[1]
    use_sink = sink is not None
    in_specs = [pl.BlockSpec((1, wq, tq), lambda i, t: (i, 0, t))]
    args = [qt]
    seq = sq
    if mode != "ctx":
        k, vt = kv
        seq = k.shape[1]
        in_specs += [pl.BlockSpec((1, seq, k.shape[2]), lambda i, t: (i, 0, 0)),
                     pl.BlockSpec((1, vt.shape[1], seq), lambda i, t: (i, 0, 0))]
        args += [k, vt]
    in_specs += [pl.BlockSpec((1, c_len, kc.shape[2]), lambda i, t: (i, 0, 0)),
                 pl.BlockSpec((1, vct.shape[1], c_len), lambda i, t: (i, 0, 0))]
    args += [kc, vct]
    if use_sink:
        in_specs.append(pl.BlockSpec(memory_space=pltpu.SMEM))
        args.append(sink)
    n_keys = c_len + {"dense": seq, "window": QUERY_SUB + 2 * WINDOW, "ctx": 0}[mode]
    cpi = ITEM_CHUNKS[mode]
    return pl.pallas_call(
        functools.partial(_attn_kernel, mode=mode, n_pairs=n_pairs, n_sub=tq // QUERY_SUB, tq=tq, seq=seq,
                          use_sink=use_sink, cpi=cpi),
        grid=(b, sq // tq),
        in_specs=in_specs,
        out_specs=pl.BlockSpec((1, wq, tq), lambda i, t: (i, 0, t)),
        out_shape=jax.ShapeDtypeStruct((b, wq, sq), BF16),
        scratch_shapes=[pltpu.VMEM((2, n_keys, 2 * cpi * QUERY_SUB), F32),
                        pltpu.VMEM((2, n_keys, 2 * cpi * QUERY_SUB), BF16)],
        compiler_params=_params(),
        name="attn_" + mode,
    )(*args)


def _post_kernel(*refs, pool, tm, seq):
    x_ref, at_ref, mod_ref, gains_ref, wo_ref, wfi_ref, wfo_ref = refs[:7]
    pos = 7
    if pool:
        u_ref, uprev_ref, unext_ref, wpool_ref, pscale_ref = refs[pos:pos + 5]
        pos += 5
    o_ref = refs[pos]
    hid_ref = refs[pos + 1]
    if pool:
        ext_ref = refs[pos + 2]

    n_a = at_ref.shape[1]
    if pool:
        t = pl.program_id(1)
        last = pl.num_programs(1) - 1
        halo = jnp.zeros((POOL_HALO, ext_ref.shape[1]), F32)
        ext_ref[0:POOL_HALO, :] = jnp.where(t > 0, uprev_ref[0], halo)
        ext_ref[POOL_HALO:POOL_HALO + tm, :] = u_ref[0]
        ext_ref[POOL_HALO + tm:, :] = jnp.where(t < last, unext_ref[0], halo)

    def pool_mix(r0, rows):
        n_ext = rows + 2 * POOL_HALO
        tok = t * tm + r0 + lax.broadcasted_iota(jnp.int32, (rows, 1), 0)

        def ahead(v, k):
            return pltpu.roll(v, n_ext - k, 0)

        def behind(v, k):
            return pltpu.roll(v, k, 0)

        mixed = []
        for g, w in enumerate(POOL_WINDOWS):
            sl = slice(g * LANES, (g + 1) * LANES)
            e = ext_ref[r0:r0 + n_ext, sl]
            run, length = e, 1
            while length < w // 2:
                run, length = run + ahead(run, length), 2 * length
            total = run + behind(run, w // 2)
            total = total[POOL_HALO:POOL_HALO + rows]
            cnt = jnp.minimum(tok + (w - w // 2), seq) - jnp.maximum(tok - w // 2, 0)
            diff = total / cnt.astype(F32) - e[POOL_HALO:POOL_HALO + rows]
            mixed.append((_dot(diff.astype(BF16), wpool_ref[g]) * pscale_ref[:, sl]).astype(BF16))
        return jnp.concatenate(mixed, axis=-1)

    def attn_out(r0, rows):
        return _dot_tn(at_ref[0, :, r0:r0 + rows], wo_ref[0:n_a, :])

    def pool_out(r0, rows, y):
        return y + _dot(pool_mix(r0, rows), wo_ref[n_a:, :]) if pool else y

    def residual_and_norm(r0, rows, y):
        x1 = x_ref[0, r0:r0 + rows, :] + mod_ref[0, 2:3, :] * (_rms_normalise(y) * gains_ref[1:2, :])
        h = _rms_normalise(x1) * gains_ref[2:3, :]
        return x1, (h * (1.0 + mod_ref[0, 4:5, :]) + mod_ref[0, 3:4, :]).astype(BF16)

    def ffn_hidden(r0, rows, h):
        for i in range(FFN_HIDDEN // FFN_CHUNK):
            gate = _dot(h, wfi_ref[:, i * FFN_CHUNK:(i + 1) * FFN_CHUNK])
            up = _dot(h, wfi_ref[:, FFN_HIDDEN + i * FFN_CHUNK:FFN_HIDDEN + (i + 1) * FFN_CHUNK])
            hid_ref[r0:r0 + rows, i * FFN_CHUNK:(i + 1) * FFN_CHUNK] = (_silu(gate) * up).astype(BF16)

    def ffn_out(r0, rows, x1):
        z = _dot(hid_ref[r0:r0 + rows, :], wfo_ref[...])
        o_ref[0, r0:r0 + rows, :] = x1 + mod_ref[0, 5:6, :] * (_rms_normalise(z) * gains_ref[3:4, :])

    rows = min(tm // 2, POST_SUB_TILE)
    starts = list(range(0, tm, rows))
    n = len(starts)
    y = {p: pool_out(starts[p], rows, attn_out(starts[p], rows)) for p in range(min(2, n))}
    normed = {0: residual_and_norm(starts[0], rows, y.pop(0))}
    for p in range(n):
        x1, h = normed.pop(p)
        ffn_hidden(starts[p], rows, h)
        if p + 2 < n:
            y[p + 2] = pool_out(starts[p + 2], rows, attn_out(starts[p + 2], rows))
        if p + 1 < n:
            normed[p + 1] = residual_and_norm(starts[p + 1], rows, y.pop(p + 1))
        ffn_out(starts[p], rows, x1)


def _post(x, a_t, mod, gains, wo, wfi, wfo, pool_args, *, tm):
    b, s, d = x.shape
    per_batch_mod = mod.shape[0] > 1
    pool = pool_args is not None
    in_specs = [
        pl.BlockSpec((1, tm, d), lambda i, t: (i, t, 0)),
        pl.BlockSpec((1, a_t.shape[1], tm), lambda i, t: (i, 0, t)),
        pl.BlockSpec((1, SUBLANES, d), (lambda i, t: (i, 0, 0)) if per_batch_mod else (lambda i, t: (0, 0, 0))),
        _const_spec(gains.shape), _const_spec(wo.shape), _const_spec(wfi.shape), _const_spec(wfo.shape),
    ]
    args = [x, a_t, mod, gains, wo, wfi, wfo]
    scratch = [pltpu.VMEM((tm, FFN_HIDDEN), BF16)]
    if pool:
        u, wpool, pscale = pool_args
        nu = u.shape[2]
        per_tile = tm // POOL_HALO
        n_halo_blocks = s // POOL_HALO
        in_specs += [
            pl.BlockSpec((1, tm, nu), lambda i, t: (i, t, 0)),
            pl.BlockSpec((1, POOL_HALO, nu), lambda i, t: (i, jnp.maximum(t * per_tile - 1, 0), 0)),
            pl.BlockSpec((1, POOL_HALO, nu),
                         lambda i, t: (i, jnp.minimum((t + 1) * per_tile, n_halo_blocks - 1), 0)),
            _const_spec(wpool.shape), _const_spec(pscale.shape),
        ]
        args += [u, u, u, wpool, pscale]
        scratch.append(pltpu.VMEM((tm + 2 * POOL_HALO, nu), F32))
    return pl.pallas_call(
        functools.partial(_post_kernel, pool=pool, tm=tm, seq=s),
        grid=(b, s // tm),
        in_specs=in_specs,
        out_specs=pl.BlockSpec((1, tm, d), lambda i, t: (i, t, 0)),
        out_shape=jax.ShapeDtypeStruct((b, s, d), F32),
        scratch_shapes=scratch,
        compiler_params=_params(),
        name="post_even" if pool else "post_odd",
    )(*args)


def _pad_rows(a, rows):
    return jnp.pad(a, [(0, 0)] * (a.ndim - 2) + [(0, rows - a.shape[-2]), (0, 0)])


def kernel(x, c, ctx, c_ctx, w_mod, b_mod, g_pre_mix, g_post_mix, g_pre_ffn, g_post_ffn, we_in, we_out,
           we_q_gain, we_k_gain, we_pool, we_pool_scale, wo_in, wo_out, wo_sink, w_ffn_in, w_ffn_out):
    batch, seq, d = x.shape
    c_len = ctx.shape[1]
    depth = w_mod.shape[0]
    tm = TOKEN_TILE
    tm_ctx = c_len
    assert d == D_MODEL and w_ffn_out.shape[1] == FFN_HIDDEN and w_mod.shape[2] % MOD_COLS_TILE == 0
    assert seq % PRE_TILE == 0 and seq % TOKEN_TILE == 0 and seq % QUERY_TILE == 0 and seq % GRID_W == 0
    assert seq >= QUERY_SUB + 2 * WINDOW and c_len % PRE_SUB_TILE == 0 and c_len % QUERY_SUB == 0
    assert max(POOL_WINDOWS) // 2 <= POOL_HALO

    cc = jnp.concatenate([c, c_ctx[None, :]], axis=0)
    cc = _pad_rows(cc, -(-(batch + 1) // SUBLANES) * SUBLANES)
    mods = _modulation(cc, w_mod, b_mod)

    cos, sin = _rope_tables(seq)
    _, lane_dim = _lane_slot_dim()
    head_mean = jnp.asarray(_head_mean_matrix(), BF16)

    for l in range(depth):
        even = l % 2 == 0
        i = l // 2
        with_ctx = l < depth - 1
        mod_x = _pad_rows(mods[l, :batch].reshape(batch, 6, d), SUBLANES)
        mod_c = _pad_rows(mods[l, batch:batch + 1].reshape(1, 6, d), SUBLANES)
        gains = _pad_rows(jnp.stack([g_pre_mix[l], g_post_mix[l], g_pre_ffn[l], g_post_ffn[l]]), SUBLANES)
        wfi = w_ffn_in[l].astype(BF16)
        wfo = w_ffn_out[l].astype(BF16)

        if even:
            n_heads, n_kv = 8, 2
            w_full, w_out = we_in[i], we_out[i].astype(BF16)
            norm_args = (head_mean, we_q_gain[i][lane_dim][None, :], we_k_gain[i][lane_dim][None, :])
            sink = None
            mode = "dense"
        else:
            n_heads, n_kv = 16, 4
            w_full, w_out = wo_in[i], wo_out[i].astype(BF16)
            norm_args = None
            sink = wo_sink[i]
            mode = "window"
        q_w, kv_w = n_heads * HEAD_DIM, n_kv * HEAD_DIM
        cols = np.concatenate([_q_cols(n_heads), _k_cols(n_kv, q_w), np.arange(q_w + kv_w, w_full.shape[1])])
        w_in = w_full[:, cols].astype(BF16)
        widths = (q_w, kv_w, kv_w, w_full.shape[1] - q_w - 2 * kv_w)

        lat = _pre(x, mod_x, gains, w_in, norm_args, (cos, sin), widths=widths, tm=PRE_TILE)
        con = _pre(ctx, mod_c, gains, w_in, norm_args, None, widths=widths, tm=tm_ctx)
        a_t = _attention(lat[0], (lat[1], lat[2]), (con[1], con[2]), sink, mode=mode, tq=QUERY_TILE)
        pool_w = (we_pool[i].astype(BF16), we_pool_scale[i][None, :]) if even else None
        x = _post(x, a_t, mod_x, gains, w_out, wfi, wfo, (lat[3],) + pool_w if even else None, tm=tm)
        if with_ctx:
            ac_t = _attention(con[0], None, (con[1], con[2]), sink, mode="ctx", tq=c_len)
            ctx = _post(ctx, ac_t, mod_c, gains, w_out, wfi, wfo,
                        (con[3],) + pool_w if even else None, tm=tm_ctx)
    return x
```

```python
import functools
import math

import numpy as np
import jax
import jax.numpy as jnp
from jax import lax
from jax.experimental import pallas as pl
from jax.experimental.pallas import tpu as pltpu

D_MODEL = 1024
HEAD_DIM = 64
GQA_GROUP = 4
GRID_W = 64
ROPE_THETA = 10000.0
EPS = 1e-6
WINDOW = 128
POOL_WINDOWS = (2, 4, 8, 16)
POOL_HALO = 8
FFN_HIDDEN = 2816
FFN_CHUNK = 256
LANES = 128
SUBLANES = 8
ROPE_HALF = HEAD_DIM // 2
LOG2E = math.log2(math.e)
Q_SCALE = LOG2E / math.sqrt(HEAD_DIM)
VMEM_LIMIT = 56 * 1024 * 1024
MOD_COLS_TILE = 1536
TOKEN_TILE = 512
POST_SUB_TILE = 256
PRE_TILE = 1024
PRE_SUB_TILE = 256
QUERY_TILE = 1024
QUERY_SUB = LANES
ITEM_CHUNKS = {"dense": 2, "window": 4, "ctx": 2}
KEY_BLOCK = 1024
VALUE_GROUP = 1024
VT_ONES_ROWS = 16
VT_CHUNK = LANES + VT_ONES_ROWS

F32 = jnp.float32
BF16 = jnp.bfloat16


def _lane_slot_dim():
    quarter = HEAD_DIM // 4
    lane = np.arange(LANES)
    part = lane // ROPE_HALF
    i = lane % ROPE_HALF
    dim = np.where(i < quarter, i, ROPE_HALF + (i - quarter)) + np.where(part >= 2, quarter, 0)
    return part % 2, dim


def _chunk_heads(c):
    head_a = (c // GQA_GROUP) * 2 * GQA_GROUP + c % GQA_GROUP
    return head_a, head_a + GQA_GROUP


def _q_cols(n_heads):
    slot, dim = _lane_slot_dim()
    cols = []
    for c in range(n_heads // 2):
        head_a, head_b = _chunk_heads(c)
        cols.append(np.where(slot == 0, head_a, head_b) * HEAD_DIM + dim)
    return np.concatenate(cols)


def _k_cols(n_kv, base):
    slot, dim = _lane_slot_dim()
    return np.concatenate([base + (2 * m + slot) * HEAD_DIM + dim for m in range(n_kv // 2)])


def _head_mean_matrix():
    slot, _ = _lane_slot_dim()
    mean = (slot[:, None] == slot[None, :]).astype(np.float32) / HEAD_DIM
    return np.concatenate([mean, mean], axis=0)


def _rope_tables(seq):
    quarter = HEAD_DIM // 4
    freqs = ROPE_THETA ** (-jnp.arange(quarter, dtype=F32) / quarter)
    t = jnp.arange(seq, dtype=jnp.int32)
    rows = (t // GRID_W).astype(F32)[:, None] * freqs[None, :]
    cols = (t % GRID_W).astype(F32)[:, None] * freqs[None, :]
    ang = jnp.concatenate([rows, cols], axis=-1)
    cos = jnp.tile(jnp.cos(ang), (1, 4))
    sin = jnp.sin(ang)
    sin = jnp.concatenate([-sin, -sin, sin, sin], axis=-1)
    return cos, sin


def _rms_normalise(x):
    return x * lax.rsqrt(jnp.mean(x * x, axis=-1, keepdims=True) + EPS)


def _silu(x):
    return x * (1.0 / (1.0 + jnp.exp(-x)))


def _dot(a, b):
    return jnp.dot(a, b, preferred_element_type=F32)


def _dot_tn(a_t, b):
    return lax.dot_general(a_t, b, (((0,), (0,)), ((), ())), preferred_element_type=F32)


def _const_spec(shape):
    return pl.BlockSpec(shape, lambda *_: (0,) * len(shape), pipeline_mode=pl.Buffered(1))


def _params():
    return pltpu.CompilerParams(dimension_semantics=("parallel", "parallel"), vmem_limit_bytes=VMEM_LIMIT)


def _mod_kernel(c_ref, w_ref, b_ref, o_ref):
    s = _silu(c_ref[...]).astype(BF16)
    o_ref[0] = _dot(s, w_ref[0].astype(BF16)) + b_ref[0]


def _modulation(cc, w_mod, b_mod):
    depth, d, n = w_mod.shape
    rows = cc.shape[0]
    tn = MOD_COLS_TILE
    return pl.pallas_call(
        _mod_kernel,
        grid=(depth, n // tn),
        in_specs=[
            pl.BlockSpec((rows, d), lambda l, j: (0, 0)),
            pl.BlockSpec((1, d, tn), lambda l, j: (l, 0, j)),
            pl.BlockSpec((1, 1, tn), lambda l, j: (l, 0, j)),
        ],
        out_specs=pl.BlockSpec((1, rows, tn), lambda l, j: (l, 0, j)),
        out_shape=jax.ShapeDtypeStruct((depth, rows, n), F32),
        compiler_params=_params(),
        name="modulation",
    )(cc, w_mod, b_mod.reshape(depth, 1, n))


def _pre_kernel(*refs, n_q, n_k, n_v, n_u, qk_norm, rope):
    x_ref, mod_ref, gains_ref, w_ref = refs[:4]
    pos = 4
    if qk_norm:
        pm_ref, qg_ref, kg_ref = refs[pos:pos + 3]
        pos += 3
    if rope:
        cos_ref, sin_ref = refs[pos:pos + 2]
        pos += 2
    qt_ref, k_ref, vt_ref = refs[pos:pos + 3]
    u_ref = refs[pos + 3] if n_u else None

    def modulated(rows):
        h = _rms_normalise(x_ref[0, rows, :]) * gains_ref[0:1, :]
        return (h * (1.0 + mod_ref[0, 1:2, :]) + mod_ref[0, 0:1, :]).astype(BF16)

    def head_chunk(c, gain_ref, rows):
        if qk_norm:
            c2 = c * c
            hi = c2.astype(BF16)
            lo = (c2 - hi.astype(F32)).astype(BF16)
            ms = _dot(jnp.concatenate([hi, lo], axis=1), pm_ref[...])
            c = c * lax.rsqrt(ms + EPS) * gain_ref[...]
        if rope:
            c = c * cos_ref[rows, :] + pltpu.roll(c, LANES // 2, 1) * sin_ref[rows, :]
        return c

    def project(rows, h):
        p = _dot(h, w_ref[...])
        for j in range(n_q // LANES):
            sl = slice(j * LANES, (j + 1) * LANES)
            c = head_chunk(p[:, sl], qg_ref if qk_norm else None, rows) * Q_SCALE
            qt_ref[0, sl, rows] = c.T.astype(BF16)
        for j in range(n_k // LANES):
            sl = slice(j * LANES, (j + 1) * LANES)
            c = head_chunk(p[:, n_q + j * LANES:n_q + (j + 1) * LANES], kg_ref if qk_norm else None, rows)
            k_ref[0, rows, sl] = c.astype(BF16)
        ones = jnp.ones((VT_ONES_ROWS, p.shape[0]), BF16)
        for j in range(n_v // LANES):
            r0 = j * VT_CHUNK
            vt_ref[0, r0:r0 + LANES, rows] = (
                p[:, n_q + n_k + j * LANES:n_q + n_k + (j + 1) * LANES].T.astype(BF16))
            vt_ref[0, r0 + LANES:r0 + VT_CHUNK, rows] = ones
        if n_u:
            u_ref[0, rows, :] = p[:, n_q + n_k + n_v:]

    tm = x_ref.shape[1]
    step = min(tm, PRE_SUB_TILE)
    parts = [slice(r, r + step) for r in range(0, tm, step)]
    h_next = modulated(parts[0])
    for i, rows in enumerate(parts):
        h_cur = h_next
        if i + 1 < len(parts):
            h_next = modulated(parts[i + 1])
        project(rows, h_cur)


def _pre(x, mod, gains, w, norm_args, rope_args, *, widths, tm):
    n_q, n_k, n_v, n_u = widths
    b, s, d = x.shape
    per_batch_mod = mod.shape[0] > 1
    qk_norm = norm_args is not None
    rope = rope_args is not None
    in_specs = [
        pl.BlockSpec((1, tm, d), lambda i, t: (i, t, 0)),
        pl.BlockSpec((1, SUBLANES, d), (lambda i, t: (i, 0, 0)) if per_batch_mod else (lambda i, t: (0, 0, 0))),
        _const_spec(gains.shape),
        _const_spec(w.shape),
    ]
    args = [x, mod, gains, w]
    if qk_norm:
        in_specs += [_const_spec(a.shape) for a in norm_args]
        args += list(norm_args)
    if rope:
        in_specs += [pl.BlockSpec((tm, LANES), lambda i, t: (t, 0))] * 2
        args += list(rope_args)
    vt_rows = n_v // LANES * VT_CHUNK
    out_shape = [jax.ShapeDtypeStruct((b, n_q, s), BF16), jax.ShapeDtypeStruct((b, s, n_k), BF16),
                 jax.ShapeDtypeStruct((b, vt_rows, s), BF16)]
    out_specs = [pl.BlockSpec((1, n_q, tm), lambda i, t: (i, 0, t)),
                 pl.BlockSpec((1, tm, n_k), lambda i, t: (i, t, 0)),
                 pl.BlockSpec((1, vt_rows, tm), lambda i, t: (i, 0, t))]
    if n_u:
        out_shape.append(jax.ShapeDtypeStruct((b, s, n_u), F32))
        out_specs.append(pl.BlockSpec((1, tm, n_u), lambda i, t: (i, t, 0)))
    return pl.pallas_call(
        functools.partial(_pre_kernel, n_q=n_q, n_k=n_k, n_v=n_v, n_u=n_u, qk_norm=qk_norm, rope=rope),
        grid=(b, s // tm),
        in_specs=in_specs,
        out_specs=out_specs,
        out_shape=out_shape,
        compiler_params=_params(),
        name="pre_even" if qk_norm else "pre_odd",
    )(*args)


def _attn_kernel(*refs, mode, n_pairs, n_sub, tq, seq, use_sink, cpi):
    qt_ref = refs[0]
    pos = 1
    if mode != "ctx":
        k_ref, vt_ref = refs[pos:pos + 2]
        pos += 2
    kc_ref, vct_ref = refs[pos:pos + 2]
    pos += 2
    if use_sink:
        sink_ref = refs[pos]
        pos += 1
    o_ref, s_ref, p_ref = refs[pos:pos + 3]

    sub = QUERY_SUB
    row = lax.broadcasted_iota(jnp.int32, (LANES, 1), 0)
    slot_a = (row // ROPE_HALF) % 2 == 0
    band = sub + 2 * WINDOW
    if mode == "window":
        rel = (lax.broadcasted_iota(jnp.int32, (band, sub), 0)
               - lax.broadcasted_iota(jnp.int32, (band, sub), 1))

    def block_geometry(sb):
        q0 = pl.program_id(1) * tq + sb * sub
        start = pl.multiple_of(jnp.clip(q0 - WINDOW, 0, seq - band), LANES)
        bias = jnp.where(jnp.abs(rel + (start - q0)) <= WINDOW, 0.0, -jnp.inf).astype(F32)
        return start, jnp.concatenate([bias] * (2 * cpi), axis=1)

    c_len = kc_ref.shape[1]
    first = {"dense": seq, "window": band, "ctx": 0}[mode]
    blocks = [(True, r, min(KEY_BLOCK, first - r), r) for r in range(0, first, KEY_BLOCK)]
    blocks += [(False, r, min(KEY_BLOCK, c_len - r), first + r) for r in range(0, c_len, KEY_BLOCK)]
    value_groups = {}
    for bi, (latent, r0, rows, b0) in enumerate(blocks):
        if r0 % VALUE_GROUP:
            _, g0, g_rows, gb0 = value_groups.pop(bi - 1)
            value_groups[bi] = (latent, g0, g_rows + rows, gb0)
        else:
            value_groups[bi] = (latent, r0, rows, b0)

    def item_chunks(item):
        return range(cpi * item, cpi * (item + 1))

    def item_pair(item):
        return cpi * item // GQA_GROUP

    def item_heads(item):
        return sum((_chunk_heads(c) for c in item_chunks(item)), ())

    def query_operand(sb, item):
        cols = []
        for c in item_chunks(item):
            qt = qt_ref[0, c * LANES:(c + 1) * LANES, sb * sub:(sb + 1) * sub]
            zero = jnp.zeros_like(qt)
            cols += [jnp.where(slot_a, qt, zero), jnp.where(slot_a, zero, qt)]
        return jnp.concatenate(cols, axis=1)

    def score_block(slot, item, geometry, rhs, blk, m):
        latent, r0, rows, b0 = blk
        psl = slice(item_pair(item) * LANES, (item_pair(item) + 1) * LANES)
        if not latent:
            s = _dot(kc_ref[0, r0:r0 + rows, psl], rhs)
        elif mode == "dense":
            s = _dot(k_ref[0, r0:r0 + rows, psl], rhs)
        else:
            start, bias = geometry
            s = _dot(k_ref[0, pl.ds(pl.multiple_of(start + r0, LANES), rows), psl], rhs) + bias[r0:r0 + rows]
        s_ref[slot, b0:b0 + rows, :] = s
        top = s.max(axis=0, keepdims=True)
        return top if m is None else jnp.maximum(m, top)

    def exp_block(slot, blk, m):
        _, _, rows, b0 = blk
        p_ref[slot, b0:b0 + rows, :] = jnp.exp2(s_ref[slot, b0:b0 + rows, :] - m).astype(BF16)

    def value_block(slot, item, geometry, blk, acc):
        latent, r0, rows, b0 = blk
        pair = item_pair(item)
        vsl = slice(pair * VT_CHUNK, (pair + 1) * VT_CHUNK)
        if not latent:
            v_t = vct_ref[0, vsl, r0:r0 + rows]
        elif mode == "dense":
            v_t = vt_ref[0, vsl, r0:r0 + rows]
        else:
            v_t = vt_ref[0, vsl, pl.ds(pl.multiple_of(geometry[0] + r0, LANES), rows)]
        pv = _dot(v_t, p_ref[slot, b0:b0 + rows, :])
        return pv if acc is None else acc + pv

    def with_sink(item, m):
        if not use_sink:
            return m, None
        sink = jnp.concatenate([jnp.full((1, sub), sink_ref[h], F32) for h in item_heads(item)], axis=1) * LOG2E
        m = jnp.maximum(m, sink)
        return m, jnp.exp2(sink - m)

    def store_output(sb, item, o4, sink_term):
        denom = o4[LANES:LANES + 1, :]
        if use_sink:
            denom = denom + sink_term
        o4 = o4[:LANES, :] * (1.0 / denom)
        for j, h in enumerate(item_heads(item)):
            rows = slice(0, HEAD_DIM) if j % 2 == 0 else slice(HEAD_DIM, 2 * HEAD_DIM)
            o_ref[0, h * HEAD_DIM:(h + 1) * HEAD_DIM, sb * sub:(sb + 1) * sub] = (
                o4[rows, j * sub:(j + 1) * sub].astype(BF16))

    items = [(sb, it) for sb in range(n_sub) for it in range(GQA_GROUP * n_pairs // cpi)]
    geometries = [block_geometry(sb) if mode == "window" else None for sb in range(n_sub)]
    n_items = len(items)
    m_cur = sink_cur = sink_prev = None
    for i in range(-1, n_items + 1):
        nxt, prv = i + 1, i - 1
        if nxt < n_items:
            rhs = query_operand(*items[nxt])
        m_next = o_acc = None
        for bi, blk in enumerate(blocks):
            if 0 <= prv and bi in value_groups and mode == "window":
                o_acc = value_block(prv % 2, items[prv][1], geometries[items[prv][0]], value_groups[bi], o_acc)
            if nxt < n_items:
                m_next = score_block(nxt % 2, items[nxt][1], geometries[items[nxt][0]], rhs, blk, m_next)
            if 0 <= i < n_items:
                exp_block(i % 2, blk, m_cur)
            if 0 <= prv and bi in value_groups and mode != "window":
                o_acc = value_block(prv % 2, items[prv][1], geometries[items[prv][0]], value_groups[bi], o_acc)
        if 0 <= prv:
            store_output(*items[prv], o_acc, sink_prev)
        sink_prev = sink_cur
        if nxt < n_items:
            m_cur, sink_cur = with_sink(items[nxt][1], m_next)


def _attention(qt, kv, kv_ctx, sink, *, mode, tq):
    b, wq, sq = qt.shape
    n_pairs = wq // (2 * GQA_GROUP * HEAD_DIM)
    kc, vct = kv_ctx
    c_len = kc.shape[1]
    use_sink = sink is not None
    in_specs = [pl.BlockSpec((1, wq, tq), lambda i, t: (i, 0, t))]
    args = [qt]
    seq = sq
    if mode != "ctx":
        k, vt = kv
        seq = k.shape[1]
        in_specs += [pl.BlockSpec((1, seq, k.shape[2]), lambda i, t: (i, 0, 0)),
                     pl.BlockSpec((1, vt.shape[1], seq), lambda i, t: (i, 0, 0))]
        args += [k, vt]
    in_specs += [pl.BlockSpec((1, c_len, kc.shape[2]), lambda i, t: (i, 0, 0)),
                 pl.BlockSpec((1, vct.shape[1], c_len), lambda i, t: (i, 0, 0))]
    args += [kc, vct]
    if use_sink:
        in_specs.append(pl.BlockSpec(memory_space=pltpu.SMEM))
        args.append(sink)
    n_keys = c_len + {"dense": seq, "window": QUERY_SUB + 2 * WINDOW, "ctx": 0}[mode]
    cpi = ITEM_CHUNKS[mode]
    return pl.pallas_call(
        functools.partial(_attn_kernel, mode=mode, n_pairs=n_pairs, n_sub=tq // QUERY_SUB, tq=tq, seq=seq,
                          use_sink=use_sink, cpi=cpi),
        grid=(b, sq // tq),
        in_specs=in_specs,
        out_specs=pl.BlockSpec((1, wq, tq), lambda i, t: (i, 0, t)),
        out_shape=jax.ShapeDtypeStruct((b, wq, sq), BF16),
        scratch_shapes=[pltpu.VMEM((2, n_keys, 2 * cpi * QUERY_SUB), F32),
                        pltpu.VMEM((2, n_keys, 2 * cpi * QUERY_SUB), BF16)],
        compiler_params=_params(),
        name="attn_" + mode,
    )(*args)


def _post_kernel(*refs, pool, tm, seq):
    x_ref, at_ref, mod_ref, gains_ref, wo_ref, wfi_ref, wfo_ref = refs[:7]
    pos = 7
    if pool:
        u_ref, uprev_ref, unext_ref, wpool_ref, pscale_ref = refs[pos:pos + 5]
        pos += 5
    o_ref = refs[pos]
    hid_ref = refs[pos + 1]
    if pool:
        ext_ref = refs[pos + 2]

    n_a = at_ref.shape[1]
    if pool:
        t = pl.program_id(1)
        last = pl.num_programs(1) - 1
        halo = jnp.zeros((POOL_HALO, ext_ref.shape[1]), F32)
        ext_ref[0:POOL_HALO, :] = jnp.where(t > 0, uprev_ref[0], halo)
        ext_ref[POOL_HALO:POOL_HALO + tm, :] = u_ref[0]
        ext_ref[POOL_HALO + tm:, :] = jnp.where(t < last, unext_ref[0], halo)

    def pool_mix(r0, rows):
        n_ext = rows + 2 * POOL_HALO
        tok = t * tm + r0 + lax.broadcasted_iota(jnp.int32, (rows, 1), 0)

        def ahead(v, k):
            return pltpu.roll(v, n_ext - k, 0)

        def behind(v, k):
            return pltpu.roll(v, k, 0)

        mixed = []
        for g, w in enumerate(POOL_WINDOWS):
            sl = slice(g * LANES, (g + 1) * LANES)
            e = ext_ref[r0:r0 + n_ext, sl]
            run, length = e, 1
            while length < w // 2:
                run, length = run + ahead(run, length), 2 * length
            total = run + behind(run, w // 2)
            total = total[POOL_HALO:POOL_HALO + rows]
            cnt = jnp.minimum(tok + (w - w // 2), seq) - jnp.maximum(tok - w // 2, 0)
            diff = total / cnt.astype(F32) - e[POOL_HALO:POOL_HALO + rows]
            mixed.append((_dot(diff.astype(BF16), wpool_ref[g]) * pscale_ref[:, sl]).astype(BF16))
        return jnp.concatenate(mixed, axis=-1)

    def attn_out(r0, rows):
        return _dot_tn(at_ref[0, :, r0:r0 + rows], wo_ref[0:n_a, :])

    def pool_out(r0, rows, y):
        return y + _dot(pool_mix(r0, rows), wo_ref[n_a:, :]) if pool else y

    def residual_and_norm(r0, rows, y):
        x1 = x_ref[0, r0:r0 + rows, :] + mod_ref[0, 2:3, :] * (_rms_normalise(y) * gains_ref[1:2, :])
        h = _rms_normalise(x1) * gains_ref[2:3, :]
        return x1, (h * (1.0 + mod_ref[0, 4:5, :]) + mod_ref[0, 3:4, :]).astype(BF16)

    def ffn_hidden(r0, rows, h):
        for i in range(FFN_HIDDEN // FFN_CHUNK):
            gate = _dot(h, wfi_ref[:, i * FFN_CHUNK:(i + 1) * FFN_CHUNK])
            up = _dot(h, wfi_ref[:, FFN_HIDDEN + i * FFN_CHUNK:FFN_HIDDEN + (i + 1) * FFN_CHUNK])
            hid_ref[r0:r0 + rows, i * FFN_CHUNK:(i + 1) * FFN_CHUNK] = (_silu(gate) * up).astype(BF16)

    def ffn_out(r0, rows, x1):
        z = _dot(hid_ref[r0:r0 + rows, :], wfo_ref[...])
        o_ref[0, r0:r0 + rows, :] = x1 + mod_ref[0, 5:6, :] * (_rms_normalise(z) * gains_ref[3:4, :])

    rows = min(tm // 2, POST_SUB_TILE)
    starts = list(range(0, tm, rows))
    n = len(starts)
    y = {p: pool_out(starts[p], rows, attn_out(starts[p], rows)) for p in range(min(2, n))}
    normed = {0: residual_and_norm(starts[0], rows, y.pop(0))}
    for p in range(n):
        x1, h = normed.pop(p)
        ffn_hidden(starts[p], rows, h)
        if p + 2 < n:
            y[p + 2] = pool_out(starts[p + 2], rows, attn_out(starts[p + 2], rows))
        if p + 1 < n:
            normed[p + 1] = residual_and_norm(starts[p + 1], rows, y.pop(p + 1))
        ffn_out(starts[p], rows, x1)


def _post(x, a_t, mod, gains, wo, wfi, wfo, pool_args, *, tm):
    b, s, d = x.shape
    per_batch_mod = mod.shape[0] > 1
    pool = pool_args is not None
    in_specs = [
        pl.BlockSpec((1, tm, d), lambda i, t: (i, t, 0)),
        pl.BlockSpec((1, a_t.shape[1], tm), lambda i, t: (i, 0, t)),
        pl.BlockSpec((1, SUBLANES, d), (lambda i, t: (i, 0, 0)) if per_batch_mod else (lambda i, t: (0, 0, 0))),
        _const_spec(gains.shape), _const_spec(wo.shape), _const_spec(wfi.shape), _const_spec(wfo.shape),
    ]
    args = [x, a_t, mod, gains, wo, wfi, wfo]
    scratch = [pltpu.VMEM((tm, FFN_HIDDEN), BF16)]
    if pool:
        u, wpool, pscale = pool_args
        nu = u.shape[2]
        per_tile = tm // POOL_HALO
        n_halo_blocks = s // POOL_HALO
        in_specs += [
            pl.BlockSpec((1, tm, nu), lambda i, t: (i, t, 0)),
            pl.BlockSpec((1, POOL_HALO, nu), lambda i, t: (i, jnp.maximum(t * per_tile - 1, 0), 0)),
            pl.BlockSpec((1, POOL_HALO, nu),
                         lambda i, t: (i, jnp.minimum((t + 1) * per_tile, n_halo_blocks - 1), 0)),
            _const_spec(wpool.shape), _const_spec(pscale.shape),
        ]
        args += [u, u, u, wpool, pscale]
        scratch.append(pltpu.VMEM((tm + 2 * POOL_HALO, nu), F32))
    return pl.pallas_call(
        functools.partial(_post_kernel, pool=pool, tm=tm, seq=s),
        grid=(b, s // tm),
        in_specs=in_specs,
        out_specs=pl.BlockSpec((1, tm, d), lambda i, t: (i, t, 0)),
        out_shape=jax.ShapeDtypeStruct((b, s, d), F32),
        scratch_shapes=scratch,
        compiler_params=_params(),
        name="post_even" if pool else "post_odd",
    )(*args)


def _pad_rows(a, rows):
    return jnp.pad(a, [(0, 0)] * (a.ndim - 2) + [(0, rows - a.shape[-2]), (0, 0)])


def kernel(x, c, ctx, c_ctx, w_mod, b_mod, g_pre_mix, g_post_mix, g_pre_ffn, g_post_ffn, we_in, we_out,
           we_q_gain, we_k_gain, we_pool, we_pool_scale, wo_in, wo_out, wo_sink, w_ffn_in, w_ffn_out):
    batch, seq, d = x.shape
    c_len = ctx.shape[1]
    depth = w_mod.shape[0]
    tm = TOKEN_TILE
    tm_ctx = c_len
    assert d == D_MODEL and w_ffn_out.shape[1] == FFN_HIDDEN and w_mod.shape[2] % MOD_COLS_TILE == 0
    assert seq % PRE_TILE == 0 and seq % TOKEN_TILE == 0 and seq % QUERY_TILE == 0 and seq % GRID_W == 0
    assert seq >= QUERY_SUB + 2 * WINDOW and c_len % PRE_SUB_TILE == 0 and c_len % QUERY_SUB == 0
    assert max(POOL_WINDOWS) // 2 <= POOL_HALO

    cc = jnp.concatenate([c, c_ctx[None, :]], axis=0)
    cc = _pad_rows(cc, -(-(batch + 1) // SUBLANES) * SUBLANES)
    mods = _modulation(cc, w_mod, b_mod)

    cos, sin = _rope_tables(seq)
    _, lane_dim = _lane_slot_dim()
    head_mean = jnp.asarray(_head_mean_matrix(), BF16)

    for l in range(depth):
        even = l % 2 == 0
        i = l // 2
        with_ctx = l < depth - 1
        mod_x = _pad_rows(mods[l, :batch].reshape(batch, 6, d), SUBLANES)
        mod_c = _pad_rows(mods[l, batch:batch + 1].reshape(1, 6, d), SUBLANES)
        gains = _pad_rows(jnp.stack([g_pre_mix[l], g_post_mix[l], g_pre_ffn[l], g_post_ffn[l]]), SUBLANES)
        wfi = w_ffn_in[l].astype(BF16)
        wfo = w_ffn_out[l].astype(BF16)

        if even:
            n_heads, n_kv = 8, 2
            w_full, w_out = we_in[i], we_out[i].astype(BF16)
            norm_args = (head_mean, we_q_gain[i][lane_dim][None, :], we_k_gain[i][lane_dim][None, :])
            sink = None
            mode = "dense"
        else:
            n_heads, n_kv = 16, 4
            w_full, w_out = wo_in[i], wo_out[i].astype(BF16)
            norm_args = None
            sink = wo_sink[i]
            mode = "window"
        q_w, kv_w = n_heads * HEAD_DIM, n_kv * HEAD_DIM
        cols = np.concatenate([_q_cols(n_heads), _k_cols(n_kv, q_w), np.arange(q_w + kv_w, w_full.shape[1])])
        w_in = w_full[:, cols].astype(BF16)
        widths = (q_w, kv_w, kv_w, w_full.shape[1] - q_w - 2 * kv_w)

        lat = _pre(x, mod_x, gains, w_in, norm_args, (cos, sin), widths=widths, tm=PRE_TILE)
        con = _pre(ctx, mod_c, gains, w_in, norm_args, None, widths=widths, tm=tm_ctx)
        a_t = _attention(lat[0], (lat[1], lat[2]), (con[1], con[2]), sink, mode=mode, tq=QUERY_TILE)
        pool_w = (we_pool[i].astype(BF16), we_pool_scale[i][None, :]) if even else None
        x = _post(x, a_t, mod_x, gains, w_out, wfi, wfo, (lat[3],) + pool_w if even else None, tm=tm)
        if with_ctx:
            ac_t = _attention(con[0], None, (con[1], con[2]), sink, mode="ctx", tq=c_len)
            ctx = _post(ctx, ac_t, mod_c, gains, w_out, wfi, wfo,
                        (con[3],) + pool_w if even else None, tm=tm_ctx)
    return x
```

```python
import functools
import math

import numpy as np
import jax
import jax.numpy as jnp
from jax import lax
from jax.experimental import pallas as pl
from jax.experimental.pallas import tpu as pltpu

D_MODEL = 1024
HEAD_DIM = 64
GQA_GROUP = 4
GRID_W = 64
ROPE_THETA = 10000.0
EPS = 1e-6
WINDOW = 128
POOL_WINDOWS = (2, 4, 8, 16)
POOL_HALO = 8
FFN_HIDDEN = 2816
FFN_CHUNK = 256
LANES = 128
SUBLANES = 8
ROPE_HALF = HEAD_DIM // 2
LOG2E = math.log2(math.e)
Q_SCALE = LOG2E / math.sqrt(HEAD_DIM)
VMEM_LIMIT = 56 * 1024 * 1024
MOD_COLS_TILE = 1536
TOKEN_TILE = 512
POST_SUB_TILE = 256
PRE_TILE = 1024
PRE_SUB_TILE = 256
QUERY_TILE = 1024
QUERY_SUB = LANES
ITEM_CHUNKS = {"dense": 2, "window": 4, "ctx": 2}
KEY_BLOCK = 1024
VALUE_GROUP = 1024
VT_ONES_ROWS = 16
VT_CHUNK = LANES + VT_ONES_ROWS

F32 = jnp.float32
BF16 = jnp.bfloat16


def _lane_slot_dim():
    quarter = HEAD_DIM // 4
    lane = np.arange(LANES)
    part = lane // ROPE_HALF
    i = lane % ROPE_HALF
    dim = np.where(i < quarter, i, ROPE_HALF + (i - quarter)) + np.where(part >= 2, quarter, 0)
    return part % 2, dim


def _chunk_heads(c):
    head_a = (c // GQA_GROUP) * 2 * GQA_GROUP + c % GQA_GROUP
    return head_a, head_a + GQA_GROUP


def _q_cols(n_heads):
    slot, dim = _lane_slot_dim()
    cols = []
    for c in range(n_heads // 2):
        head_a, head_b = _chunk_heads(c)
        cols.append(np.where(slot == 0, head_a, head_b) * HEAD_DIM + dim)
    return np.concatenate(cols)


def _k_cols(n_kv, base):
    slot, dim = _lane_slot_dim()
    return np.concatenate([base + (2 * m + slot) * HEAD_DIM + dim for m in range(n_kv // 2)])


def _head_mean_matrix():
    slot, _ = _lane_slot_dim()
    mean = (slot[:, None] == slot[None, :]).astype(np.float32) / HEAD_DIM
    return np.concatenate([mean, mean], axis=0)


def _rope_tables(seq):
    quarter = HEAD_DIM // 4
    freqs = ROPE_THETA ** (-jnp.arange(quarter, dtype=F32) / quarter)
    t = jnp.arange(seq, dtype=jnp.int32)
    rows = (t // GRID_W).astype(F32)[:, None] * freqs[None, :]
    cols = (t % GRID_W).astype(F32)[:, None] * freqs[None, :]
    ang = jnp.concatenate([rows, cols], axis=-1)
    cos = jnp.tile(jnp.cos(ang), (1, 4))
    sin = jnp.sin(ang)
    sin = jnp.concatenate([-sin, -sin, sin, sin], axis=-1)
    return cos, sin


def _rms_normalise(x):
    return x * lax.rsqrt(jnp.mean(x * x, axis=-1, keepdims=True) + EPS)


def _silu(x):
    return x * (1.0 / (1.0 + jnp.exp(-x)))


def _dot(a, b):
    return jnp.dot(a, b, preferred_element_type=F32)


def _dot_tn(a_t, b):
    return lax.dot_general(a_t, b, (((0,), (0,)), ((), ())), preferred_element_type=F32)


def _const_spec(shape):
    return pl.BlockSpec(shape, lambda *_: (0,) * len(shape), pipeline_mode=pl.Buffered(1))


def _params():
    return pltpu.CompilerParams(dimension_semantics=("parallel", "parallel"), vmem_limit_bytes=VMEM_LIMIT)


def _mod_kernel(c_ref, w_ref, b_ref, o_ref):
    s = _silu(c_ref[...]).astype(BF16)
    o_ref[0] = _dot(s, w_ref[0].astype(BF16)) + b_ref[0]


def _modulation(cc, w_mod, b_mod):
    depth, d, n = w_mod.shape
    rows = cc.shape[0]
    tn = MOD_COLS_TILE
    return pl.pallas_call(
        _mod_kernel,
        grid=(depth, n // tn),
        in_specs=[
            pl.BlockSpec((rows, d), lambda l, j: (0, 0)),
            pl.BlockSpec((1, d, tn), lambda l, j: (l, 0, j)),
            pl.BlockSpec((1, 1, tn), lambda l, j: (l, 0, j)),
        ],
        out_specs=pl.BlockSpec((1, rows, tn), lambda l, j: (l, 0, j)),
        out_shape=jax.ShapeDtypeStruct((depth, rows, n), F32),
        compiler_params=_params(),
        name="modulation",
    )(cc, w_mod, b_mod.reshape(depth, 1, n))


def _pre_kernel(*refs, n_q, n_k, n_v, n_u, qk_norm, rope):
    x_ref, mod_ref, gains_ref, w_ref = refs[:4]
    pos = 4
    if qk_norm:
        pm_ref, qg_ref, kg_ref = refs[pos:pos + 3]
        pos += 3
    if rope:
        cos_ref, sin_ref = refs[pos:pos + 2]
        pos += 2
    qt_ref, k_ref, vt_ref = refs[pos:pos + 3]
    u_ref = refs[pos + 3] if n_u else None

    def modulated(rows):
        h = _rms_normalise(x_ref[0, rows, :]) * gains_ref[0:1, :]
        return (h * (1.0 + mod_ref[0, 1:2, :]) + mod_ref[0, 0:1, :]).astype(BF16)

    def head_chunk(c, gain_ref, rows):
        if qk_norm:
            c2 = c * c
            hi = c2.astype(BF16)
            lo = (c2 - hi.astype(F32)).astype(BF16)
            ms = _dot(jnp.concatenate([hi, lo], axis=1), pm_ref[...])
            c = c * lax.rsqrt(ms + EPS) * gain_ref[...]
        if rope:
            c = c * cos_ref[rows, :] + pltpu.roll(c, LANES // 2, 1) * sin_ref[rows, :]
        return c

    def project(rows, h):
        p = _dot(h, w_ref[...])
        for j in range(n_q // LANES):
            sl = slice(j * LANES, (j + 1) * LANES)
            c = head_chunk(p[:, sl], qg_ref if qk_norm else None, rows) * Q_SCALE
            qt_ref[0, sl, rows] = c.T.astype(BF16)
        for j in range(n_k // LANES):
            sl = slice(j * LANES, (j + 1) * LANES)
            c = head_chunk(p[:, n_q + j * LANES:n_q + (j + 1) * LANES], kg_ref if qk_norm else None, rows)
            k_ref[0, rows, sl] = c.astype(BF16)
        ones = jnp.ones((VT_ONES_ROWS, p.shape[0]), BF16)
        for j in range(n_v // LANES):
            r0 = j * VT_CHUNK
            vt_ref[0, r0:r0 + LANES, rows] = (
                p[:, n_q + n_k + j * LANES:n_q + n_k + (j + 1) * LANES].T.astype(BF16))
            vt_ref[0, r0 + LANES:r0 + VT_CHUNK, rows] = ones
        if n_u:
            u_ref[0, rows, :] = p[:, n_q + n_k + n_v:]

    tm = x_ref.shape[1]
    step = min(tm, PRE_SUB_TILE)
    parts = [slice(r, r + step) for r in range(0, tm, step)]
    h_next = modulated(parts[0])
    for i, rows in enumerate(parts):
        h_cur = h_next
        if i + 1 < len(parts):
            h_next = modulated(parts[i + 1])
        project(rows, h_cur)


def _pre(x, mod, gains, w, norm_args, rope_args, *, widths, tm):
    n_q, n_k, n_v, n_u = widths
    b, s, d = x.shape
    per_batch_mod = mod.shape[0] > 1
    qk_norm = norm_args is not None
    rope = rope_args is not None
    in_specs = [
        pl.BlockSpec((1, tm, d), lambda i, t: (i, t, 0)),
        pl.BlockSpec((1, SUBLANES, d), (lambda i, t: (i, 0, 0)) if per_batch_mod else (lambda i, t: (0, 0, 0))),
        _const_spec(gains.shape),
        _const_spec(w.shape),
    ]
    args = [x, mod, gains, w]
    if qk_norm:
        in_specs += [_const_spec(a.shape) for a in norm_args]
        args += list(norm_args)
    if rope:
        in_specs += [pl.BlockSpec((tm, LANES), lambda i, t: (t, 0))] * 2
        args += list(rope_args)
    vt_rows = n_v // LANES * VT_CHUNK
    out_shape = [jax.ShapeDtypeStruct((b, n_q, s), BF16), jax.ShapeDtypeStruct((b, s, n_k), BF16),
                 jax.ShapeDtypeStruct((b, vt_rows, s), BF16)]
    out_specs = [pl.BlockSpec((1, n_q, tm), lambda i, t: (i, 0, t)),
                 pl.BlockSpec((1, tm, n_k), lambda i, t: (i, t, 0)),
                 pl.BlockSpec((1, vt_rows, tm), lambda i, t: (i, 0, t))]
    if n_u:
        out_shape.append(jax.ShapeDtypeStruct((b, s, n_u), F32))
        out_specs.append(pl.BlockSpec((1, tm, n_u), lambda i, t: (i, t, 0)))
    return pl.pallas_call(
        functools.partial(_pre_kernel, n_q=n_q, n_k=n_k, n_v=n_v, n_u=n_u, qk_norm=qk_norm, rope=rope),
        grid=(b, s // tm),
        in_specs=in_specs,
        out_specs=out_specs,
        out_shape=out_shape,
        compiler_params=_params(),
        name="pre_even" if qk_norm else "pre_odd",
    )(*args)


def _attn_kernel(*refs, mode, n_pairs, n_sub, tq, seq, use_sink, cpi):
    qt_ref = refs[0]
    pos = 1
    if mode != "ctx":
        k_ref, vt_ref = refs[pos:pos + 2]
        pos += 2
    kc_ref, vct_ref = refs[pos:pos + 2]
    pos += 2
    if use_sink:
        sink_ref = refs[pos]
        pos += 1
    o_ref, s_ref, p_ref = refs[pos:pos + 3]

    sub = QUERY_SUB
    row = lax.broadcasted_iota(jnp.int32, (LANES, 1), 0)
    slot_a = (row // ROPE_HALF) % 2 == 0
    band = sub + 2 * WINDOW
    if mode == "window":
        rel = (lax.broadcasted_iota(jnp.int32, (band, sub), 0)
               - lax.broadcasted_iota(jnp.int32, (band, sub), 1))

    def block_geometry(sb):
        q0 = pl.program_id(1) * tq + sb * sub
        start = pl.multiple_of(jnp.clip(q0 - WINDOW, 0, seq - band), LANES)
        bias = jnp.where(jnp.abs(rel + (start - q0)) <= WINDOW, 0.0, -jnp.inf).astype(F32)
        return start, jnp.concatenate([bias] * (2 * cpi), axis=1)

    c_len = kc_ref.shape[1]
    first = {"dense": seq, "window": band, "ctx": 0}[mode]
    blocks = [(True, r, min(KEY_BLOCK, first - r), r) for r in range(0, first, KEY_BLOCK)]
    blocks += [(False, r, min(KEY_BLOCK, c_len - r), first + r) for r in range(0, c_len, KEY_BLOCK)]
    value_groups = {}
    for bi, (latent, r0, rows, b0) in enumerate(blocks):
        if r0 % VALUE_GROUP:
            _, g0, g_rows, gb0 = value_groups.pop(bi - 1)
            value_groups[bi] = (latent, g0, g_rows + rows, gb0)
        else:
            value_groups[bi] = (latent, r0, rows, b0)

    def item_chunks(item):
        return range(cpi * item, cpi * (item + 1))

    def item_pair(item):
        return cpi * item // GQA_GROUP

    def item_heads(item):
        return sum((_chunk_heads(c) for c in item_chunks(item)), ())

    def query_operand(sb, item):
        cols = []
        for c in item_chunks(item):
            qt = qt_ref[0, c * LANES:(c + 1) * LANES, sb * sub:(sb + 1) * sub]
            zero = jnp.zeros_like(qt)
            cols += [jnp.where(slot_a, qt, zero), jnp.where(slot_a, zero, qt)]
        return jnp.concatenate(cols, axis=1)

    def score_block(slot, item, geometry, rhs, blk, m):
        latent, r0, rows, b0 = blk
        psl = slice(item_pair(item) * LANES, (item_pair(item) + 1) * LANES)
        if not latent:
            s = _dot(kc_ref[0, r0:r0 + rows, psl], rhs)
        elif mode == "dense":
            s = _dot(k_ref[0, r0:r0 + rows, psl], rhs)
        else:
            start, bias = geometry
            s = _dot(k_ref[0, pl.ds(pl.multiple_of(start + r0, LANES), rows), psl], rhs) + bias[r0:r0 + rows]
        s_ref[slot, b0:b0 + rows, :] = s
        top = s.max(axis=0, keepdims=True)
        return top if m is None else jnp.maximum(m, top)

    def exp_block(slot, blk, m):
        _, _, rows, b0 = blk
        p_ref[slot, b0:b0 + rows, :] = jnp.exp2((s_ref[slot, b0:b0 + rows, :] - m).astype(BF16))

    def value_block(slot, item, geometry, blk, acc):
        latent, r0, rows, b0 = blk
        pair = item_pair(item)
        vsl = slice(pair * VT_CHUNK, (pair + 1) * VT_CHUNK)
        if not latent:
            v_t = vct_ref[0, vsl, r0:r0 + rows]
        elif mode == "dense":
            v_t = vt_ref[0, vsl, r0:r0 + rows]
        else:
            v_t = vt_ref[0, vsl, pl.ds(pl.multiple_of(geometry[0] + r0, LANES), rows)]
        pv = _dot(v_t, p_ref[slot, b0:b0 + rows, :])
        return pv if acc is None else acc + pv

    def with_sink(item, m):
        if not use_sink:
            return m, None
        sink = jnp.concatenate([jnp.full((1, sub), sink_ref[h], F32) for h in item_heads(item)], axis=1) * LOG2E
        m = jnp.maximum(m, sink)
        return m, jnp.exp2(sink - m)

    def store_output(sb, item, o4, sink_term):
        denom = o4[LANES:LANES + 1, :]
        if use_sink:
            denom = denom + sink_term
        o4 = o4[:LANES, :] * (1.0 / denom)
        for j, h in enumerate(item_heads(item)):
            rows = slice(0, HEAD_DIM) if j % 2 == 0 else slice(HEAD_DIM, 2 * HEAD_DIM)
            o_ref[0, h * HEAD_DIM:(h + 1) * HEAD_DIM, sb * sub:(sb + 1) * sub] = (
                o4[rows, j * sub:(j + 1) * sub].astype(BF16))

    items = [(sb, it) for sb in range(n_sub) for it in range(GQA_GROUP * n_pairs // cpi)]
    geometries = [block_geometry(sb) if mode == "window" else None for sb in range(n_sub)]
    n_items = len(items)
    m_cur = sink_cur = sink_prev = None
    for i in range(-1, n_items + 1):
        nxt, prv = i + 1, i - 1
        if nxt < n_items:
            rhs = query_operand(*items[nxt])
        m_next = o_acc = None
        for bi, blk in enumerate(blocks):
            if nxt < n_items:
                m_next = score_block(nxt % 2, items[nxt][1], geometries[items[nxt][0]], rhs, blk, m_next)
            if 0 <= i < n_items:
                exp_block(i % 2, blk, m_cur)
            if 0 <= prv and bi in value_groups:
                o_acc = value_block(prv % 2, items[prv][1], geometries[items[prv][0]], value_groups[bi], o_acc)
        if 0 <= prv:
            store_output(*items[prv], o_acc, sink_prev)
        sink_prev = sink_cur
        if nxt < n_items:
            m_cur, sink_cur = with_sink(items[nxt][1], m_next)


def _attention(qt, kv, kv_ctx, sink, *, mode, tq):
    b, wq, sq = qt.shape
    n_pairs = wq // (2 * GQA_GROUP * HEAD_DIM)
    kc, vct = kv_ctx
    c_len = kc.shape[1]
    use_sink = sink is not None
    in_specs = [pl.BlockSpec((1, wq, tq), lambda i, t: (i, 0, t))]
    args = [qt]
    seq = sq
    if mode != "ctx":
        k, vt = kv
        seq = k.shape[1]
        in_specs += [pl.BlockSpec((1, seq, k.shape[2]), lambda i, t: (i, 0, 0)),
                     pl.BlockSpec((1, vt.shape[1], seq), lambda i, t: (i, 0, 0))]
        args += [k, vt]
    in_specs += [pl.BlockSpec((1, c_len, kc.shape[2]), lambda i, t: (i, 0, 0)),
                 pl.BlockSpec((1, vct.shape[1], c_len), lambda i, t: (i, 0, 0))]
    args += [kc, vct]
    if use_sink:
        in_specs.append(pl.BlockSpec(memory_space=pltpu.SMEM))
        args.append(sink)
    n_keys = c_len + {"dense": seq, "window": QUERY_SUB + 2 * WINDOW, "ctx": 0}[mode]
    cpi = ITEM_CHUNKS[mode]
    return pl.pallas_call(
        functools.partial(_attn_kernel, mode=mode, n_pairs=n_pairs, n_sub=tq // QUERY_SUB, tq=tq, seq=seq,
                          use_sink=use_sink, cpi=cpi),
        grid=(b, sq // tq),
        in_specs=in_specs,
        out_specs=pl.BlockSpec((1, wq, tq), lambda i, t: (i, 0, t)),
        out_shape=jax.ShapeDtypeStruct((b, wq, sq), BF16),
        scratch_shapes=[pltpu.VMEM((2, n_keys, 2 * cpi * QUERY_SUB), F32),
                        pltpu.VMEM((2, n_keys, 2 * cpi * QUERY_SUB), BF16)],
        compiler_params=_params(),
        name="attn_" + mode,
    )(*args)


def _post_kernel(*refs, pool, tm, seq):
    x_ref, at_ref, mod_ref, gains_ref, wo_ref, wfi_ref, wfo_ref = refs[:7]
    pos = 7
    if pool:
        u_ref, uprev_ref, unext_ref, wpool_ref, pscale_ref = refs[pos:pos + 5]
        pos += 5
    o_ref = refs[pos]
    hid_ref = refs[pos + 1]
    if pool:
        ext_ref = refs[pos + 2]

    n_a = at_ref.shape[1]
    if pool:
        t = pl.program_id(1)
        last = pl.num_programs(1) - 1
        halo = jnp.zeros((POOL_HALO, ext_ref.shape[1]), F32)
        ext_ref[0:POOL_HALO, :] = jnp.where(t > 0, uprev_ref[0], halo)
        ext_ref[POOL_HALO:POOL_HALO + tm, :] = u_ref[0]
        ext_ref[POOL_HALO + tm:, :] = jnp.where(t < last, unext_ref[0], halo)

    def pool_mix(r0, rows):
        n_ext = rows + 2 * POOL_HALO
        tok = t * tm + r0 + lax.broadcasted_iota(jnp.int32, (rows, 1), 0)

        def ahead(v, k):
            return pltpu.roll(v, n_ext - k, 0)

        def behind(v, k):
            return pltpu.roll(v, k, 0)

        mixed = []
        for g, w in enumerate(POOL_WINDOWS):
            sl = slice(g * LANES, (g + 1) * LANES)
            e = ext_ref[r0:r0 + n_ext, sl]
            run, length = e, 1
            while length < w // 2:
                run, length = run + ahead(run, length), 2 * length
            total = run + behind(run, w // 2)
            total = total[POOL_HALO:POOL_HALO + rows]
            cnt = jnp.minimum(tok + (w - w // 2), seq) - jnp.maximum(tok - w // 2, 0)
            diff = total / cnt.astype(F32) - e[POOL_HALO:POOL_HALO + rows]
            mixed.append((_dot(diff.astype(BF16), wpool_ref[g]) * pscale_ref[:, sl]).astype(BF16))
        return jnp.concatenate(mixed, axis=-1)

    def attn_out(r0, rows):
        return _dot_tn(at_ref[0, :, r0:r0 + rows], wo_ref[0:n_a, :])

    def pool_out(r0, rows, y):
        return y + _dot(pool_mix(r0, rows), wo_ref[n_a:, :]) if pool else y

    def residual_and_norm(r0, rows, y):
        x1 = x_ref[0, r0:r0 + rows, :] + mod_ref[0, 2:3, :] * (_rms_normalise(y) * gains_ref[1:2, :])
        h = _rms_normalise(x1) * gains_ref[2:3, :]
        return x1, (h * (1.0 + mod_ref[0, 4:5, :]) + mod_ref[0, 3:4, :]).astype(BF16)

    def ffn_hidden(r0, rows, h):
        for i in range(FFN_HIDDEN // FFN_CHUNK):
            gate = _dot(h, wfi_ref[:, i * FFN_CHUNK:(i + 1) * FFN_CHUNK])
            up = _dot(h, wfi_ref[:, FFN_HIDDEN + i * FFN_CHUNK:FFN_HIDDEN + (i + 1) * FFN_CHUNK])
            hid_ref[r0:r0 + rows, i * FFN_CHUNK:(i + 1) * FFN_CHUNK] = (_silu(gate) * up).astype(BF16)

    def ffn_out(r0, rows, x1):
        z = _dot(hid_ref[r0:r0 + rows, :], wfo_ref[...])
        o_ref[0, r0:r0 + rows, :] = x1 + mod_ref[0, 5:6, :] * (_rms_normalise(z) * gains_ref[3:4, :])

    rows = min(tm // 2, POST_SUB_TILE)
    starts = list(range(0, tm, rows))
    n = len(starts)
    y = {p: pool_out(starts[p], rows, attn_out(starts[p], rows)) for p in range(min(2, n))}
    normed = {0: residual_and_norm(starts[0], rows, y.pop(0))}
    for p in range(n):
        x1, h = normed.pop(p)
        ffn_hidden(starts[p], rows, h)
        if p + 2 < n:
            y[p + 2] = pool_out(starts[p + 2], rows, attn_out(starts[p + 2], rows))
        if p + 1 < n:
            normed[p + 1] = residual_and_norm(starts[p + 1], rows, y.pop(p + 1))
        ffn_out(starts[p], rows, x1)


def _post(x, a_t, mod, gains, wo, wfi, wfo, pool_args, *, tm):
    b, s, d = x.shape
    per_batch_mod = mod.shape[0] > 1
    pool = pool_args is not None
    in_specs = [
        pl.BlockSpec((1, tm, d), lambda i, t: (i, t, 0)),
        pl.BlockSpec((1, a_t.shape[1], tm), lambda i, t: (i, 0, t)),
        pl.BlockSpec((1, SUBLANES, d), (lambda i, t: (i, 0, 0)) if per_batch_mod else (lambda i, t: (0, 0, 0))),
        _const_spec(gains.shape), _const_spec(wo.shape), _const_spec(wfi.shape), _const_spec(wfo.shape),
    ]
    args = [x, a_t, mod, gains, wo, wfi, wfo]
    scratch = [pltpu.VMEM((tm, FFN_HIDDEN), BF16)]
    if pool:
        u, wpool, pscale = pool_args
        nu = u.shape[2]
        per_tile = tm // POOL_HALO
        n_halo_blocks = s // POOL_HALO
        in_specs += [
            pl.BlockSpec((1, tm, nu), lambda i, t: (i, t, 0)),
            pl.BlockSpec((1, POOL_HALO, nu), lambda i, t: (i, jnp.maximum(t * per_tile - 1, 0), 0)),
            pl.BlockSpec((1, POOL_HALO, nu),
                         lambda i, t: (i, jnp.minimum((t + 1) * per_tile, n_halo_blocks - 1), 0)),
            _const_spec(wpool.shape), _const_spec(pscale.shape),
        ]
        args += [u, u, u, wpool, pscale]
        scratch.append(pltpu.VMEM((tm + 2 * POOL_HALO, nu), F32))
    return pl.pallas_call(
        functools.partial(_post_kernel, pool=pool, tm=tm, seq=s),
        grid=(b, s // tm),
        in_specs=in_specs,
        out_specs=pl.BlockSpec((1, tm, d), lambda i, t: (i, t, 0)),
        out_shape=jax.ShapeDtypeStruct((b, s, d), F32),
        scratch_shapes=scratch,
        compiler_params=_params(),
        name="post_even" if pool else "post_odd",
    )(*args)


def _pad_rows(a, rows):
    return jnp.pad(a, [(0, 0)] * (a.ndim - 2) + [(0, rows - a.shape[-2]), (0, 0)])


def kernel(x, c, ctx, c_ctx, w_mod, b_mod, g_pre_mix, g_post_mix, g_pre_ffn, g_post_ffn, we_in, we_out,
           we_q_gain, we_k_gain, we_pool, we_pool_scale, wo_in, wo_out, wo_sink, w_ffn_in, w_ffn_out):
    batch, seq, d = x.shape
    c_len = ctx.shape[1]
    depth = w_mod.shape[0]
    tm = TOKEN_TILE
    tm_ctx = c_len
    assert d == D_MODEL and w_ffn_out.shape[1] == FFN_HIDDEN and w_mod.shape[2] % MOD_COLS_TILE == 0
    assert seq % PRE_TILE == 0 and seq % TOKEN_TILE == 0 and seq % QUERY_TILE == 0 and seq % GRID_W == 0
    assert seq >= QUERY_SUB + 2 * WINDOW and c_len % PRE_SUB_TILE == 0 and c_len % QUERY_SUB == 0
    assert max(POOL_WINDOWS) // 2 <= POOL_HALO

    cc = jnp.concatenate([c, c_ctx[None, :]], axis=0)
    cc = _pad_rows(cc, -(-(batch + 1) // SUBLANES) * SUBLANES)
    mods = _modulation(cc, w_mod, b_mod)

    cos, sin = _rope_tables(seq)
    _, lane_dim = _lane_slot_dim()
    head_mean = jnp.asarray(_head_mean_matrix(), BF16)

    for l in range(depth):
        even = l % 2 == 0
        i = l // 2
        with_ctx = l < depth - 1
        mod_x = _pad_rows(mods[l, :batch].reshape(batch, 6, d), SUBLANES)
        mod_c = _pad_rows(mods[l, batch:batch + 1].reshape(1, 6, d), SUBLANES)
        gains = _pad_rows(jnp.stack([g_pre_mix[l], g_post_mix[l], g_pre_ffn[l], g_post_ffn[l]]), SUBLANES)
        wfi = w_ffn_in[l].astype(BF16)
        wfo = w_ffn_out[l].astype(BF16)

        if even:
            n_heads, n_kv = 8, 2
            w_full, w_out = we_in[i], we_out[i].astype(BF16)
            norm_args = (head_mean, we_q_gain[i][lane_dim][None, :], we_k_gain[i][lane_dim][None, :])
            sink = None
            mode = "dense"
        else:
            n_heads, n_kv = 16, 4
            w_full, w_out = wo_in[i], wo_out[i].astype(BF16)
            norm_args = None
            sink = wo_sink[i]
            mode = "window"
        q_w, kv_w = n_heads * HEAD_DIM, n_kv * HEAD_DIM
        cols = np.concatenate([_q_cols(n_heads), _k_cols(n_kv, q_w), np.arange(q_w + kv_w, w_full.shape[1])])
        w_in = w_full[:, cols].astype(BF16)
        widths = (q_w, kv_w, kv_w, w_full.shape[1] - q_w - 2 * kv_w)

        lat = _pre(x, mod_x, gains, w_in, norm_args, (cos, sin), widths=widths, tm=PRE_TILE)
        con = _pre(ctx, mod_c, gains, w_in, norm_args, None, widths=widths, tm=tm_ctx)
        a_t = _attention(lat[0], (lat[1], lat[2]), (con[1], con[2]), sink, mode=mode, tq=QUERY_TILE)
        pool_w = (we_pool[i].astype(BF16), we_pool_scale[i][None, :]) if even else None
        x = _post(x, a_t, mod_x, gains, w_out, wfi, wfo, (lat[3],) + pool_w if even else None, tm=tm)
        if with_ctx:
            ac_t = _attention(con[0], None, (con[1], con[2]), sink, mode="ctx", tq=c_len)
            ctx = _post(ctx, ac_t, mod_c, gains, w_out, wfi, wfo,
                        (con[3],) + pool_w if even else None, tm=tm_ctx)
    return x
```

```python
import functools
import math

import numpy as np
import jax
import jax.numpy as jnp
from jax import lax
from jax.experimental import pallas as pl
from jax.experimental.pallas import tpu as pltpu

D_MODEL = 1024
HEAD_DIM = 64
GQA_GROUP = 4
GRID_W = 64
ROPE_THETA = 10000.0
EPS = 1e-6
WINDOW = 128
POOL_WINDOWS = (2, 4, 8, 16)
POOL_HALO = 8
FFN_HIDDEN = 2816
FFN_CHUNK = 256
LANES = 128
SUBLANES = 8
ROPE_HALF = HEAD_DIM // 2
LOG2E = math.log2(math.e)
Q_SCALE = LOG2E / math.sqrt(HEAD_DIM)
VMEM_LIMIT = 56 * 1024 * 1024
MOD_COLS_TILE = 1536
TOKEN_TILE = 512
POST_SUB_TILE = 256
PRE_TILE = 1024
PRE_SUB_TILE = 512
QUERY_TILE = 1024
QUERY_SUB = LANES
ITEM_CHUNKS = {"dense": 2, "window": 4, "ctx": 2}
KEY_BLOCK = 1024
VALUE_GROUP = 1024
VT_ONES_ROWS = 16
VT_CHUNK = LANES + VT_ONES_ROWS

F32 = jnp.float32
BF16 = jnp.bfloat16


def _lane_slot_dim():
    quarter = HEAD_DIM // 4
    lane = np.arange(LANES)
    part = lane // ROPE_HALF
    i = lane % ROPE_HALF
    dim = np.where(i < quarter, i, ROPE_HALF + (i - quarter)) + np.where(part >= 2, quarter, 0)
    return part % 2, dim


def _chunk_heads(c):
    head_a = (c // GQA_GROUP) * 2 * GQA_GROUP + c % GQA_GROUP
    return head_a, head_a + GQA_GROUP


def _q_cols(n_heads):
    slot, dim = _lane_slot_dim()
    cols = []
    for c in range(n_heads // 2):
        head_a, head_b = _chunk_heads(c)
        cols.append(np.where(slot == 0, head_a, head_b) * HEAD_DIM + dim)
    return np.concatenate(cols)


def _k_cols(n_kv, base):
    slot, dim = _lane_slot_dim()
    return np.concatenate([base + (2 * m + slot) * HEAD_DIM + dim for m in range(n_kv // 2)])


def _head_mean_matrix():
    slot, _ = _lane_slot_dim()
    mean = (slot[:, None] == slot[None, :]).astype(np.float32) / HEAD_DIM
    return np.concatenate([mean, mean], axis=0)


def _rope_tables(seq):
    quarter = HEAD_DIM // 4
    freqs = ROPE_THETA ** (-jnp.arange(quarter, dtype=F32) / quarter)
    t = jnp.arange(seq, dtype=jnp.int32)
    rows = (t // GRID_W).astype(F32)[:, None] * freqs[None, :]
    cols = (t % GRID_W).astype(F32)[:, None] * freqs[None, :]
    ang = jnp.concatenate([rows, cols], axis=-1)
    cos = jnp.tile(jnp.cos(ang), (1, 4))
    sin = jnp.sin(ang)
    sin = jnp.concatenate([-sin, -sin, sin, sin], axis=-1)
    return cos, sin


def _rms_normalise(x):
    return x * lax.rsqrt(jnp.mean(x * x, axis=-1, keepdims=True) + EPS)


def _silu(x):
    return x * (1.0 / (1.0 + jnp.exp(-x)))


def _dot(a, b):
    return jnp.dot(a, b, preferred_element_type=F32)


def _dot_tn(a_t, b):
    return lax.dot_general(a_t, b, (((0,), (0,)), ((), ())), preferred_element_type=F32)


def _const_spec(shape):
    return pl.BlockSpec(shape, lambda *_: (0,) * len(shape), pipeline_mode=pl.Buffered(1))


def _params():
    return pltpu.CompilerParams(dimension_semantics=("parallel", "parallel"), vmem_limit_bytes=VMEM_LIMIT)


def _mod_kernel(c_ref, w_ref, b_ref, o_ref):
    s = _silu(c_ref[...]).astype(BF16)
    o_ref[0] = _dot(s, w_ref[0].astype(BF16)) + b_ref[0]


def _modulation(cc, w_mod, b_mod):
    depth, d, n = w_mod.shape
    rows = cc.shape[0]
    tn = MOD_COLS_TILE
    return pl.pallas_call(
        _mod_kernel,
        grid=(depth, n // tn),
        in_specs=[
            pl.BlockSpec((rows, d), lambda l, j: (0, 0)),
            pl.BlockSpec((1, d, tn), lambda l, j: (l, 0, j)),
            pl.BlockSpec((1, 1, tn), lambda l, j: (l, 0, j)),
        ],
        out_specs=pl.BlockSpec((1, rows, tn), lambda l, j: (l, 0, j)),
        out_shape=jax.ShapeDtypeStruct((depth, rows, n), F32),
        compiler_params=_params(),
        name="modulation",
    )(cc, w_mod, b_mod.reshape(depth, 1, n))


def _pre_kernel(*refs, n_q, n_k, n_v, n_u, qk_norm, rope):
    x_ref, mod_ref, gains_ref, w_ref = refs[:4]
    pos = 4
    if qk_norm:
        pm_ref, qg_ref, kg_ref = refs[pos:pos + 3]
        pos += 3
    if rope:
        cos_ref, sin_ref = refs[pos:pos + 2]
        pos += 2
    qt_ref, k_ref, vt_ref = refs[pos:pos + 3]
    u_ref = refs[pos + 3] if n_u else None

    def modulated(rows):
        h = _rms_normalise(x_ref[0, rows, :]) * gains_ref[0:1, :]
        return (h * (1.0 + mod_ref[0, 1:2, :]) + mod_ref[0, 0:1, :]).astype(BF16)

    def head_chunk(c, gain_ref, rows):
        if qk_norm:
            c2 = c * c
            hi = c2.astype(BF16)
            lo = (c2 - hi.astype(F32)).astype(BF16)
            ms = _dot(jnp.concatenate([hi, lo], axis=1), pm_ref[...])
            c = c * lax.rsqrt(ms + EPS) * gain_ref[...]
        if rope:
            c = c * cos_ref[rows, :] + pltpu.roll(c, LANES // 2, 1) * sin_ref[rows, :]
        return c

    def project(rows, h):
        p = _dot(h, w_ref[...])
        for j in range(n_q // LANES):
            sl = slice(j * LANES, (j + 1) * LANES)
            c = head_chunk(p[:, sl], qg_ref if qk_norm else None, rows) * Q_SCALE
            qt_ref[0, sl, rows] = c.T.astype(BF16)
        for j in range(n_k // LANES):
            sl = slice(j * LANES, (j + 1) * LANES)
            c = head_chunk(p[:, n_q + j * LANES:n_q + (j + 1) * LANES], kg_ref if qk_norm else None, rows)
            k_ref[0, rows, sl] = c.astype(BF16)
        ones = jnp.ones((VT_ONES_ROWS, p.shape[0]), BF16)
        for j in range(n_v // LANES):
            r0 = j * VT_CHUNK
            vt_ref[0, r0:r0 + LANES, rows] = (
                p[:, n_q + n_k + j * LANES:n_q + n_k + (j + 1) * LANES].T.astype(BF16))
            vt_ref[0, r0 + LANES:r0 + VT_CHUNK, rows] = ones
        if n_u:
            u_ref[0, rows, :] = p[:, n_q + n_k + n_v:]

    tm = x_ref.shape[1]
    step = min(tm, PRE_SUB_TILE)
    parts = [slice(r, r + step) for r in range(0, tm, step)]
    h_next = modulated(parts[0])
    for i, rows in enumerate(parts):
        h_cur = h_next
        if i + 1 < len(parts):
            h_next = modulated(parts[i + 1])
        project(rows, h_cur)


def _pre(x, mod, gains, w, norm_args, rope_args, *, widths, tm):
    n_q, n_k, n_v, n_u = widths
    b, s, d = x.shape
    per_batch_mod = mod.shape[0] > 1
    qk_norm = norm_args is not None
    rope = rope_args is not None
    in_specs = [
        pl.BlockSpec((1, tm, d), lambda i, t: (i, t, 0)),
        pl.BlockSpec((1, SUBLANES, d), (lambda i, t: (i, 0, 0)) if per_batch_mod else (lambda i, t: (0, 0, 0))),
        _const_spec(gains.shape),
        _const_spec(w.shape),
    ]
    args = [x, mod, gains, w]
    if qk_norm:
        in_specs += [_const_spec(a.shape) for a in norm_args]
        args += list(norm_args)
    if rope:
        in_specs += [pl.BlockSpec((tm, LANES), lambda i, t: (t, 0))] * 2
        args += list(rope_args)
    vt_rows = n_v // LANES * VT_CHUNK
    out_shape = [jax.ShapeDtypeStruct((b, n_q, s), BF16), jax.ShapeDtypeStruct((b, s, n_k), BF16),
                 jax.ShapeDtypeStruct((b, vt_rows, s), BF16)]
    out_specs = [pl.BlockSpec((1, n_q, tm), lambda i, t: (i, 0, t)),
                 pl.BlockSpec((1, tm, n_k), lambda i, t: (i, t, 0)),
                 pl.BlockSpec((1, vt_rows, tm), lambda i, t: (i, 0, t))]
    if n_u:
        out_shape.append(jax.ShapeDtypeStruct((b, s, n_u), F32))
        out_specs.append(pl.BlockSpec((1, tm, n_u), lambda i, t: (i, t, 0)))
    return pl.pallas_call(
        functools.partial(_pre_kernel, n_q=n_q, n_k=n_k, n_v=n_v, n_u=n_u, qk_norm=qk_norm, rope=rope),
        grid=(b, s // tm),
        in_specs=in_specs,
        out_specs=out_specs,
        out_shape=out_shape,
        compiler_params=_params(),
        name="pre_even" if qk_norm else "pre_odd",
    )(*args)


def _attn_kernel(*refs, mode, n_pairs, n_sub, tq, seq, use_sink, cpi):
    qt_ref = refs[0]
    pos = 1
    if mode != "ctx":
        k_ref, vt_ref = refs[pos:pos + 2]
        pos += 2
    kc_ref, vct_ref = refs[pos:pos + 2]
    pos += 2
    if use_sink:
        sink_ref = refs[pos]
        pos += 1
    o_ref, s_ref, p_ref = refs[pos:pos + 3]

    sub = QUERY_SUB
    row = lax.broadcasted_iota(jnp.int32, (LANES, 1), 0)
    slot_a = (row // ROPE_HALF) % 2 == 0
    band = sub + 2 * WINDOW
    if mode == "window":
        rel = (lax.broadcasted_iota(jnp.int32, (band, sub), 0)
               - lax.broadcasted_iota(jnp.int32, (band, sub), 1))

    def block_geometry(sb):
        q0 = pl.program_id(1) * tq + sb * sub
        start = pl.multiple_of(jnp.clip(q0 - WINDOW, 0, seq - band), LANES)
        bias = jnp.where(jnp.abs(rel + (start - q0)) <= WINDOW, 0.0, -jnp.inf).astype(F32)
        return start, jnp.concatenate([bias] * (2 * cpi), axis=1)

    c_len = kc_ref.shape[1]
    first = {"dense": seq, "window": band, "ctx": 0}[mode]
    blocks = [(True, r, min(KEY_BLOCK, first - r), r) for r in range(0, first, KEY_BLOCK)]
    blocks += [(False, r, min(KEY_BLOCK, c_len - r), first + r) for r in range(0, c_len, KEY_BLOCK)]
    value_groups = {}
    for bi, (latent, r0, rows, b0) in enumerate(blocks):
        if r0 % VALUE_GROUP:
            _, g0, g_rows, gb0 = value_groups.pop(bi - 1)
            value_groups[bi] = (latent, g0, g_rows + rows, gb0)
        else:
            value_groups[bi] = (latent, r0, rows, b0)

    def item_chunks(item):
        return range(cpi * item, cpi * (item + 1))

    def item_pair(item):
        return cpi * item // GQA_GROUP

    def item_heads(item):
        return sum((_chunk_heads(c) for c in item_chunks(item)), ())

    def query_operand(sb, item):
        cols = []
        for c in item_chunks(item):
            qt = qt_ref[0, c * LANES:(c + 1) * LANES, sb * sub:(sb + 1) * sub]
            zero = jnp.zeros_like(qt)
            cols += [jnp.where(slot_a, qt, zero), jnp.where(slot_a, zero, qt)]
        return jnp.concatenate(cols, axis=1)

    def score_block(slot, item, geometry, rhs, blk, m):
        latent, r0, rows, b0 = blk
        psl = slice(item_pair(item) * LANES, (item_pair(item) + 1) * LANES)
        if not latent:
            s = _dot(kc_ref[0, r0:r0 + rows, psl], rhs)
        elif mode == "dense":
            s = _dot(k_ref[0, r0:r0 + rows, psl], rhs)
        else:
            start, bias = geometry
            s = _dot(k_ref[0, pl.ds(pl.multiple_of(start + r0, LANES), rows), psl], rhs) + bias[r0:r0 + rows]
        s_ref[slot, b0:b0 + rows, :] = s
        top = s.max(axis=0, keepdims=True)
        return top if m is None else jnp.maximum(m, top)

    def exp_block(slot, blk, m):
        _, _, rows, b0 = blk
        p_ref[slot, b0:b0 + rows, :] = jnp.exp2(s_ref[slot, b0:b0 + rows, :] - m).astype(BF16)

    def value_block(slot, item, geometry, blk, acc):
        latent, r0, rows, b0 = blk
        pair = item_pair(item)
        vsl = slice(pair * VT_CHUNK, (pair + 1) * VT_CHUNK)
        if not latent:
            v_t = vct_ref[0, vsl, r0:r0 + rows]
        elif mode == "dense":
            v_t = vt_ref[0, vsl, r0:r0 + rows]
        else:
            v_t = vt_ref[0, vsl, pl.ds(pl.multiple_of(geometry[0] + r0, LANES), rows)]
        pv = _dot(v_t, p_ref[slot, b0:b0 + rows, :])
        return pv if acc is None else acc + pv

    def with_sink(item, m):
        if not use_sink:
            return m, None
        sink = jnp.concatenate([jnp.full((1, sub), sink_ref[h], F32) for h in item_heads(item)], axis=1) * LOG2E
        m = jnp.maximum(m, sink)
        return m, jnp.exp2(sink - m)

    def store_output(sb, item, o4, sink_term):
        denom = o4[LANES:LANES + 1, :]
        if use_sink:
            denom = denom + sink_term
        o4 = o4[:LANES, :] * (1.0 / denom)
        for j, h in enumerate(item_heads(item)):
            rows = slice(0, HEAD_DIM) if j % 2 == 0 else slice(HEAD_DIM, 2 * HEAD_DIM)
            o_ref[0, h * HEAD_DIM:(h + 1) * HEAD_DIM, sb * sub:(sb + 1) * sub] = (
                o4[rows, j * sub:(j + 1) * sub].astype(BF16))

    items = [(sb, it) for sb in range(n_sub) for it in range(GQA_GROUP * n_pairs // cpi)]
    geometries = [block_geometry(sb) if mode == "window" else None for sb in range(n_sub)]
    n_items = len(items)
    m_cur = sink_cur = sink_prev = None
    for i in range(-1, n_items + 1):
        nxt, prv = i + 1, i - 1
        if nxt < n_items:
            rhs = query_operand(*items[nxt])
        m_next = o_acc = None
        for bi, blk in enumerate(blocks):
            if nxt < n_items:
                m_next = score_block(nxt % 2, items[nxt][1], geometries[items[nxt][0]], rhs, blk, m_next)
            if 0 <= i < n_items:
                exp_block(i % 2, blk, m_cur)
            if 0 <= prv and bi in value_groups:
                o_acc = value_block(prv % 2, items[prv][1], geometries[items[prv][0]], value_groups[bi], o_acc)
        if 0 <= prv:
            store_output(*items[prv], o_acc, sink_prev)
        sink_prev = sink_cur
        if nxt < n_items:
            m_cur, sink_cur = with_sink(items[nxt][1], m_next)


def _attention(qt, kv, kv_ctx, sink, *, mode, tq):
    b, wq, sq = qt.shape
    n_pairs = wq // (2 * GQA_GROUP * HEAD_DIM)
    kc, vct = kv_ctx
    c_len = kc.shape[1]
    use_sink = sink is not None
    in_specs = [pl.BlockSpec((1, wq, tq), lambda i, t: (i, 0, t))]
    args = [qt]
    seq = sq
    if mode != "ctx":
        k, vt = kv
        seq = k.shape[1]
        in_specs += [pl.BlockSpec((1, seq, k.shape[2]), lambda i, t: (i, 0, 0)),
                     pl.BlockSpec((1, vt.shape[1], seq), lambda i, t: (i, 0, 0))]
        args += [k, vt]
    in_specs += [pl.BlockSpec((1, c_len, kc.shape[2]), lambda i, t: (i, 0, 0)),
                 pl.BlockSpec((1, vct.shape[1], c_len), lambda i, t: (i, 0, 0))]
    args += [kc, vct]
    if use_sink:
        in_specs.append(pl.BlockSpec(memory_space=pltpu.SMEM))
        args.append(sink)
    n_keys = c_len + {"dense": seq, "window": QUERY_SUB + 2 * WINDOW, "ctx": 0}[mode]
    cpi = ITEM_CHUNKS[mode]
    return pl.pallas_call(
        functools.partial(_attn_kernel, mode=mode, n_pairs=n_pairs, n_sub=tq // QUERY_SUB, tq=tq, seq=seq,
                          use_sink=use_sink, cpi=cpi),
        grid=(b, sq // tq),
        in_specs=in_specs,
        out_specs=pl.BlockSpec((1, wq, tq), lambda i, t: (i, 0, t)),
        out_shape=jax.ShapeDtypeStruct((b, wq, sq), BF16),
        scratch_shapes=[pltpu.VMEM((2, n_keys, 2 * cpi * QUERY_SUB), F32),
                        pltpu.VMEM((2, n_keys, 2 * cpi * QUERY_SUB), BF16)],
        compiler_params=_params(),
        name="attn_" + mode,
    )(*args)


def _post_kernel(*refs, pool, tm, seq):
    x_ref, at_ref, mod_ref, gains_ref, wo_ref, wfi_ref, wfo_ref = refs[:7]
    pos = 7
    if pool:
        u_ref, uprev_ref, unext_ref, wpool_ref, pscale_ref = refs[pos:pos + 5]
        pos += 5
    o_ref = refs[pos]
    hid_ref = refs[pos + 1]
    if pool:
        ext_ref = refs[pos + 2]

    n_a = at_ref.shape[1]
    if pool:
        t = pl.program_id(1)
        last = pl.num_programs(1) - 1
        halo = jnp.zeros((POOL_HALO, ext_ref.shape[1]), F32)
        ext_ref[0:POOL_HALO, :] = jnp.where(t > 0, uprev_ref[0], halo)
        ext_ref[POOL_HALO:POOL_HALO + tm, :] = u_ref[0]
        ext_ref[POOL_HALO + tm:, :] = jnp.where(t < last, unext_ref[0], halo)

    def pool_mix(r0, rows):
        n_ext = rows + 2 * POOL_HALO
        tok = t * tm + r0 + lax.broadcasted_iota(jnp.int32, (rows, 1), 0)

        def ahead(v, k):
            return pltpu.roll(v, n_ext - k, 0)

        def behind(v, k):
            return pltpu.roll(v, k, 0)

        mixed = []
        for g, w in enumerate(POOL_WINDOWS):
            sl = slice(g * LANES, (g + 1) * LANES)
            e = ext_ref[r0:r0 + n_ext, sl]
            run, length = e, 1
            while length < w // 2:
                run, length = run + ahead(run, length), 2 * length
            total = run + behind(run, w // 2)
            total = total[POOL_HALO:POOL_HALO + rows]
            cnt = jnp.minimum(tok + (w - w // 2), seq) - jnp.maximum(tok - w // 2, 0)
            diff = total / cnt.astype(F32) - e[POOL_HALO:POOL_HALO + rows]
            mixed.append((_dot(diff.astype(BF16), wpool_ref[g]) * pscale_ref[:, sl]).astype(BF16))
        return jnp.concatenate(mixed, axis=-1)

    def attn_out(r0, rows):
        return _dot_tn(at_ref[0, :, r0:r0 + rows], wo_ref[0:n_a, :])

    def pool_out(r0, rows, y):
        return y + _dot(pool_mix(r0, rows), wo_ref[n_a:, :]) if pool else y

    def residual_and_norm(r0, rows, y):
        x1 = x_ref[0, r0:r0 + rows, :] + mod_ref[0, 2:3, :] * (_rms_normalise(y) * gains_ref[1:2, :])
        h = _rms_normalise(x1) * gains_ref[2:3, :]
        return x1, (h * (1.0 + mod_ref[0, 4:5, :]) + mod_ref[0, 3:4, :]).astype(BF16)

    def ffn_hidden(r0, rows, h):
        for i in range(FFN_HIDDEN // FFN_CHUNK):
            gate = _dot(h, wfi_ref[:, i * FFN_CHUNK:(i + 1) * FFN_CHUNK])
            up = _dot(h, wfi_ref[:, FFN_HIDDEN + i * FFN_CHUNK:FFN_HIDDEN + (i + 1) * FFN_CHUNK])
            hid_ref[r0:r0 + rows, i * FFN_CHUNK:(i + 1) * FFN_CHUNK] = (_silu(gate) * up).astype(BF16)

    def ffn_out(r0, rows, x1):
        z = _dot(hid_ref[r0:r0 + rows, :], wfo_ref[...])
        o_ref[0, r0:r0 + rows, :] = x1 + mod_ref[0, 5:6, :] * (_rms_normalise(z) * gains_ref[3:4, :])

    rows = min(tm // 2, POST_SUB_TILE)
    starts = list(range(0, tm, rows))
    n = len(starts)
    y = {p: pool_out(starts[p], rows, attn_out(starts[p], rows)) for p in range(min(2, n))}
    normed = {0: residual_and_norm(starts[0], rows, y.pop(0))}
    for p in range(n):
        x1, h = normed.pop(p)
        ffn_hidden(starts[p], rows, h)
        if p + 2 < n:
            y[p + 2] = pool_out(starts[p + 2], rows, attn_out(starts[p + 2], rows))
        if p + 1 < n:
            normed[p + 1] = residual_and_norm(starts[p + 1], rows, y.pop(p + 1))
        ffn_out(starts[p], rows, x1)


def _post(x, a_t, mod, gains, wo, wfi, wfo, pool_args, *, tm):
    b, s, d = x.shape
    per_batch_mod = mod.shape[0] > 1
    pool = pool_args is not None
    in_specs = [
        pl.BlockSpec((1, tm, d), lambda i, t: (i, t, 0)),
        pl.BlockSpec((1, a_t.shape[1], tm), lambda i, t: (i, 0, t)),
        pl.BlockSpec((1, SUBLANES, d), (lambda i, t: (i, 0, 0)) if per_batch_mod else (lambda i, t: (0, 0, 0))),
        _const_spec(gains.shape), _const_spec(wo.shape), _const_spec(wfi.shape), _const_spec(wfo.shape),
    ]
    args = [x, a_t, mod, gains, wo, wfi, wfo]
    scratch = [pltpu.VMEM((tm, FFN_HIDDEN), BF16)]
    if pool:
        u, wpool, pscale = pool_args
        nu = u.shape[2]
        per_tile = tm // POOL_HALO
        n_halo_blocks = s // POOL_HALO
        in_specs += [
            pl.BlockSpec((1, tm, nu), lambda i, t: (i, t, 0)),
            pl.BlockSpec((1, POOL_HALO, nu), lambda i, t: (i, jnp.maximum(t * per_tile - 1, 0), 0)),
            pl.BlockSpec((1, POOL_HALO, nu),
                         lambda i, t: (i, jnp.minimum((t + 1) * per_tile, n_halo_blocks - 1), 0)),
            _const_spec(wpool.shape), _const_spec(pscale.shape),
        ]
        args += [u, u, u, wpool, pscale]
        scratch.append(pltpu.VMEM((tm + 2 * POOL_HALO, nu), F32))
    return pl.pallas_call(
        functools.partial(_post_kernel, pool=pool, tm=tm, seq=s),
        grid=(b, s // tm),
        in_specs=in_specs,
        out_specs=pl.BlockSpec((1, tm, d), lambda i, t: (i, t, 0)),
        out_shape=jax.ShapeDtypeStruct((b, s, d), F32),
        scratch_shapes=scratch,
        compiler_params=_params(),
        name="post_even" if pool else "post_odd",
    )(*args)


def _pad_rows(a, rows):
    return jnp.pad(a, [(0, 0)] * (a.ndim - 2) + [(0, rows - a.shape[-2]), (0, 0)])


def kernel(x, c, ctx, c_ctx, w_mod, b_mod, g_pre_mix, g_post_mix, g_pre_ffn, g_post_ffn, we_in, we_out,
           we_q_gain, we_k_gain, we_pool, we_pool_scale, wo_in, wo_out, wo_sink, w_ffn_in, w_ffn_out):
    batch, seq, d = x.shape
    c_len = ctx.shape[1]
    depth = w_mod.shape[0]
    tm = TOKEN_TILE
    tm_ctx = c_len
    assert d == D_MODEL and w_ffn_out.shape[1] == FFN_HIDDEN and w_mod.shape[2] % MOD_COLS_TILE == 0
    assert seq % PRE_TILE == 0 and seq % TOKEN_TILE == 0 and seq % QUERY_TILE == 0 and seq % GRID_W == 0
    assert seq >= QUERY_SUB + 2 * WINDOW and c_len % min(c_len, PRE_SUB_TILE) == 0 and c_len % QUERY_SUB == 0
    assert max(POOL_WINDOWS) // 2 <= POOL_HALO

    cc = jnp.concatenate([c, c_ctx[None, :]], axis=0)
    cc = _pad_rows(cc, -(-(batch + 1) // SUBLANES) * SUBLANES)
    mods = _modulation(cc, w_mod, b_mod)

    cos, sin = _rope_tables(seq)
    _, lane_dim = _lane_slot_dim()
    head_mean = jnp.asarray(_head_mean_matrix(), BF16)

    for l in range(depth):
        even = l % 2 == 0
        i = l // 2
        with_ctx = l < depth - 1
        mod_x = _pad_rows(mods[l, :batch].reshape(batch, 6, d), SUBLANES)
        mod_c = _pad_rows(mods[l, batch:batch + 1].reshape(1, 6, d), SUBLANES)
        gains = _pad_rows(jnp.stack([g_pre_mix[l], g_post_mix[l], g_pre_ffn[l], g_post_ffn[l]]), SUBLANES)
        wfi = w_ffn_in[l].astype(BF16)
        wfo = w_ffn_out[l].astype(BF16)

        if even:
            n_heads, n_kv = 8, 2
            w_full, w_out = we_in[i], we_out[i].astype(BF16)
            norm_args = (head_mean, we_q_gain[i][lane_dim][None, :], we_k_gain[i][lane_dim][None, :])
            sink = None
            mode = "dense"
        else:
            n_heads, n_kv = 16, 4
            w_full, w_out = wo_in[i], wo_out[i].astype(BF16)
            norm_args = None
            sink = wo_sink[i]
            mode = "window"
        q_w, kv_w = n_heads * HEAD_DIM, n_kv * HEAD_DIM
        cols = np.concatenate([_q_cols(n_heads), _k_cols(n_kv, q_w), np.arange(q_w + kv_w, w_full.shape[1])])
        w_in = w_full[:, cols].astype(BF16)
        widths = (q_w, kv_w, kv_w, w_full.shape[1] - q_w - 2 * kv_w)

        lat = _pre(x, mod_x, gains, w_in, norm_args, (cos, sin), widths=widths, tm=PRE_TILE)
        con = _pre(ctx, mod_c, gains, w_in, norm_args, None, widths=widths, tm=tm_ctx)
        a_t = _attention(lat[0], (lat[1], lat[2]), (con[1], con[2]), sink, mode=mode, tq=QUERY_TILE)
        pool_w = (we_pool[i].astype(BF16), we_pool_scale[i][None, :]) if even else None
        x = _post(x, a_t, mod_x, gains, w_out, wfi, wfo, (lat[3],) + pool_w if even else None, tm=tm)
        if with_ctx:
            ac_t = _attention(con[0], None, (con[1], con[2]), sink, mode="ctx", tq=c_len)
            ctx = _post(ctx, ac_t, mod_c, gains, w_out, wfi, wfo,
                        (con[3],) + pool_w if even else None, tm=tm_ctx)
    return x
```

```python
import functools
import math

import numpy as np
import jax
import jax.numpy as jnp
from jax import lax
from jax.experimental import pallas as pl
from jax.experimental.pallas import tpu as pltpu

D_MODEL = 1024
HEAD_DIM = 64
GQA_GROUP = 4
GRID_W = 64
ROPE_THETA = 10000.0
EPS = 1e-6
WINDOW = 128
POOL_WINDOWS = (2, 4, 8, 16)
POOL_HALO = 8
FFN_HIDDEN = 2816
FFN_CHUNK = 256
LANES = 128
SUBLANES = 8
ROPE_HALF = HEAD_DIM // 2
LOG2E = math.log2(math.e)
Q_SCALE = LOG2E / math.sqrt(HEAD_DIM)
VMEM_LIMIT = 56 * 1024 * 1024
MOD_COLS_TILE = 1536
TOKEN_TILE = 512
POST_SUB_TILE = 256
PRE_TILE = 1024
PRE_SUB_TILE = 512
QUERY_TILE = 2048
QUERY_SUB = LANES
ITEM_CHUNKS = {"dense": 2, "window": 4, "ctx": 2}
KEY_BLOCK = 1024
VALUE_GROUP = 1024
VT_ONES_ROWS = 16
VT_CHUNK = LANES + VT_ONES_ROWS

F32 = jnp.float32
BF16 = jnp.bfloat16


def _lane_slot_dim():
    quarter = HEAD_DIM // 4
    lane = np.arange(LANES)
    part = lane // ROPE_HALF
    i = lane % ROPE_HALF
    dim = np.where(i < quarter, i, ROPE_HALF + (i - quarter)) + np.where(part >= 2, quarter, 0)
    return part % 2, dim


def _chunk_heads(c):
    head_a = (c // GQA_GROUP) * 2 * GQA_GROUP + c % GQA_GROUP
    return head_a, head_a + GQA_GROUP


def _q_cols(n_heads):
    slot, dim = _lane_slot_dim()
    cols = []
    for c in range(n_heads // 2):
        head_a, head_b = _chunk_heads(c)
        cols.append(np.where(slot == 0, head_a, head_b) * HEAD_DIM + dim)
    return np.concatenate(cols)


def _k_cols(n_kv, base):
    slot, dim = _lane_slot_dim()
    return np.concatenate([base + (2 * m + slot) * HEAD_DIM + dim for m in range(n_kv // 2)])


def _head_mean_matrix():
    slot, _ = _lane_slot_dim()
    mean = (slot[:, None] == slot[None, :]).astype(np.float32) / HEAD_DIM
    return np.concatenate([mean, mean], axis=0)


def _rope_tables(seq):
    quarter = HEAD_DIM // 4
    freqs = ROPE_THETA ** (-jnp.arange(quarter, dtype=F32) / quarter)
    t = jnp.arange(seq, dtype=jnp.int32)
    rows = (t // GRID_W).astype(F32)[:, None] * freqs[None, :]
    cols = (t % GRID_W).astype(F32)[:, None] * freqs[None, :]
    ang = jnp.concatenate([rows, cols], axis=-1)
    cos = jnp.tile(jnp.cos(ang), (1, 4))
    sin = jnp.sin(ang)
    sin = jnp.concatenate([-sin, -sin, sin, sin], axis=-1)
    return cos, sin


def _rms_normalise(x):
    return x * lax.rsqrt(jnp.mean(x * x, axis=-1, keepdims=True) + EPS)


def _silu(x):
    return x * (1.0 / (1.0 + jnp.exp(-x)))


def _dot(a, b):
    return jnp.dot(a, b, preferred_element_type=F32)


def _dot_tn(a_t, b):
    return lax.dot_general(a_t, b, (((0,), (0,)), ((), ())), preferred_element_type=F32)


def _const_spec(shape):
    return pl.BlockSpec(shape, lambda *_: (0,) * len(shape), pipeline_mode=pl.Buffered(1))


def _params():
    return pltpu.CompilerParams(dimension_semantics=("parallel", "parallel"), vmem_limit_bytes=VMEM_LIMIT)


def _mod_kernel(c_ref, w_ref, b_ref, o_ref):
    s = _silu(c_ref[...]).astype(BF16)
    o_ref[0] = _dot(s, w_ref[0].astype(BF16)) + b_ref[0]


def _modulation(cc, w_mod, b_mod):
    depth, d, n = w_mod.shape
    rows = cc.shape[0]
    tn = MOD_COLS_TILE
    return pl.pallas_call(
        _mod_kernel,
        grid=(depth, n // tn),
        in_specs=[
            pl.BlockSpec((rows, d), lambda l, j: (0, 0)),
            pl.BlockSpec((1, d, tn), lambda l, j: (l, 0, j)),
            pl.BlockSpec((1, 1, tn), lambda l, j: (l, 0, j)),
        ],
        out_specs=pl.BlockSpec((1, rows, tn), lambda l, j: (l, 0, j)),
        out_shape=jax.ShapeDtypeStruct((depth, rows, n), F32),
        compiler_params=_params(),
        name="modulation",
    )(cc, w_mod, b_mod.reshape(depth, 1, n))


def _pre_kernel(*refs, n_q, n_k, n_v, n_u, qk_norm, rope):
    x_ref, mod_ref, gains_ref, w_ref = refs[:4]
    pos = 4
    if qk_norm:
        pm_ref, qg_ref, kg_ref = refs[pos:pos + 3]
        pos += 3
    if rope:
        cos_ref, sin_ref = refs[pos:pos + 2]
        pos += 2
    qt_ref, k_ref, vt_ref = refs[pos:pos + 3]
    u_ref = refs[pos + 3] if n_u else None

    def modulated(rows):
        h = _rms_normalise(x_ref[0, rows, :]) * gains_ref[0:1, :]
        return (h * (1.0 + mod_ref[0, 1:2, :]) + mod_ref[0, 0:1, :]).astype(BF16)

    def head_chunk(c, gain_ref, rows):
        if qk_norm:
            c2 = c * c
            hi = c2.astype(BF16)
            lo = (c2 - hi.astype(F32)).astype(BF16)
            ms = _dot(jnp.concatenate([hi, lo], axis=1), pm_ref[...])
            c = c * lax.rsqrt(ms + EPS) * gain_ref[...]
        if rope:
            c = c * cos_ref[rows, :] + pltpu.roll(c, LANES // 2, 1) * sin_ref[rows, :]
        return c

    def project(rows, h):
        p = _dot(h, w_ref[...])
        for j in range(n_q // LANES):
            sl = slice(j * LANES, (j + 1) * LANES)
            c = head_chunk(p[:, sl], qg_ref if qk_norm else None, rows) * Q_SCALE
            qt_ref[0, sl, rows] = c.T.astype(BF16)
        for j in range(n_k // LANES):
            sl = slice(j * LANES, (j + 1) * LANES)
            c = head_chunk(p[:, n_q + j * LANES:n_q + (j + 1) * LANES], kg_ref if qk_norm else None, rows)
            k_ref[0, rows, sl] = c.astype(BF16)
        ones = jnp.ones((VT_ONES_ROWS, p.shape[0]), BF16)
        for j in range(n_v // LANES):
            r0 = j * VT_CHUNK
            vt_ref[0, r0:r0 + LANES, rows] = (
                p[:, n_q + n_k + j * LANES:n_q + n_k + (j + 1) * LANES].T.astype(BF16))
            vt_ref[0, r0 + LANES:r0 + VT_CHUNK, rows] = ones
        if n_u:
            u_ref[0, rows, :] = p[:, n_q + n_k + n_v:]

    tm = x_ref.shape[1]
    step = min(tm, PRE_SUB_TILE)
    parts = [slice(r, r + step) for r in range(0, tm, step)]
    h_next = modulated(parts[0])
    for i, rows in enumerate(parts):
        h_cur = h_next
        if i + 1 < len(parts):
            h_next = modulated(parts[i + 1])
        project(rows, h_cur)


def _pre(x, mod, gains, w, norm_args, rope_args, *, widths, tm):
    n_q, n_k, n_v, n_u = widths
    b, s, d = x.shape
    per_batch_mod = mod.shape[0] > 1
    qk_norm = norm_args is not None
    rope = rope_args is not None
    in_specs = [
        pl.BlockSpec((1, tm, d), lambda i, t: (i, t, 0)),
        pl.BlockSpec((1, SUBLANES, d), (lambda i, t: (i, 0, 0)) if per_batch_mod else (lambda i, t: (0, 0, 0))),
        _const_spec(gains.shape),
        _const_spec(w.shape),
    ]
    args = [x, mod, gains, w]
    if qk_norm:
        in_specs += [_const_spec(a.shape) for a in norm_args]
        args += list(norm_args)
    if rope:
        in_specs += [pl.BlockSpec((tm, LANES), lambda i, t: (t, 0))] * 2
        args += list(rope_args)
    vt_rows = n_v // LANES * VT_CHUNK
    out_shape = [jax.ShapeDtypeStruct((b, n_q, s), BF16), jax.ShapeDtypeStruct((b, s, n_k), BF16),
                 jax.ShapeDtypeStruct((b, vt_rows, s), BF16)]
    out_specs = [pl.BlockSpec((1, n_q, tm), lambda i, t: (i, 0, t)),
                 pl.BlockSpec((1, tm, n_k), lambda i, t: (i, t, 0)),
                 pl.BlockSpec((1, vt_rows, tm), lambda i, t: (i, 0, t))]
    if n_u:
        out_shape.append(jax.ShapeDtypeStruct((b, s, n_u), F32))
        out_specs.append(pl.BlockSpec((1, tm, n_u), lambda i, t: (i, t, 0)))
    return pl.pallas_call(
        functools.partial(_pre_kernel, n_q=n_q, n_k=n_k, n_v=n_v, n_u=n_u, qk_norm=qk_norm, rope=rope),
        grid=(b, s // tm),
        in_specs=in_specs,
        out_specs=out_specs,
        out_shape=out_shape,
        compiler_params=_params(),
        name="pre_even" if qk_norm else "pre_odd",
    )(*args)


def _attn_kernel(*refs, mode, n_pairs, n_sub, tq, seq, use_sink, cpi):
    qt_ref = refs[0]
    pos = 1
    if mode != "ctx":
        k_ref, vt_ref = refs[pos:pos + 2]
        pos += 2
    kc_ref, vct_ref = refs[pos:pos + 2]
    pos += 2
    if use_sink:
        sink_ref = refs[pos]
        pos += 1
    o_ref, s_ref, p_ref = refs[pos:pos + 3]

    sub = QUERY_SUB
    row = lax.broadcasted_iota(jnp.int32, (LANES, 1), 0)
    slot_a = (row // ROPE_HALF) % 2 == 0
    band = sub + 2 * WINDOW
    if mode == "window":
        rel = (lax.broadcasted_iota(jnp.int32, (band, sub), 0)
               - lax.broadcasted_iota(jnp.int32, (band, sub), 1))

    def block_geometry(sb):
        q0 = pl.program_id(1) * tq + sb * sub
        start = pl.multiple_of(jnp.clip(q0 - WINDOW, 0, seq - band), LANES)
        bias = jnp.where(jnp.abs(rel + (start - q0)) <= WINDOW, 0.0, -jnp.inf).astype(F32)
        return start, jnp.concatenate([bias] * (2 * cpi), axis=1)

    c_len = kc_ref.shape[1]
    first = {"dense": seq, "window": band, "ctx": 0}[mode]
    blocks = [(True, r, min(KEY_BLOCK, first - r), r) for r in range(0, first, KEY_BLOCK)]
    blocks += [(False, r, min(KEY_BLOCK, c_len - r), first + r) for r in range(0, c_len, KEY_BLOCK)]
    value_groups = {}
    for bi, (latent, r0, rows, b0) in enumerate(blocks):
        if r0 % VALUE_GROUP:
            _, g0, g_rows, gb0 = value_groups.pop(bi - 1)
            value_groups[bi] = (latent, g0, g_rows + rows, gb0)
        else:
            value_groups[bi] = (latent, r0, rows, b0)

    def item_chunks(item):
        return range(cpi * item, cpi * (item + 1))

    def item_pair(item):
        return cpi * item // GQA_GROUP

    def item_heads(item):
        return sum((_chunk_heads(c) for c in item_chunks(item)), ())

    def query_operand(sb, item):
        cols = []
        for c in item_chunks(item):
            qt = qt_ref[0, c * LANES:(c + 1) * LANES, sb * sub:(sb + 1) * sub]
            zero = jnp.zeros_like(qt)
            cols += [jnp.where(slot_a, qt, zero), jnp.where(slot_a, zero, qt)]
        return jnp.concatenate(cols, axis=1)

    def score_block(slot, item, geometry, rhs, blk, m):
        latent, r0, rows, b0 = blk
        psl = slice(item_pair(item) * LANES, (item_pair(item) + 1) * LANES)
        if not latent:
            s = _dot(kc_ref[0, r0:r0 + rows, psl], rhs)
        elif mode == "dense":
            s = _dot(k_ref[0, r0:r0 + rows, psl], rhs)
        else:
            start, bias = geometry
            s = _dot(k_ref[0, pl.ds(pl.multiple_of(start + r0, LANES), rows), psl], rhs) + bias[r0:r0 + rows]
        s_ref[slot, b0:b0 + rows, :] = s
        top = s.max(axis=0, keepdims=True)
        return top if m is None else jnp.maximum(m, top)

    def exp_block(slot, blk, m):
        _, _, rows, b0 = blk
        p_ref[slot, b0:b0 + rows, :] = jnp.exp2(s_ref[slot, b0:b0 + rows, :] - m).astype(BF16)

    def value_block(slot, item, geometry, blk, acc):
        latent, r0, rows, b0 = blk
        pair = item_pair(item)
        vsl = slice(pair * VT_CHUNK, (pair + 1) * VT_CHUNK)
        if not latent:
            v_t = vct_ref[0, vsl, r0:r0 + rows]
        elif mode == "dense":
            v_t = vt_ref[0, vsl, r0:r0 + rows]
        else:
            v_t = vt_ref[0, vsl, pl.ds(pl.multiple_of(geometry[0] + r0, LANES), rows)]
        pv = _dot(v_t, p_ref[slot, b0:b0 + rows, :])
        return pv if acc is None else acc + pv

    def with_sink(item, m):
        if not use_sink:
            return m, None
        sink = jnp.concatenate([jnp.full((1, sub), sink_ref[h], F32) for h in item_heads(item)], axis=1) * LOG2E
        m = jnp.maximum(m, sink)
        return m, jnp.exp2(sink - m)

    def store_output(sb, item, o4, sink_term):
        denom = o4[LANES:LANES + 1, :]
        if use_sink:
            denom = denom + sink_term
        o4 = o4[:LANES, :] * (1.0 / denom)
        for j, h in enumerate(item_heads(item)):
            rows = slice(0, HEAD_DIM) if j % 2 == 0 else slice(HEAD_DIM, 2 * HEAD_DIM)
            o_ref[0, h * HEAD_DIM:(h + 1) * HEAD_DIM, sb * sub:(sb + 1) * sub] = (
                o4[rows, j * sub:(j + 1) * sub].astype(BF16))

    items = [(sb, it) for sb in range(n_sub) for it in range(GQA_GROUP * n_pairs // cpi)]
    geometries = [block_geometry(sb) if mode == "window" else None for sb in range(n_sub)]
    n_items = len(items)
    m_cur = sink_cur = sink_prev = None
    for i in range(-1, n_items + 1):
        nxt, prv = i + 1, i - 1
        if nxt < n_items:
            rhs = query_operand(*items[nxt])
        m_next = o_acc = None
        for bi, blk in enumerate(blocks):
            if nxt < n_items:
                m_next = score_block(nxt % 2, items[nxt][1], geometries[items[nxt][0]], rhs, blk, m_next)
            if 0 <= i < n_items:
                exp_block(i % 2, blk, m_cur)
            if 0 <= prv and bi in value_groups:
                o_acc = value_block(prv % 2, items[prv][1], geometries[items[prv][0]], value_groups[bi], o_acc)
        if 0 <= prv:
            store_output(*items[prv], o_acc, sink_prev)
        sink_prev = sink_cur
        if nxt < n_items:
            m_cur, sink_cur = with_sink(items[nxt][1], m_next)


def _attention(qt, kv, kv_ctx, sink, *, mode, tq):
    b, wq, sq = qt.shape
    n_pairs = wq // (2 * GQA_GROUP * HEAD_DIM)
    kc, vct = kv_ctx
    c_len = kc.shape[1]
    use_sink = sink is not None
    in_specs = [pl.BlockSpec((1, wq, tq), lambda i, t: (i, 0, t))]
    args = [qt]
    seq = sq
    if mode != "ctx":
        k, vt = kv
        seq = k.shape[1]
        in_specs += [pl.BlockSpec((1, seq, k.shape[2]), lambda i, t: (i, 0, 0)),
                     pl.BlockSpec((1, vt.shape[1], seq), lambda i, t: (i, 0, 0))]
        args += [k, vt]
    in_specs += [pl.BlockSpec((1, c_len, kc.shape[2]), lambda i, t: (i, 0, 0)),
                 pl.BlockSpec((1, vct.shape[1], c_len), lambda i, t: (i, 0, 0))]
    args += [kc, vct]
    if use_sink:
        in_specs.append(pl.BlockSpec(memory_space=pltpu.SMEM))
        args.append(sink)
    n_keys = c_len + {"dense": seq, "window": QUERY_SUB + 2 * WINDOW, "ctx": 0}[mode]
    cpi = ITEM_CHUNKS[mode]
    return pl.pallas_call(
        functools.partial(_attn_kernel, mode=mode, n_pairs=n_pairs, n_sub=tq // QUERY_SUB, tq=tq, seq=seq,
                          use_sink=use_sink, cpi=cpi),
        grid=(b, sq // tq),
        in_specs=in_specs,
        out_specs=pl.BlockSpec((1, wq, tq), lambda i, t: (i, 0, t)),
        out_shape=jax.ShapeDtypeStruct((b, wq, sq), BF16),
        scratch_shapes=[pltpu.VMEM((2, n_keys, 2 * cpi * QUERY_SUB), F32),
                        pltpu.VMEM((2, n_keys, 2 * cpi * QUERY_SUB), BF16)],
        compiler_params=_params(),
        name="attn_" + mode,
    )(*args)


def _post_kernel(*refs, pool, tm, seq):
    x_ref, at_ref, mod_ref, gains_ref, wo_ref, wfi_ref, wfo_ref = refs[:7]
    pos = 7
    if pool:
        u_ref, uprev_ref, unext_ref, wpool_ref, pscale_ref = refs[pos:pos + 5]
        pos += 5
    o_ref = refs[pos]
    hid_ref = refs[pos + 1]
    if pool:
        ext_ref = refs[pos + 2]

    n_a = at_ref.shape[1]
    if pool:
        t = pl.program_id(1)
        last = pl.num_programs(1) - 1
        halo = jnp.zeros((POOL_HALO, ext_ref.shape[1]), F32)
        ext_ref[0:POOL_HALO, :] = jnp.where(t > 0, uprev_ref[0], halo)
        ext_ref[POOL_HALO:POOL_HALO + tm, :] = u_ref[0]
        ext_ref[POOL_HALO + tm:, :] = jnp.where(t < last, unext_ref[0], halo)

    def pool_mix(r0, rows):
        n_ext = rows + 2 * POOL_HALO
        tok = t * tm + r0 + lax.broadcasted_iota(jnp.int32, (rows, 1), 0)

        def ahead(v, k):
            return pltpu.roll(v, n_ext - k, 0)

        def behind(v, k):
            return pltpu.roll(v, k, 0)

        mixed = []
        for g, w in enumerate(POOL_WINDOWS):
            sl = slice(g * LANES, (g + 1) * LANES)
            e = ext_ref[r0:r0 + n_ext, sl]
            run, length = e, 1
            while length < w // 2:
                run, length = run + ahead(run, length), 2 * length
            total = run + behind(run, w // 2)
            total = total[POOL_HALO:POOL_HALO + rows]
            cnt = jnp.minimum(tok + (w - w // 2), seq) - jnp.maximum(tok - w // 2, 0)
            diff = total / cnt.astype(F32) - e[POOL_HALO:POOL_HALO + rows]
            mixed.append((_dot(diff.astype(BF16), wpool_ref[g]) * pscale_ref[:, sl]).astype(BF16))
        return jnp.concatenate(mixed, axis=-1)

    def attn_out(r0, rows):
        return _dot_tn(at_ref[0, :, r0:r0 + rows], wo_ref[0:n_a, :])

    def pool_out(r0, rows, y):
        return y + _dot(pool_mix(r0, rows), wo_ref[n_a:, :]) if pool else y

    def residual_and_norm(r0, rows, y):
        x1 = x_ref[0, r0:r0 + rows, :] + mod_ref[0, 2:3, :] * (_rms_normalise(y) * gains_ref[1:2, :])
        h = _rms_normalise(x1) * gains_ref[2:3, :]
        return x1, (h * (1.0 + mod_ref[0, 4:5, :]) + mod_ref[0, 3:4, :]).astype(BF16)

    def ffn_hidden(r0, rows, h):
        for i in range(FFN_HIDDEN // FFN_CHUNK):
            gate = _dot(h, wfi_ref[:, i * FFN_CHUNK:(i + 1) * FFN_CHUNK])
            up = _dot(h, wfi_ref[:, FFN_HIDDEN + i * FFN_CHUNK:FFN_HIDDEN + (i + 1) * FFN_CHUNK])
            hid_ref[r0:r0 + rows, i * FFN_CHUNK:(i + 1) * FFN_CHUNK] = (_silu(gate) * up).astype(BF16)

    def ffn_out(r0, rows, x1):
        z = _dot(hid_ref[r0:r0 + rows, :], wfo_ref[...])
        o_ref[0, r0:r0 + rows, :] = x1 + mod_ref[0, 5:6, :] * (_rms_normalise(z) * gains_ref[3:4, :])

    rows = min(tm // 2, POST_SUB_TILE)
    starts = list(range(0, tm, rows))
    n = len(starts)
    y = {p: pool_out(starts[p], rows, attn_out(starts[p], rows)) for p in range(min(2, n))}
    normed = {0: residual_and_norm(starts[0], rows, y.pop(0))}
    for p in range(n):
        x1, h = normed.pop(p)
        ffn_hidden(starts[p], rows, h)
        if p + 2 < n:
            y[p + 2] = pool_out(starts[p + 2], rows, attn_out(starts[p + 2], rows))
        if p + 1 < n:
            normed[p + 1] = residual_and_norm(starts[p + 1], rows, y.pop(p + 1))
        ffn_out(starts[p], rows, x1)


def _post(x, a_t, mod, gains, wo, wfi, wfo, pool_args, *, tm):
    b, s, d = x.shape
    per_batch_mod = mod.shape[0] > 1
    pool = pool_args is not None
    in_specs = [
        pl.BlockSpec((1, tm, d), lambda i, t: (i, t, 0)),
        pl.BlockSpec((1, a_t.shape[1], tm), lambda i, t: (i, 0, t)),
        pl.BlockSpec((1, SUBLANES, d), (lambda i, t: (i, 0, 0)) if per_batch_mod else (lambda i, t: (0, 0, 0))),
        _const_spec(gains.shape), _const_spec(wo.shape), _const_spec(wfi.shape), _const_spec(wfo.shape),
    ]
    args = [x, a_t, mod, gains, wo, wfi, wfo]
    scratch = [pltpu.VMEM((tm, FFN_HIDDEN), BF16)]
    if pool:
        u, wpool, pscale = pool_args
        nu = u.shape[2]
        per_tile = tm // POOL_HALO
        n_halo_blocks = s // POOL_HALO
        in_specs += [
            pl.BlockSpec((1, tm, nu), lambda i, t: (i, t, 0)),
            pl.BlockSpec((1, POOL_HALO, nu), lambda i, t: (i, jnp.maximum(t * per_tile - 1, 0), 0)),
            pl.BlockSpec((1, POOL_HALO, nu),
                         lambda i, t: (i, jnp.minimum((t + 1) * per_tile, n_halo_blocks - 1), 0)),
            _const_spec(wpool.shape), _const_spec(pscale.shape),
        ]
        args += [u, u, u, wpool, pscale]
        scratch.append(pltpu.VMEM((tm + 2 * POOL_HALO, nu), F32))
    return pl.pallas_call(
        functools.partial(_post_kernel, pool=pool, tm=tm, seq=s),
        grid=(b, s // tm),
        in_specs=in_specs,
        out_specs=pl.BlockSpec((1, tm, d), lambda i, t: (i, t, 0)),
        out_shape=jax.ShapeDtypeStruct((b, s, d), F32),
        scratch_shapes=scratch,
        compiler_params=_params(),
        name="post_even" if pool else "post_odd",
    )(*args)


def _pad_rows(a, rows):
    return jnp.pad(a, [(0, 0)] * (a.ndim - 2) + [(0, rows - a.shape[-2]), (0, 0)])


def kernel(x, c, ctx, c_ctx, w_mod, b_mod, g_pre_mix, g_post_mix, g_pre_ffn, g_post_ffn, we_in, we_out,
           we_q_gain, we_k_gain, we_pool, we_pool_scale, wo_in, wo_out, wo_sink, w_ffn_in, w_ffn_out):
    batch, seq, d = x.shape
    c_len = ctx.shape[1]
    depth = w_mod.shape[0]
    tm = TOKEN_TILE
    tm_ctx = c_len
    assert d == D_MODEL and w_ffn_out.shape[1] == FFN_HIDDEN and w_mod.shape[2] % MOD_COLS_TILE == 0
    assert seq % PRE_TILE == 0 and seq % TOKEN_TILE == 0 and seq % QUERY_TILE == 0 and seq % GRID_W == 0
    assert seq >= QUERY_SUB + 2 * WINDOW and c_len % min(c_len, PRE_SUB_TILE) == 0 and c_len % QUERY_SUB == 0
    assert max(POOL_WINDOWS) // 2 <= POOL_HALO

    cc = jnp.concatenate([c, c_ctx[None, :]], axis=0)
    cc = _pad_rows(cc, -(-(batch + 1) // SUBLANES) * SUBLANES)
    mods = _modulation(cc, w_mod, b_mod)

    cos, sin = _rope_tables(seq)
    _, lane_dim = _lane_slot_dim()
    head_mean = jnp.asarray(_head_mean_matrix(), BF16)

    for l in range(depth):
        even = l % 2 == 0
        i = l // 2
        with_ctx = l < depth - 1
        mod_x = _pad_rows(mods[l, :batch].reshape(batch, 6, d), SUBLANES)
        mod_c = _pad_rows(mods[l, batch:batch + 1].reshape(1, 6, d), SUBLANES)
        gains = _pad_rows(jnp.stack([g_pre_mix[l], g_post_mix[l], g_pre_ffn[l], g_post_ffn[l]]), SUBLANES)
        wfi = w_ffn_in[l].astype(BF16)
        wfo = w_ffn_out[l].astype(BF16)

        if even:
            n_heads, n_kv = 8, 2
            w_full, w_out = we_in[i], we_out[i].astype(BF16)
            norm_args = (head_mean, we_q_gain[i][lane_dim][None, :], we_k_gain[i][lane_dim][None, :])
            sink = None
            mode = "dense"
        else:
            n_heads, n_kv = 16, 4
            w_full, w_out = wo_in[i], wo_out[i].astype(BF16)
            norm_args = None
            sink = wo_sink[i]
            mode = "window"
        q_w, kv_w = n_heads * HEAD_DIM, n_kv * HEAD_DIM
        cols = np.concatenate([_q_cols(n_heads), _k_cols(n_kv, q_w), np.arange(q_w + kv_w, w_full.shape[1])])
        w_in = w_full[:, cols].astype(BF16)
        widths = (q_w, kv_w, kv_w, w_full.shape[1] - q_w - 2 * kv_w)

        lat = _pre(x, mod_x, gains, w_in, norm_args, (cos, sin), widths=widths, tm=PRE_TILE)
        con = _pre(ctx, mod_c, gains, w_in, norm_args, None, widths=widths, tm=tm_ctx)
        a_t = _attention(lat[0], (lat[1], lat[2]), (con[1], con[2]), sink, mode=mode, tq=QUERY_TILE)
        pool_w = (we_pool[i].astype(BF16), we_pool_scale[i][None, :]) if even else None
        x = _post(x, a_t, mod_x, gains, w_out, wfi, wfo, (lat[3],) + pool_w if even else None, tm=tm)
        if with_ctx:
            ac_t = _attention(con[0], None, (con[1], con[2]), sink, mode="ctx", tq=c_len)
            ctx = _post(ctx, ac_t, mod_c, gains, w_out, wfi, wfo,
                        (con[3],) + pool_w if even else None, tm=tm_ctx)
    return x
```

```python
import functools
import math

import numpy as np
import jax
import jax.numpy as jnp
from jax import lax
from jax.experimental import pallas as pl
from jax.experimental.pallas import tpu as pltpu

D_MODEL = 1024
HEAD_DIM = 64
GQA_GROUP = 4
GRID_W = 64
ROPE_THETA = 10000.0
EPS = 1e-6
WINDOW = 128
POOL_WINDOWS = (2, 4, 8, 16)
POOL_HALO = 8
FFN_HIDDEN = 2816
FFN_CHUNK = 256
LANES = 128
SUBLANES = 8
ROPE_HALF = HEAD_DIM // 2
LOG2E = math.log2(math.e)
Q_SCALE = LOG2E / math.sqrt(HEAD_DIM)
VMEM_LIMIT = 56 * 1024 * 1024
MOD_COLS_TILE = 1536
TOKEN_TILE = 512
POST_SUB_TILE = 256
PRE_TILE = 1024
PRE_SUB_TILE = 512
QUERY_TILE = 2048
QUERY_SUB = LANES
ITEM_CHUNKS = {"dense": 2, "window": 4, "ctx": 2}
KEY_BLOCK = 1024
VALUE_GROUP = 1024
VT_ONES_ROWS = 16
VT_CHUNK = LANES + VT_ONES_ROWS

F32 = jnp.float32
BF16 = jnp.bfloat16


def _lane_slot_dim():
    quarter = HEAD_DIM // 4
    lane = np.arange(LANES)
    part = lane // ROPE_HALF
    i = lane % ROPE_HALF
    dim = np.where(i < quarter, i, ROPE_HALF + (i - quarter)) + np.where(part >= 2, quarter, 0)
    return part % 2, dim


def _chunk_heads(c):
    head_a = (c // GQA_GROUP) * 2 * GQA_GROUP + c % GQA_GROUP
    return head_a, head_a + GQA_GROUP


def _q_cols(n_heads):
    slot, dim = _lane_slot_dim()
    cols = []
    for c in range(n_heads // 2):
        head_a, head_b = _chunk_heads(c)
        cols.append(np.where(slot == 0, head_a, head_b) * HEAD_DIM + dim)
    return np.concatenate(cols)


def _k_cols(n_kv, base):
    slot, dim = _lane_slot_dim()
    return np.concatenate([base + (2 * m + slot) * HEAD_DIM + dim for m in range(n_kv // 2)])


def _head_mean_matrix():
    slot, _ = _lane_slot_dim()
    mean = (slot[:, None] == slot[None, :]).astype(np.float32) / HEAD_DIM
    return np.concatenate([mean, mean], axis=0)


def _rope_tables(seq):
    quarter = HEAD_DIM // 4
    freqs = ROPE_THETA ** (-jnp.arange(quarter, dtype=F32) / quarter)
    t = jnp.arange(seq, dtype=jnp.int32)
    rows = (t // GRID_W).astype(F32)[:, None] * freqs[None, :]
    cols = (t % GRID_W).astype(F32)[:, None] * freqs[None, :]
    ang = jnp.concatenate([rows, cols], axis=-1)
    cos = jnp.tile(jnp.cos(ang), (1, 4))
    sin = jnp.sin(ang)
    sin = jnp.concatenate([-sin, -sin, sin, sin], axis=-1)
    return cos, sin


def _rms_normalise(x):
    return x * lax.rsqrt(jnp.mean(x * x, axis=-1, keepdims=True) + EPS)


def _silu(x):
    return x * (1.0 / (1.0 + jnp.exp(-x)))


def _dot(a, b):
    return jnp.dot(a, b, preferred_element_type=F32)


def _dot_tn(a_t, b):
    return lax.dot_general(a_t, b, (((0,), (0,)), ((), ())), preferred_element_type=F32)


def _const_spec(shape):
    return pl.BlockSpec(shape, lambda *_: (0,) * len(shape), pipeline_mode=pl.Buffered(1))


def _params():
    return pltpu.CompilerParams(dimension_semantics=("parallel", "parallel"), vmem_limit_bytes=VMEM_LIMIT)


def _mod_kernel(c_ref, w_ref, b_ref, o_ref):
    s = _silu(c_ref[...]).astype(BF16)
    o_ref[0] = _dot(s, w_ref[0].astype(BF16)) + b_ref[0]


def _modulation(cc, w_mod, b_mod):
    depth, d, n = w_mod.shape
    rows = cc.shape[0]
    tn = MOD_COLS_TILE
    return pl.pallas_call(
        _mod_kernel,
        grid=(depth, n // tn),
        in_specs=[
            pl.BlockSpec((rows, d), lambda l, j: (0, 0)),
            pl.BlockSpec((1, d, tn), lambda l, j: (l, 0, j)),
            pl.BlockSpec((1, 1, tn), lambda l, j: (l, 0, j)),
        ],
        out_specs=pl.BlockSpec((1, rows, tn), lambda l, j: (l, 0, j)),
        out_shape=jax.ShapeDtypeStruct((depth, rows, n), F32),
        compiler_params=_params(),
        name="modulation",
    )(cc, w_mod, b_mod.reshape(depth, 1, n))


def _pre_kernel(*refs, n_q, n_k, n_v, n_u, qk_norm, rope):
    x_ref, mod_ref, gains_ref, w_ref = refs[:4]
    pos = 4
    if qk_norm:
        pm_ref, qg_ref, kg_ref = refs[pos:pos + 3]
        pos += 3
    if rope:
        cos_ref, sin_ref = refs[pos:pos + 2]
        pos += 2
    qt_ref, k_ref, vt_ref = refs[pos:pos + 3]
    u_ref = refs[pos + 3] if n_u else None

    def modulated(rows):
        h = _rms_normalise(x_ref[0, rows, :]) * gains_ref[0:1, :]
        return (h * (1.0 + mod_ref[0, 1:2, :]) + mod_ref[0, 0:1, :]).astype(BF16)

    def head_chunk(c, gain_ref, rows):
        if qk_norm:
            c2 = c * c
            hi = c2.astype(BF16)
            lo = (c2 - hi.astype(F32)).astype(BF16)
            ms = _dot(jnp.concatenate([hi, lo], axis=1), pm_ref[...])
            c = c * lax.rsqrt(ms + EPS) * gain_ref[...]
        if rope:
            c = c * cos_ref[rows, :] + pltpu.roll(c, LANES // 2, 1) * sin_ref[rows, :]
        return c

    def project(rows, h):
        p = _dot(h, w_ref[...])
        for j in range(n_q // LANES):
            sl = slice(j * LANES, (j + 1) * LANES)
            c = head_chunk(p[:, sl], qg_ref if qk_norm else None, rows) * Q_SCALE
            qt_ref[0, sl, rows] = c.T.astype(BF16)
        for j in range(n_k // LANES):
            sl = slice(j * LANES, (j + 1) * LANES)
            c = head_chunk(p[:, n_q + j * LANES:n_q + (j + 1) * LANES], kg_ref if qk_norm else None, rows)
            k_ref[0, rows, sl] = c.astype(BF16)
        ones = jnp.ones((VT_ONES_ROWS, p.shape[0]), BF16)
        for j in range(n_v // LANES):
            r0 = j * VT_CHUNK
            vt_ref[0, r0:r0 + LANES, rows] = (
                p[:, n_q + n_k + j * LANES:n_q + n_k + (j + 1) * LANES].T.astype(BF16))
            vt_ref[0, r0 + LANES:r0 + VT_CHUNK, rows] = ones
        if n_u:
            u_ref[0, rows, :] = p[:, n_q + n_k + n_v:]

    tm = x_ref.shape[1]
    step = min(tm, PRE_SUB_TILE)
    parts = [slice(r, r + step) for r in range(0, tm, step)]
    h_next = modulated(parts[0])
    for i, rows in enumerate(parts):
        h_cur = h_next
        if i + 1 < len(parts):
            h_next = modulated(parts[i + 1])
        project(rows, h_cur)


def _pre(x, mod, gains, w, norm_args, rope_args, *, widths, tm):
    n_q, n_k, n_v, n_u = widths
    b, s, d = x.shape
    per_batch_mod = mod.shape[0] > 1
    qk_norm = norm_args is not None
    rope = rope_args is not None
    in_specs = [
        pl.BlockSpec((1, tm, d), lambda i, t: (i, t, 0)),
        pl.BlockSpec((1, SUBLANES, d), (lambda i, t: (i, 0, 0)) if per_batch_mod else (lambda i, t: (0, 0, 0))),
        _const_spec(gains.shape),
        _const_spec(w.shape),
    ]
    args = [x, mod, gains, w]
    if qk_norm:
        in_specs += [_const_spec(a.shape) for a in norm_args]
        args += list(norm_args)
    if rope:
        in_specs += [pl.BlockSpec((tm, LANES), lambda i, t: (t, 0))] * 2
        args += list(rope_args)
    vt_rows = n_v // LANES * VT_CHUNK
    out_shape = [jax.ShapeDtypeStruct((b, n_q, s), BF16), jax.ShapeDtypeStruct((b, s, n_k), BF16),
                 jax.ShapeDtypeStruct((b, vt_rows, s), BF16)]
    out_specs = [pl.BlockSpec((1, n_q, tm), lambda i, t: (i, 0, t)),
                 pl.BlockSpec((1, tm, n_k), lambda i, t: (i, t, 0)),
                 pl.BlockSpec((1, vt_rows, tm), lambda i, t: (i, 0, t))]
    if n_u:
        out_shape.append(jax.ShapeDtypeStruct((b, s, n_u), F32))
        out_specs.append(pl.BlockSpec((1, tm, n_u), lambda i, t: (i, t, 0)))
    return pl.pallas_call(
        functools.partial(_pre_kernel, n_q=n_q, n_k=n_k, n_v=n_v, n_u=n_u, qk_norm=qk_norm, rope=rope),
        grid=(b, s // tm),
        in_specs=in_specs,
        out_specs=out_specs,
        out_shape=out_shape,
        compiler_params=_params(),
        name="pre_even" if qk_norm else "pre_odd",
    )(*args)


def _attn_kernel(*refs, mode, n_pairs, n_sub, tq, seq, use_sink, cpi):
    qt_ref = refs[0]
    pos = 1
    if mode != "ctx":
        k_ref, vt_ref = refs[pos:pos + 2]
        pos += 2
    kc_ref, vct_ref = refs[pos:pos + 2]
    pos += 2
    if use_sink:
        sink_ref = refs[pos]
        pos += 1
    o_ref, s_ref, p_ref = refs[pos:pos + 3]

    sub = QUERY_SUB
    row = lax.broadcasted_iota(jnp.int32, (LANES, 1), 0)
    slot_a = (row // ROPE_HALF) % 2 == 0
    band = sub + 2 * WINDOW
    if mode == "window":
        rel = (lax.broadcasted_iota(jnp.int32, (band, sub), 0)
               - lax.broadcasted_iota(jnp.int32, (band, sub), 1))

    def block_geometry(sb):
        q0 = pl.program_id(1) * tq + sb * sub
        start = pl.multiple_of(jnp.clip(q0 - WINDOW, 0, seq - band), LANES)
        bias = jnp.where(jnp.abs(rel + (start - q0)) <= WINDOW, 0.0, -jnp.inf).astype(F32)
        return start, jnp.concatenate([bias] * (2 * cpi), axis=1)

    c_len = kc_ref.shape[1]
    first = {"dense": seq, "window": band, "ctx": 0}[mode]
    blocks = [(True, r, min(KEY_BLOCK, first - r), r) for r in range(0, first, KEY_BLOCK)]
    blocks += [(False, r, min(KEY_BLOCK, c_len - r), first + r) for r in range(0, c_len, KEY_BLOCK)]
    value_groups = {}
    for bi, (latent, r0, rows, b0) in enumerate(blocks):
        if r0 % VALUE_GROUP:
            _, g0, g_rows, gb0 = value_groups.pop(bi - 1)
            value_groups[bi] = (latent, g0, g_rows + rows, gb0)
        else:
            value_groups[bi] = (latent, r0, rows, b0)

    def item_chunks(item):
        return range(cpi * item, cpi * (item + 1))

    def item_pair(item):
        return cpi * item // GQA_GROUP

    def item_heads(item):
        return sum((_chunk_heads(c) for c in item_chunks(item)), ())

    def query_operand(sb, item):
        cols = []
        for c in item_chunks(item):
            qt = qt_ref[0, c * LANES:(c + 1) * LANES, sb * sub:(sb + 1) * sub]
            zero = jnp.zeros_like(qt)
            cols += [jnp.where(slot_a, qt, zero), jnp.where(slot_a, zero, qt)]
        return jnp.concatenate(cols, axis=1)

    def score_block(slot, item, geometry, rhs, blk, m):
        latent, r0, rows, b0 = blk
        psl = slice(item_pair(item) * LANES, (item_pair(item) + 1) * LANES)
        if not latent:
            s = _dot(kc_ref[0, r0:r0 + rows, psl], rhs)
        elif mode == "dense":
            s = _dot(k_ref[0, r0:r0 + rows, psl], rhs)
        else:
            start, bias = geometry
            s = _dot(k_ref[0, pl.ds(pl.multiple_of(start + r0, LANES), rows), psl], rhs) + bias[r0:r0 + rows]
        s_ref[slot, b0:b0 + rows, :] = s
        top = s.max(axis=0, keepdims=True)
        return top if m is None else jnp.maximum(m, top)

    def exp_block(slot, blk, m):
        _, _, rows, b0 = blk
        p_ref[slot, b0:b0 + rows, :] = jnp.exp2(s_ref[slot, b0:b0 + rows, :] - m).astype(BF16)

    def value_block(slot, item, geometry, blk, acc):
        latent, r0, rows, b0 = blk
        pair = item_pair(item)
        vsl = slice(pair * VT_CHUNK, (pair + 1) * VT_CHUNK)
        if not latent:
            v_t = vct_ref[0, vsl, r0:r0 + rows]
        elif mode == "dense":
            v_t = vt_ref[0, vsl, r0:r0 + rows]
        else:
            v_t = vt_ref[0, vsl, pl.ds(pl.multiple_of(geometry[0] + r0, LANES), rows)]
        pv = _dot(v_t, p_ref[slot, b0:b0 + rows, :])
        return pv if acc is None else acc + pv

    def with_sink(item, m):
        if not use_sink:
            return m, None
        sink = jnp.concatenate([jnp.full((1, sub), sink_ref[h], F32) for h in item_heads(item)], axis=1) * LOG2E
        m = jnp.maximum(m, sink)
        return m, jnp.exp2(sink - m)

    def store_output(sb, item, o4, sink_term):
        denom = o4[LANES:LANES + 1, :]
        if use_sink:
            denom = denom + sink_term
        o4 = o4[:LANES, :] * (1.0 / denom)
        for j, h in enumerate(item_heads(item)):
            rows = slice(0, HEAD_DIM) if j % 2 == 0 else slice(HEAD_DIM, 2 * HEAD_DIM)
            o_ref[0, h * HEAD_DIM:(h + 1) * HEAD_DIM, sb * sub:(sb + 1) * sub] = (
                o4[rows, j * sub:(j + 1) * sub].astype(BF16))

    items = [(sb, it) for sb in range(n_sub) for it in range(GQA_GROUP * n_pairs // cpi)]
    geometries = [block_geometry(sb) if mode == "window" else None for sb in range(n_sub)]
    n_items = len(items)
    m_cur = sink_cur = None
    for i in range(-1, n_items):
        nxt = i + 1
        if nxt < n_items:
            rhs = query_operand(*items[nxt])
        m_next = o_acc = None
        for bi, blk in enumerate(blocks):
            if nxt < n_items:
                m_next = score_block(nxt % 2, items[nxt][1], geometries[items[nxt][0]], rhs, blk, m_next)
            if 0 <= i:
                exp_block(i % 2, blk, m_cur)
                if bi in value_groups:
                    o_acc = value_block(i % 2, items[i][1], geometries[items[i][0]], value_groups[bi], o_acc)
        if 0 <= i:
            store_output(*items[i], o_acc, sink_cur)
        if nxt < n_items:
            m_cur, sink_cur = with_sink(items[nxt][1], m_next)


def _attention(qt, kv, kv_ctx, sink, *, mode, tq):
    b, wq, sq = qt.shape
    n_pairs = wq // (2 * GQA_GROUP * HEAD_DIM)
    kc, vct = kv_ctx
    c_len = kc.shape[1]
    use_sink = sink is not None
    in_specs = [pl.BlockSpec((1, wq, tq), lambda i, t: (i, 0, t))]
    args = [qt]
    seq = sq
    if mode != "ctx":
        k, vt = kv
        seq = k.shape[1]
        in_specs += [pl.BlockSpec((1, seq, k.shape[2]), lambda i, t: (i, 0, 0)),
                     pl.BlockSpec((1, vt.shape[1], seq), lambda i, t: (i, 0, 0))]
        args += [k, vt]
    in_specs += [pl.BlockSpec((1, c_len, kc.shape[2]), lambda i, t: (i, 0, 0)),
                 pl.BlockSpec((1, vct.shape[1], c_len), lambda i, t: (i, 0, 0))]
    args += [kc, vct]
    if use_sink:
        in_specs.append(pl.BlockSpec(memory_space=pltpu.SMEM))
        args.append(sink)
    n_keys = c_len + {"dense": seq, "window": QUERY_SUB + 2 * WINDOW, "ctx": 0}[mode]
    cpi = ITEM_CHUNKS[mode]
    return pl.pallas_call(
        functools.partial(_attn_kernel, mode=mode, n_pairs=n_pairs, n_sub=tq // QUERY_SUB, tq=tq, seq=seq,
                          use_sink=use_sink, cpi=cpi),
        grid=(b, sq // tq),
        in_specs=in_specs,
        out_specs=pl.BlockSpec((1, wq, tq), lambda i, t: (i, 0, t)),
        out_shape=jax.ShapeDtypeStruct((b, wq, sq), BF16),
        scratch_shapes=[pltpu.VMEM((2, n_keys, 2 * cpi * QUERY_SUB), F32),
                        pltpu.VMEM((2, n_keys, 2 * cpi * QUERY_SUB), BF16)],
        compiler_params=_params(),
        name="attn_" + mode,
    )(*args)


def _post_kernel(*refs, pool, tm, seq):
    x_ref, at_ref, mod_ref, gains_ref, wo_ref, wfi_ref, wfo_ref = refs[:7]
    pos = 7
    if pool:
        u_ref, uprev_ref, unext_ref, wpool_ref, pscale_ref = refs[pos:pos + 5]
        pos += 5
    o_ref = refs[pos]
    hid_ref = refs[pos + 1]
    if pool:
        ext_ref = refs[pos + 2]

    n_a = at_ref.shape[1]
    if pool:
        t = pl.program_id(1)
        last = pl.num_programs(1) - 1
        halo = jnp.zeros((POOL_HALO, ext_ref.shape[1]), F32)
        ext_ref[0:POOL_HALO, :] = jnp.where(t > 0, uprev_ref[0], halo)
        ext_ref[POOL_HALO:POOL_HALO + tm, :] = u_ref[0]
        ext_ref[POOL_HALO + tm:, :] = jnp.where(t < last, unext_ref[0], halo)

    def pool_mix(r0, rows):
        n_ext = rows + 2 * POOL_HALO
        tok = t * tm + r0 + lax.broadcasted_iota(jnp.int32, (rows, 1), 0)

        def ahead(v, k):
            return pltpu.roll(v, n_ext - k, 0)

        def behind(v, k):
            return pltpu.roll(v, k, 0)

        mixed = []
        for g, w in enumerate(POOL_WINDOWS):
            sl = slice(g * LANES, (g + 1) * LANES)
            e = ext_ref[r0:r0 + n_ext, sl]
            run, length = e, 1
            while length < w // 2:
                run, length = run + ahead(run, length), 2 * length
            total = run + behind(run, w // 2)
            total = total[POOL_HALO:POOL_HALO + rows]
            cnt = jnp.minimum(tok + (w - w // 2), seq) - jnp.maximum(tok - w // 2, 0)
            diff = total / cnt.astype(F32) - e[POOL_HALO:POOL_HALO + rows]
            mixed.append((_dot(diff.astype(BF16), wpool_ref[g]) * pscale_ref[:, sl]).astype(BF16))
        return jnp.concatenate(mixed, axis=-1)

    def attn_out(r0, rows):
        return _dot_tn(at_ref[0, :, r0:r0 + rows], wo_ref[0:n_a, :])

    def pool_out(r0, rows, y):
        return y + _dot(pool_mix(r0, rows), wo_ref[n_a:, :]) if pool else y

    def residual_and_norm(r0, rows, y):
        x1 = x_ref[0, r0:r0 + rows, :] + mod_ref[0, 2:3, :] * (_rms_normalise(y) * gains_ref[1:2, :])
        h = _rms_normalise(x1) * gains_ref[2:3, :]
        return x1, (h * (1.0 + mod_ref[0, 4:5, :]) + mod_ref[0, 3:4, :]).astype(BF16)

    def ffn_hidden(r0, rows, h):
        for i in range(FFN_HIDDEN // FFN_CHUNK):
            gate = _dot(h, wfi_ref[:, i * FFN_CHUNK:(i + 1) * FFN_CHUNK])
            up = _dot(h, wfi_ref[:, FFN_HIDDEN + i * FFN_CHUNK:FFN_HIDDEN + (i + 1) * FFN_CHUNK])
            hid_ref[r0:r0 + rows, i * FFN_CHUNK:(i + 1) * FFN_CHUNK] = (_silu(gate) * up).astype(BF16)

    def ffn_out(r0, rows, x1):
        z = _dot(hid_ref[r0:r0 + rows, :], wfo_ref[...])
        o_ref[0, r0:r0 + rows, :] = x1 + mod_ref[0, 5:6, :] * (_rms_normalise(z) * gains_ref[3:4, :])

    rows = min(tm // 2, POST_SUB_TILE)
    starts = list(range(0, tm, rows))
    n = len(starts)
    y = {p: pool_out(starts[p], rows, attn_out(starts[p], rows)) for p in range(min(2, n))}
    normed = {0: residual_and_norm(starts[0], rows, y.pop(0))}
    for p in range(n):
        x1, h = normed.pop(p)
        ffn_hidden(starts[p], rows, h)
        if p + 2 < n:
            y[p + 2] = pool_out(starts[p + 2], rows, attn_out(starts[p + 2], rows))
        if p + 1 < n:
            normed[p + 1] = residual_and_norm(starts[p + 1], rows, y.pop(p + 1))
        ffn_out(starts[p], rows, x1)


def _post(x, a_t, mod, gains, wo, wfi, wfo, pool_args, *, tm):
    b, s, d = x.shape
    per_batch_mod = mod.shape[0] > 1
    pool = pool_args is not None
    in_specs = [
        pl.BlockSpec((1, tm, d), lambda i, t: (i, t, 0)),
        pl.BlockSpec((1, a_t.shape[1], tm), lambda i, t: (i, 0, t)),
        pl.BlockSpec((1, SUBLANES, d), (lambda i, t: (i, 0, 0)) if per_batch_mod else (lambda i, t: (0, 0, 0))),
        _const_spec(gains.shape), _const_spec(wo.shape), _const_spec(wfi.shape), _const_spec(wfo.shape),
    ]
    args = [x, a_t, mod, gains, wo, wfi, wfo]
    scratch = [pltpu.VMEM((tm, FFN_HIDDEN), BF16)]
    if pool:
        u, wpool, pscale = pool_args
        nu = u.shape[2]
        per_tile = tm // POOL_HALO
        n_halo_blocks = s // POOL_HALO
        in_specs += [
            pl.BlockSpec((1, tm, nu), lambda i, t: (i, t, 0)),
            pl.BlockSpec((1, POOL_HALO, nu), lambda i, t: (i, jnp.maximum(t * per_tile - 1, 0), 0)),
            pl.BlockSpec((1, POOL_HALO, nu),
                         lambda i, t: (i, jnp.minimum((t + 1) * per_tile, n_halo_blocks - 1), 0)),
            _const_spec(wpool.shape), _const_spec(pscale.shape),
        ]
        args += [u, u, u, wpool, pscale]
        scratch.append(pltpu.VMEM((tm + 2 * POOL_HALO, nu), F32))
    return pl.pallas_call(
        functools.partial(_post_kernel, pool=pool, tm=tm, seq=s),
        grid=(b, s // tm),
        in_specs=in_specs,
        out_specs=pl.BlockSpec((1, tm, d), lambda i, t: (i, t, 0)),
        out_shape=jax.ShapeDtypeStruct((b, s, d), F32),
        scratch_shapes=scratch,
        compiler_params=_params(),
        name="post_even" if pool else "post_odd",
    )(*args)


def _pad_rows(a, rows):
    return jnp.pad(a, [(0, 0)] * (a.ndim - 2) + [(0, rows - a.shape[-2]), (0, 0)])


def kernel(x, c, ctx, c_ctx, w_mod, b_mod, g_pre_mix, g_post_mix, g_pre_ffn, g_post_ffn, we_in, we_out,
           we_q_gain, we_k_gain, we_pool, we_pool_scale, wo_in, wo_out, wo_sink, w_ffn_in, w_ffn_out):
    batch, seq, d = x.shape
    c_len = ctx.shape[1]
    depth = w_mod.shape[0]
    tm = TOKEN_TILE
    tm_ctx = c_len
    assert d == D_MODEL and w_ffn_out.shape[1] == FFN_HIDDEN and w_mod.shape[2] % MOD_COLS_TILE == 0
    assert seq % PRE_TILE == 0 and seq % TOKEN_TILE == 0 and seq % QUERY_TILE == 0 and seq % GRID_W == 0
    assert seq >= QUERY_SUB + 2 * WINDOW and c_len % min(c_len, PRE_SUB_TILE) == 0 and c_len % QUERY_SUB == 0
    assert max(POOL_WINDOWS) // 2 <= POOL_HALO

    cc = jnp.concatenate([c, c_ctx[None, :]], axis=0)
    cc = _pad_rows(cc, -(-(batch + 1) // SUBLANES) * SUBLANES)
    mods = _modulation(cc, w_mod, b_mod)

    cos, sin = _rope_tables(seq)
    _, lane_dim = _lane_slot_dim()
    head_mean = jnp.asarray(_head_mean_matrix(), BF16)

    for l in range(depth):
        even = l % 2 == 0
        i = l // 2
        with_ctx = l < depth - 1
        mod_x = _pad_rows(mods[l, :batch].reshape(batch, 6, d), SUBLANES)
        mod_c = _pad_rows(mods[l, batch:batch + 1].reshape(1, 6, d), SUBLANES)
        gains = _pad_rows(jnp.stack([g_pre_mix[l], g_post_mix[l], g_pre_ffn[l], g_post_ffn[l]]), SUBLANES)
        wfi = w_ffn_in[l].astype(BF16)
        wfo = w_ffn_out[l].astype(BF16)

        if even:
            n_heads, n_kv = 8, 2
            w_full, w_out = we_in[i], we_out[i].astype(BF16)
            norm_args = (head_mean, we_q_gain[i][lane_dim][None, :], we_k_gain[i][lane_dim][None, :])
            sink = None
            mode = "dense"
        else:
            n_heads, n_kv = 16, 4
            w_full, w_out = wo_in[i], wo_out[i].astype(BF16)
            norm_args = None
            sink = wo_sink[i]
            mode = "window"
        q_w, kv_w = n_heads * HEAD_DIM, n_kv * HEAD_DIM
        cols = np.concatenate([_q_cols(n_heads), _k_cols(n_kv, q_w), np.arange(q_w + kv_w, w_full.shape[1])])
        w_in = w_full[:, cols].astype(BF16)
        widths = (q_w, kv_w, kv_w, w_full.shape[1] - q_w - 2 * kv_w)

        lat = _pre(x, mod_x, gains, w_in, norm_args, (cos, sin), widths=widths, tm=PRE_TILE)
        con = _pre(ctx, mod_c, gains, w_in, norm_args, None, widths=widths, tm=tm_ctx)
        a_t = _attention(lat[0], (lat[1], lat[2]), (con[1], con[2]), sink, mode=mode, tq=QUERY_TILE)
        pool_w = (we_pool[i].astype(BF16), we_pool_scale[i][None, :]) if even else None
        x = _post(x, a_t, mod_x, gains, w_out, wfi, wfo, (lat[3],) + pool_w if even else None, tm=tm)
        if with_ctx:
            ac_t = _attention(con[0], None, (con[1], con[2]), sink, mode="ctx", tq=c_len)
            ctx = _post(ctx, ac_t, mod_c, gains, w_out, wfi, wfo,
                        (con[3],) + pool_w if even else None, tm=tm_ctx)
    return x
```
